```python
import math
import jax
import jax.numpy as jnp
from jax import lax
import numpy as np

D_MODEL = 1024
BATCH = 8
SEQ = 8192
DEPTH = 2
DEC_BATCH = 32
DEC_SEQ = 16
PAST_LEN = 4096

CHUNK = 64
Q_BLOCK = 128
HEAD_DIM = 64
N_HEADS_A = 6
N_HEADS_B = 6
N_HEADS_C = 4
DC_HALF = HEAD_DIM // 2
W_MIX = (N_HEADS_A + N_HEADS_B + N_HEADS_C) * HEAD_DIM
IDX_HEADS = 8
IDX_DIM = 32
TOPK_MAX = 256
NUM_BUCKETS = 32
MAX_DISTANCE = 128
D_FF = 2816
CONV_W = 3
EPS = 1e-6
COL_SIZES = (N_HEADS_A * HEAD_DIM,) * 3 + (IDX_HEADS * IDX_DIM, IDX_HEADS, IDX_DIM) + (N_HEADS_B * HEAD_DIM,) * 3 + (N_HEADS_C * HEAD_DIM,) * 3
D_IN = sum(COL_SIZES)

kernel_name = 'hybrid_streaming_encoder_step'


def _split_points():
    pts, acc = [], 0
    for s in COL_SIZES[:-1]:
        acc += s
        pts.append(acc)
    return pts


def rmsnorm(x, g):
    xf = x.astype(jnp.float32)
    y = xf * lax.rsqrt(jnp.mean(xf * xf, axis=-1, keepdims=True) + EPS)
    return (y * g.astype(jnp.float32)).astype(x.dtype)


def t5_bucket(rel):
    half = NUM_BUCKETS // 2
    max_exact = half // 2
    ret = jnp.where(rel > 0, half, 0)
    n = jnp.abs(rel)
    nf = jnp.maximum(n, 1).astype(jnp.float32)
    large = max_exact + (jnp.log(nf / max_exact) / math.log(MAX_DISTANCE / max_exact) * (half - max_exact)).astype(jnp.int32)
    large = jnp.minimum(large, half - 1)
    return ret + jnp.where(n < max_exact, n, large)


def chunk_visible(qpos, kpos):
    return (kpos[None, :] // CHUNK) <= (qpos[:, None] // CHUNK)


def dsa_block(qa, qi, wi, qpos, ka, va, ki, kpos, bias_tab, topk):
    vis = chunk_visible(qpos, kpos)
    dots = jnp.einsum('bqhd,bsd->bqhs', qi.astype(jnp.float32), ki.astype(jnp.float32))
    iscore = jnp.einsum('bqh,bqhs->bqs', wi.astype(jnp.float32), jax.nn.relu(dots))
    iscore = jnp.where(vis[None], iscore, -jnp.inf)
    top_val, top_idx = lax.top_k(iscore, topk)
    valid = jnp.isfinite(top_val)
    kg = jax.vmap(lambda k, i: k[i])(ka, top_idx)
    vg = jax.vmap(lambda v, i: v[i])(va, top_idx)
    rel = kpos[top_idx] - qpos[None, :, None]
    bias = bias_tab[t5_bucket(rel)].astype(jnp.float32).transpose(0, 1, 3, 2)
    logits = jnp.einsum('bqhd,bqkhd->bqhk', qa, kg).astype(jnp.float32) * HEAD_DIM ** -0.5 + bias
    logits = jnp.where(valid[:, :, None, :], logits, -jnp.inf)
    p = jax.nn.softmax(logits, axis=-1).astype(va.dtype)
    return jnp.einsum('bqhk,bqkhd->bqhd', p, vg)


def sb_block(qb, qpos, kb, vb, kpos):
    z = jnp.einsum('bqhd,bshd->bhqs', qb, kb).astype(jnp.float32) * HEAD_DIM ** -0.5
    causal = kpos[None, :] < qpos[:, None]
    log_1mb = jnp.where(causal, jax.nn.log_sigmoid(-z), 0.0)
    tail = lax.cumsum(log_1mb, axis=3, reverse=True) - log_1mb
    a = jnp.where(causal, jnp.exp(jax.nn.log_sigmoid(z) + tail), 0.0)
    return jnp.einsum('bhqs,bshd->bqhd', a.astype(vb.dtype), vb)


def diff_block(qc, qpos, kc, vc, kpos, bias_tab, lam):
    vis = chunk_visible(qpos, kpos)
    bias = bias_tab[t5_bucket(kpos[None, :] - qpos[:, None])].astype(jnp.float32)
    bias = bias.transpose(2, 0, 1)[None, :, None]
    logits = jnp.einsum('bqhmd,bshmd->bhmqs', qc, kc).astype(jnp.float32) * DC_HALF ** -0.5 + bias
    logits = jnp.where(vis, logits, -jnp.inf)
    p = jax.nn.softmax(logits, axis=-1)
    attn = p[:, :, 0] - lam * p[:, :, 1]
    return jnp.einsum('bhqs,bshd->bqhd', attn.astype(vc.dtype), vc)


def sweep_queries(fn, q_args):
    *arrs, qpos = q_args
    T = qpos.shape[0]
    if T <= Q_BLOCK or T % Q_BLOCK != 0:
        return fn(*arrs, qpos)
    nb = T // Q_BLOCK
    blk = [jnp.moveaxis(a.reshape(a.shape[0], nb, Q_BLOCK, *a.shape[2:]), 1, 0) for a in arrs]
    out = lax.map(lambda args: fn(*args[:-1], args[-1]), (*blk, qpos.reshape(nb, Q_BLOCK)))
    out = jnp.moveaxis(out, 0, 1)
    return out.reshape(out.shape[0], T, *out.shape[3:])


def lambda_init(l):
    return 0.8 - 0.6 * math.exp(-0.3 * l)


def trunk_layer(x, past, l, params):
    (w_in, w_out, rel_bias, lambda_q1, lambda_k1, lambda_q2, lambda_k2, subln_gain,
     g_pre_mix, g_post_mix, g_pre_ffn, g_post_ffn, w_up, conv_w, conv_b, w_down) = params
    pk_a, pv_a, pk_i, pk_b, pv_b, pk_c, pv_c, conv_state = past
    B, T, _ = x.shape
    P = pk_a.shape[1]
    S = P + T
    qpos = P + jnp.arange(T, dtype=jnp.int32)
    kpos = jnp.arange(S, dtype=jnp.int32)
    topk = min(TOPK_MAX, S // 4)

    h = rmsnorm(x, g_pre_mix[l])
    proj = h @ w_in[l]
    q_a, k_a, v_a, q_i, w_i, k_i, q_b, k_b, v_b, q_c, k_c, v_c = jnp.split(proj, _split_points(), axis=-1)
    q_a = q_a.reshape(B, T, N_HEADS_A, HEAD_DIM)
    k_a = k_a.reshape(B, T, N_HEADS_A, HEAD_DIM)
    v_a = v_a.reshape(B, T, N_HEADS_A, HEAD_DIM)
    q_i = q_i.reshape(B, T, IDX_HEADS, IDX_DIM)
    w_i = w_i * (IDX_HEADS ** -0.5 * IDX_DIM ** -0.5)
    q_b = q_b.reshape(B, T, N_HEADS_B, HEAD_DIM)
    k_b = k_b.reshape(B, T, N_HEADS_B, HEAD_DIM)
    v_b = v_b.reshape(B, T, N_HEADS_B, HEAD_DIM)
    q_c = q_c.reshape(B, T, N_HEADS_C, HEAD_DIM)
    k_c = k_c.reshape(B, T, N_HEADS_C, HEAD_DIM)
    v_c = v_c.reshape(B, T, N_HEADS_C, HEAD_DIM)

    ka = jnp.concatenate([pk_a, k_a], axis=1)
    va = jnp.concatenate([pv_a, v_a], axis=1)
    ki = jnp.concatenate([pk_i, k_i], axis=1)
    kb = jnp.concatenate([pk_b, k_b], axis=1)
    vb = jnp.concatenate([pv_b, v_b], axis=1)
    kc = jnp.concatenate([pk_c, k_c], axis=1).reshape(B, S, N_HEADS_C, 2, DC_HALF)
    vc = jnp.concatenate([pv_c, v_c], axis=1)
    bias_a = rel_bias[:, :N_HEADS_A]
    bias_c = rel_bias[:, N_HEADS_A:]

    o_a = sweep_queries(lambda qa_, qi_, wi_, qp_: dsa_block(qa_, qi_, wi_, qp_, ka, va, ki, kpos, bias_a, topk),
                        (q_a, q_i, w_i, qpos))
    o_b = sweep_queries(lambda qb_, qp_: sb_block(qb_, qp_, kb, vb, kpos), (q_b, qpos))
    lam0 = lambda_init(l)
    lam = (jnp.exp(jnp.sum(lambda_q1[l].astype(jnp.float32) * lambda_k1[l].astype(jnp.float32)))
           - jnp.exp(jnp.sum(lambda_q2[l].astype(jnp.float32) * lambda_k2[l].astype(jnp.float32))) + lam0)
    o_c = sweep_queries(lambda qc_, qp_: diff_block(qc_, qp_, kc, vc, kpos, bias_c, lam),
                        (q_c.reshape(B, T, N_HEADS_C, 2, DC_HALF), qpos))
    o_c = rmsnorm(o_c, subln_gain[l]) * (1.0 - lam0)

    mix = jnp.concatenate([o_a.reshape(B, T, -1), o_b.reshape(B, T, -1), o_c.reshape(B, T, -1)], axis=-1) @ w_out[l]
    x = x + rmsnorm(mix, g_post_mix[l])

    h = rmsnorm(x, g_pre_ffn[l])
    u = h @ w_up[l]
    u_ext = jnp.concatenate([conv_state, u], axis=1)
    c = conv_b[l] + sum(conv_w[l, j] * u_ext[:, j:j + T] for j in range(CONV_W))
    gate, val = jnp.split(c, 2, axis=-1)
    f = (jax.nn.gelu(gate, approximate=True) * val) @ w_down[l]
    x = x + rmsnorm(f, g_post_ffn[l])
    new_conv = u_ext[:, T:]
    return x, (k_a, v_a, k_i, k_b, v_b, k_c, v_c, new_conv)


def run_trunk(x, caches, params):
    news = []
    for l in range(DEPTH):
        x, new = trunk_layer(x, tuple(c[l] for c in caches), l, params)
        news.append(new)
    return x, [jnp.stack([n[i] for n in news]) for i in range(len(news[0]))]


def empty_caches(B, dtype):
    return (jnp.zeros((DEPTH, B, 0, N_HEADS_A, HEAD_DIM), dtype),
            jnp.zeros((DEPTH, B, 0, N_HEADS_A, HEAD_DIM), dtype),
            jnp.zeros((DEPTH, B, 0, IDX_DIM), dtype),
            jnp.zeros((DEPTH, B, 0, N_HEADS_B, HEAD_DIM), dtype),
            jnp.zeros((DEPTH, B, 0, N_HEADS_B, HEAD_DIM), dtype),
            jnp.zeros((DEPTH, B, 0, N_HEADS_C, HEAD_DIM), dtype),
            jnp.zeros((DEPTH, B, 0, N_HEADS_C, HEAD_DIM), dtype),
            jnp.zeros((DEPTH, B, CONV_W - 1, 2 * D_FF), dtype))


def setup_inputs(seed: int = 0) -> dict:
    key = jax.random.key(seed)
    ks = jax.random.split(key, 26)

    def nrm(k, shape, s=1.0):
        return s * jax.random.normal(k, shape, jnp.float32)

    def gain(k, n=D_MODEL):
        return 1.0 + nrm(k, (DEPTH, n), 0.05)

    return {
        'x_prompt': nrm(ks[0], (BATCH, SEQ, D_MODEL)),
        'x_sample': nrm(ks[1], (DEC_BATCH, DEC_SEQ, D_MODEL)),
        'cache_k_a': nrm(ks[2], (DEPTH, DEC_BATCH, PAST_LEN, N_HEADS_A, HEAD_DIM)),
        'cache_v_a': nrm(ks[3], (DEPTH, DEC_BATCH, PAST_LEN, N_HEADS_A, HEAD_DIM)),
        'cache_idx_k': nrm(ks[4], (DEPTH, DEC_BATCH, PAST_LEN, IDX_DIM)),
        'cache_k_b': nrm(ks[5], (DEPTH, DEC_BATCH, PAST_LEN, N_HEADS_B, HEAD_DIM)),
        'cache_v_b': nrm(ks[6], (DEPTH, DEC_BATCH, PAST_LEN, N_HEADS_B, HEAD_DIM)),
        'cache_k_c': nrm(ks[7], (DEPTH, DEC_BATCH, PAST_LEN, N_HEADS_C, HEAD_DIM)),
        'cache_v_c': nrm(ks[8], (DEPTH, DEC_BATCH, PAST_LEN, N_HEADS_C, HEAD_DIM)),
        'state_ffn_conv': nrm(ks[9], (DEPTH, DEC_BATCH, CONV_W - 1, 2 * D_FF)),
        'w_in': nrm(ks[10], (DEPTH, D_MODEL, D_IN), D_MODEL ** -0.5),
        'w_out': nrm(ks[11], (DEPTH, W_MIX, D_MODEL), W_MIX ** -0.5),
        'rel_bias': nrm(ks[12], (NUM_BUCKETS, N_HEADS_A + N_HEADS_C), 0.5),
        'lambda_q1': nrm(ks[13], (DEPTH, DC_HALF), 0.1),
        'lambda_k1': nrm(ks[14], (DEPTH, DC_HALF), 0.1),
        'lambda_q2': nrm(ks[15], (DEPTH, DC_HALF), 0.1),
        'lambda_k2': nrm(ks[16], (DEPTH, DC_HALF), 0.1),
        'subln_gain': gain(ks[17], HEAD_DIM),
        'g_pre_mix': gain(ks[18]),
        'g_post_mix': gain(ks[19]),
        'g_pre_ffn': gain(ks[20]),
        'g_post_ffn': gain(ks[21]),
        'w_up': nrm(ks[22], (DEPTH, D_MODEL, 2 * D_FF), D_MODEL ** -0.5),
        'conv_w': nrm(ks[23], (DEPTH, CONV_W, 2 * D_FF), 0.5),
        'conv_b': nrm(ks[24], (DEPTH, 2 * D_FF), 0.01),
        'w_down': nrm(ks[25], (DEPTH, D_FF, D_MODEL), D_FF ** -0.5),
    }


def reference(x_prompt, x_sample, cache_k_a, cache_v_a, cache_idx_k, cache_k_b, cache_v_b,
              cache_k_c, cache_v_c, state_ffn_conv, w_in, w_out, rel_bias,
              lambda_q1, lambda_k1, lambda_q2, lambda_k2, subln_gain,
              g_pre_mix, g_post_mix, g_pre_ffn, g_post_ffn, w_up, conv_w, conv_b, w_down):
    params = (w_in, w_out, rel_bias, lambda_q1, lambda_k1, lambda_q2, lambda_k2, subln_gain,
              g_pre_mix, g_post_mix, g_pre_ffn, g_post_ffn, w_up, conv_w, conv_b, w_down)
    y_prompt, (p_k_a, p_v_a, p_idx_k, p_k_b, p_v_b, p_k_c, p_v_c, p_conv) = run_trunk(
        x_prompt, empty_caches(x_prompt.shape[0], x_prompt.dtype), params)
    sample_caches = (cache_k_a, cache_v_a, cache_idx_k, cache_k_b, cache_v_b, cache_k_c, cache_v_c, state_ffn_conv)
    y_sample, (s_k_a, s_v_a, s_idx_k, s_k_b, s_v_b, s_k_c, s_v_c, s_conv) = run_trunk(x_sample, sample_caches, params)
    return (y_prompt, y_sample,
            p_k_a, p_v_a, p_idx_k, p_k_b, p_v_b, p_k_c, p_v_c, p_conv,
            s_k_a, s_v_a, s_idx_k, s_k_b, s_v_b, s_k_c, s_v_c, s_conv)
```

```python
import functools
import math

import numpy as np
import jax
import jax.numpy as jnp
from jax import lax
from jax.experimental import pallas as pl
from jax.experimental.pallas import tpu as pltpu

F32 = jnp.float32
BF16 = jnp.bfloat16
I32 = jnp.int32

LANES = 128
VMEM_LIMIT = 56 * 1024 * 1024

CHUNK = 64
HEAD_DIM = 64
N_HEADS_A = 6
N_HEADS_B = 6
N_HEADS_C = 4
DC_HALF = HEAD_DIM // 2
IDX_HEADS = 8
IDX_DIM = 32
TOPK_MAX = 256
NUM_BUCKETS = 32
MAX_DISTANCE = 128
CONV_W = 3
EPS = 1e-6

WA = N_HEADS_A * HEAD_DIM
WB = N_HEADS_B * HEAD_DIM
WC = N_HEADS_C * HEAD_DIM
WQI = IDX_HEADS * IDX_DIM

NEG = -1e30
INT_MIN = -2 ** 31
KEY_NEG_INF = int(np.int32(np.float32(-np.inf).view(np.int32)) ^ np.int32(0x7FFFFFFF))
EXP_ZERO = -104.0

KB = 128
NT_DIMS = (((1,), (1,)), ((), ()))


def _cparams(sem):
    return pltpu.CompilerParams(dimension_semantics=sem, vmem_limit_bytes=VMEM_LIMIT)


def _dot_nt(a, b):
    return lax.dot_general(a, b, NT_DIMS, preferred_element_type=F32)


def _dot(a, b):
    return jnp.dot(a, b, preferred_element_type=F32)


_PROJ_GROUPS = (
    ("qa", 0, WA, HEAD_DIM ** -0.5, False, True),
    ("ka", 384, WA, 1.0, True, True),
    ("va", 768, WA, 1.0, True, True),
    ("qi", 1152, WQI, 1.0, True, False),
    ("misc", 1408, LANES, 1.0, True, True),
    ("qb", 1536, WB, HEAD_DIM ** -0.5, False, True),
    ("kb", 1920, WB, 1.0, True, True),
    ("vb", 2304, WB, 1.0, True, True),
    ("qc", 2688, WC, 1.0, False, True),
    ("kc", 2944, WC, 1.0, True, True),
    ("vc", 3200, WC, 1.0, True, True),
)
_PROJ_NPAD = 3456


def _proj_out_names():
    names = []
    for name, _, _, _, f32o, bf16o in _PROJ_GROUPS:
        if f32o:
            names.append(name + "_f32")
        if bf16o:
            names.append(name + "_bf16")
    return names


def _proj_kernel(x_ref, g_ref, w_ref, *out_refs):
    x = x_ref[...]
    ms = jnp.mean(x * x, axis=-1, keepdims=True)
    h = (x * lax.rsqrt(ms + EPS) * g_ref[...]).astype(BF16)
    k = 0
    for _, off, width, scale, f32o, bf16o in _PROJ_GROUPS:
        r = _dot(h, w_ref[:, off:off + width])
        if f32o:
            out_refs[k][...] = r
            k += 1
        if bf16o:
            out_refs[k][...] = (r * scale).astype(BF16) if scale != 1.0 else r.astype(BF16)
            k += 1


def _pad_w_in(w):
    d = w.shape[0]
    misc = jnp.concatenate([w[:, 1416:1448], w[:, 1408:1416],
                            jnp.zeros((d, LANES - IDX_DIM - IDX_HEADS), w.dtype)], axis=1)
    return jnp.concatenate([w[:, 0:1408], misc, w[:, 1448:]], axis=1).astype(BF16)


def _proj(x2d, g, w_pad):
    m, d = x2d.shape
    tm = min(512, m)
    assert m % tm == 0
    out_shapes, out_specs = [], []
    for _, _, width, _, f32o, bf16o in _PROJ_GROUPS:
        for want, dt in ((f32o, F32), (bf16o, BF16)):
            if want:
                out_shapes.append(jax.ShapeDtypeStruct((m, width), dt))
                out_specs.append(pl.BlockSpec((tm, width), lambda i: (i, 0)))
    outs = pl.pallas_call(
        _proj_kernel,
        grid=(m // tm,),
        in_specs=[pl.BlockSpec((tm, d), lambda i: (i, 0)),
                  pl.BlockSpec((1, d), lambda i: (0, 0)),
                  pl.BlockSpec((d, _PROJ_NPAD), lambda i: (0, 0))],
        out_specs=out_specs,
        out_shape=out_shapes,
        compiler_params=_cparams(("parallel",)),
        name="proj",
    )(x2d, g.reshape(1, d), w_pad)
    return dict(zip(_proj_out_names(), outs))


def _t5_bucket(rel):
    half = NUM_BUCKETS // 2
    max_exact = half // 2
    ret = jnp.where(rel > 0, half, 0)
    n = jnp.abs(rel)
    nf = jnp.maximum(n, 1).astype(F32)
    large = max_exact + (jnp.log(nf / max_exact) / math.log(MAX_DISTANCE / max_exact)
                         * (half - max_exact)).astype(I32)
    large = jnp.minimum(large, half - 1)
    return ret + jnp.where(n < max_exact, n, large)


def _bias_tiles(tab, qb, n_valid):
    qq = jnp.arange(qb, dtype=I32)[:, None]
    kk = jnp.arange(KB, dtype=I32)[None, :]
    tiles = []
    for d in (0, 1):
        rel = (kk - d * KB) - qq
        tiles.append(tab[_t5_bucket(rel)].astype(F32).transpose(2, 0, 1))
    vis = ((kk // CHUNK) <= (qq // CHUNK)) & (kk < n_valid)
    tiles[0] = jnp.where(vis[None], tiles[0], NEG)
    near = jnp.stack(tiles, axis=1)
    far_row = tab[_t5_bucket(jnp.array(-MAX_DISTANCE, I32))].astype(F32)
    far = jnp.broadcast_to(far_row[:, None, None], (tab.shape[1], 1, KB))
    return near, far


def _softmax_update(s, v, m_ref, l_ref, acc_ref, idx):
    m_old = m_ref[idx]
    m_new = jnp.maximum(m_old, jnp.max(s, axis=1, keepdims=True))
    alpha = jnp.exp(m_old - m_new)
    p = jnp.exp(s - m_new)
    l_ref[idx] = alpha * l_ref[idx] + jnp.sum(p, axis=1, keepdims=True)
    acc_ref[idx] = alpha * acc_ref[idx] + _dot(p.astype(BF16), v)
    m_ref[idx] = m_new


def _diff_kernel(lam_ref, gain_ref, bnear_ref, bfar_ref, q_ref, k_ref, v_ref, o_ref,
                 m_ref, l_ref, acc_ref, *, qb, qb0, lam0):
    qbi = qb0 + pl.program_id(2)
    q = q_ref[0]
    lane = lax.broadcasted_iota(I32, (qb, LANES), 1)
    qz = [jnp.where((lane >= DC_HALF * c) & (lane < DC_HALF * (c + 1)), q, jnp.zeros_like(q))
          for c in range(4)]
    m_ref[...] = jnp.full(m_ref.shape, NEG, F32)
    l_ref[...] = jnp.zeros(l_ref.shape, F32)
    acc_ref[...] = jnp.zeros(acc_ref.shape, F32)
    scale = DC_HALF ** -0.5

    def step(j, bias_of_head):
        off = pl.multiple_of(j * KB, KB)
        k = k_ref[0, pl.ds(off, KB), :]
        v = v_ref[0, pl.ds(off, KB), :]
        for c in range(4):
            s = _dot_nt(qz[c], k) * scale + bias_of_head(c // 2)
            _softmax_update(s, v, m_ref, l_ref, acc_ref, c)

    def far_body(j, carry):
        step(j, lambda hh: bfar_ref[hh])
        return carry

    lax.fori_loop(0, qbi - 1, far_body, 0)

    @pl.when(qbi >= 1)
    def _():
        step(qbi - 1, lambda hh: bnear_ref[hh, 1])

    step(qbi, lambda hh: bnear_ref[hh, 0])

    lam_v = lam_ref[...]
    s1 = jnp.sum(lam_v[0:1] * lam_v[1:2], axis=1, keepdims=True)
    s2 = jnp.sum(lam_v[2:3] * lam_v[3:4], axis=1, keepdims=True)
    lam = jnp.exp(s1) - jnp.exp(s2) + lam0
    o0 = acc_ref[0] / l_ref[0] - lam * (acc_ref[1] / l_ref[1])
    o1 = acc_ref[2] / l_ref[2] - lam * (acc_ref[3] / l_ref[3])
    first = lane < HEAD_DIM
    o = jnp.where(first, o0, o1)
    sq = o * o
    ms0 = jnp.sum(jnp.where(first, sq, 0.0), axis=1, keepdims=True)
    ms1 = jnp.sum(jnp.where(first, 0.0, sq), axis=1, keepdims=True)
    ms = jnp.where(first, ms0, ms1) * (1.0 / HEAD_DIM)
    y = o * lax.rsqrt(ms + EPS) * gain_ref[...]
    o_ref[0] = (y * (1.0 - lam0)).astype(o_ref.dtype)


def _diff_attn(q, k, v, bnear, bfar, lamv, gain2, *, qb, qb0, lam0):
    b, t, _ = q.shape
    s_pad = k.shape[1]
    nq = t // qb
    npair = N_HEADS_C // 2
    kern = functools.partial(_diff_kernel, qb=qb, qb0=qb0, lam0=lam0)
    return pl.pallas_call(
        kern,
        grid=(b, npair, nq),
        in_specs=[
            pl.BlockSpec((4, LANES), lambda bi, p, i: (0, 0)),
            pl.BlockSpec((1, LANES), lambda bi, p, i: (0, 0)),
            pl.BlockSpec((2, 2, qb, KB), lambda bi, p, i: (p, 0, 0, 0)),
            pl.BlockSpec((2, 1, KB), lambda bi, p, i: (p, 0, 0)),
            pl.BlockSpec((1, qb, LANES), lambda bi, p, i: (bi, i, p)),
            pl.BlockSpec((1, s_pad, LANES), lambda bi, p, i: (bi, 0, p)),
            pl.BlockSpec((1, s_pad, LANES), lambda bi, p, i: (bi, 0, p)),
        ],
        out_specs=pl.BlockSpec((1, qb, LANES), lambda bi, p, i: (bi, i, p)),
        out_shape=jax.ShapeDtypeStruct((b, t, WC), BF16),
        scratch_shapes=[pltpu.VMEM((4, qb, 1), F32), pltpu.VMEM((4, qb, 1), F32),
                        pltpu.VMEM((4, qb, LANES), F32)],
        compiler_params=_cparams(("parallel", "parallel", "arbitrary")),
        name="diff_attn",
    )(lamv, gain2, bnear, bfar, q, k, v)


def _sb_kernel(q_ref, k_ref, v_ref, o_ref, carry_ref, acc_ref, *, qb, qb0):
    qbi = qb0 + pl.program_id(2)
    q = q_ref[0]
    lane = lax.broadcasted_iota(I32, (qb, LANES), 1)
    causal = lax.broadcasted_iota(I32, (qb, KB), 1) < lax.broadcasted_iota(I32, (qb, KB), 0)
    tri = jnp.where(lax.broadcasted_iota(I32, (KB, KB), 0) > lax.broadcasted_iota(I32, (KB, KB), 1),
                    1.0, 0.0).astype(BF16)
    outs = []
    for hh in range(2):
        qz = jnp.where((lane >= HEAD_DIM * hh) & (lane < HEAD_DIM * (hh + 1)), q, jnp.zeros_like(q))
        carry_ref[...] = jnp.zeros(carry_ref.shape, F32)
        acc_ref[...] = jnp.zeros(acc_ref.shape, F32)

        def block(j, masked, qz=qz):
            off = pl.multiple_of(j * KB, KB)
            k = k_ref[0, pl.ds(off, KB), :]
            v = v_ref[0, pl.ds(off, KB), :]
            z = _dot_nt(qz, k)
            sp = jnp.maximum(z, 0.0) + jnp.log1p(jnp.exp(-jnp.abs(z)))
            lg = -sp
            if masked:
                lg = jnp.where(causal, lg, 0.0)
            l1 = lg.astype(BF16)
            r1 = lg - l1.astype(F32)
            l2 = r1.astype(BF16)
            l3 = (r1 - l2.astype(F32)).astype(BF16)
            tail = _dot(l1, tri) + _dot(l2, tri) + _dot(l3, tri)
            carry = carry_ref[...]
            a = jnp.exp(z - sp + tail + carry)
            if masked:
                a = jnp.where(causal, a, 0.0)
            acc_ref[...] += _dot(a.astype(BF16), v)
            carry_new = carry + jnp.sum(lg, axis=1, keepdims=True)
            carry_ref[...] = carry_new
            return jnp.max(carry_new)

        mx0 = block(qbi, True)

        def cond(st):
            return jnp.logical_and(st[0] >= 0, st[1] > EXP_ZERO)

        def body(st):
            return st[0] - 1, block(st[0], False)

        lax.while_loop(cond, body, (qbi - 1, mx0))
        outs.append(acc_ref[...])
    o_ref[0] = jnp.where(lane < HEAD_DIM, outs[0], outs[1]).astype(o_ref.dtype)


def _sb_attn(q, k, v, *, qb, qb0):
    b, t, _ = q.shape
    s_pad = k.shape[1]
    nq = t // qb
    kern = functools.partial(_sb_kernel, qb=qb, qb0=qb0)
    return pl.pallas_call(
        kern,
        grid=(b, N_HEADS_B // 2, nq),
        in_specs=[
            pl.BlockSpec((1, qb, LANES), lambda bi, p, i: (bi, i, p)),
            pl.BlockSpec((1, s_pad, LANES), lambda bi, p, i: (bi, 0, p)),
            pl.BlockSpec((1, s_pad, LANES), lambda bi, p, i: (bi, 0, p)),
        ],
        out_specs=pl.BlockSpec((1, qb, LANES), lambda bi, p, i: (bi, i, p)),
        out_shape=jax.ShapeDtypeStruct((b, t, WB), BF16),
        scratch_shapes=[pltpu.VMEM((qb, 1), F32), pltpu.VMEM((qb, LANES), F32)],
        compiler_params=_cparams(("parallel", "parallel", "arbitrary")),
        name="sb_attn",
    )(q, k, v)


def _dsa_kernel(bnear_ref, bfar_ref, qa_ref, qi_ref, wi_ref, ka_ref, va_ref, ki_ref, o_ref,
                kx_ref, wb_ref, qiz_ref, qaz_ref, cs_ref, m_ref, l_ref, acc_ref,
                *, qb, qb0, topk, idx_bits):
    qbi = qb0 + pl.program_id(1)
    lane = lax.broadcasted_iota(I32, (qb, LANES), 1)
    col = lax.broadcasted_iota(I32, (qb, KB), 1)

    qi = qi_ref[0]
    for h in range(IDX_HEADS):
        g, sh = divmod(h * IDX_DIM, LANES)
        part = qi[:, g * LANES:(g + 1) * LANES]
        if sh:
            part = pltpu.roll(part, LANES - sh, 1)
        qiz_ref[h] = jnp.where(lane < IDX_DIM, part, 0.0).astype(BF16)
    wi = wi_ref[0] * (IDX_HEADS ** -0.5 * IDX_DIM ** -0.5)
    for h in range(IDX_HEADS):
        wb_ref[h] = jnp.broadcast_to(wi[:, IDX_DIM + h:IDX_DIM + h + 1], (qb, KB))
    qa = qa_ref[0]
    for h in range(N_HEADS_A):
        part = qa[:, (h // 2) * LANES:(h // 2 + 1) * LANES]
        hh = h % 2
        qaz_ref[h] = jnp.where((lane >= HEAD_DIM * hh) & (lane < HEAD_DIM * (hh + 1)),
                               part, jnp.zeros_like(part))

    def score_block(j, diag):
        off = pl.multiple_of(j * KB, KB)
        ki = ki_ref[0, pl.ds(off, KB), :]
        sc = jnp.zeros((qb, KB), F32)
        for h in range(IDX_HEADS):
            sc = sc + wb_ref[h] * jnp.maximum(_dot_nt(qiz_ref[h], ki), 0.0)
        sc = jnp.where(sc == 0.0, 0.0, sc)
        bits = lax.bitcast_convert_type(sc, I32)
        key = bits ^ ((bits >> 31) & 0x7FFFFFFF)
        if diag:
            key = jnp.where(bnear_ref[0, 0] > 0.5 * NEG, key, KEY_NEG_INF)
        kx_ref[j] = key

    def score_body(j, carry):
        score_block(j, False)
        return carry

    lax.fori_loop(0, qbi, score_body, 0)
    score_block(qbi, True)

    def count(pred):
        def body(j, acc):
            return acc + jnp.where(pred(kx_ref[j], col + j * KB), 1.0, 0.0)
        acc = lax.fori_loop(0, qbi + 1, body, jnp.zeros((qb, KB), F32))
        return jnp.sum(acc, axis=1, keepdims=True)

    wide = lambda a: jnp.broadcast_to(a, (qb, KB))

    def bit_body(t, r):
        cand = r + lax.shift_left(jnp.int32(1), 31 - t)
        cand_w = wide(cand)
        cnt = count(lambda kx, idx: kx >= cand_w)
        return jnp.where(cnt >= topk, cand, r)

    r1 = lax.fori_loop(0, 32, bit_body, jnp.full((qb, 1), INT_MIN, I32))
    r = wide(r1)

    need = topk - count(lambda kx, idx: kx > r)
    n_eq = count(lambda kx, idx: kx == r)
    cs_ref[...] = jnp.full((qb, 1), 2 ** idx_bits - 1, I32)
    overflow = jnp.where((n_eq > need) & (r1 > KEY_NEG_INF), 1.0, 0.0)

    @pl.when(jnp.max(overflow) > 0.0)
    def _():
        def tie_body(t, c):
            cand = c + lax.shift_left(jnp.int32(1), idx_bits - 1 - t)
            cand_w = wide(cand)
            cnt = count(lambda kx, idx: (kx == r) & (idx < cand_w))
            return jnp.where(cnt < need, cand, c)
        cs_ref[...] = lax.fori_loop(0, idx_bits, tie_body, jnp.zeros((qb, 1), I32))

    cs = wide(cs_ref[...])

    m_ref[...] = jnp.full(m_ref.shape, NEG, F32)
    l_ref[...] = jnp.zeros(l_ref.shape, F32)
    acc_ref[...] = jnp.zeros(acc_ref.shape, F32)

    def att_block(j, bias_of_head):
        off = pl.multiple_of(j * KB, KB)
        kx = kx_ref[j]
        idx = col + j * KB
        sel = (kx > r) | ((kx == r) & (idx <= cs))
        sel = sel & (kx > KEY_NEG_INF)
        madd = jnp.where(sel, 0.0, NEG)
        k = ka_ref[0, pl.ds(off, KB), :]
        v = va_ref[0, pl.ds(off, KB), :]
        for h in range(N_HEADS_A):
            p = h // 2
            s = _dot_nt(qaz_ref[h], k[:, p * LANES:(p + 1) * LANES]) + bias_of_head(h) + madd
            _softmax_update(s, v[:, p * LANES:(p + 1) * LANES], m_ref, l_ref, acc_ref, h)

    def far_body(j, carry):
        att_block(j, lambda h: bfar_ref[h])
        return carry

    lax.fori_loop(0, qbi - 1, far_body, 0)

    @pl.when(qbi >= 1)
    def _():
        att_block(qbi - 1, lambda h: bnear_ref[h, 1])

    att_block(qbi, lambda h: bnear_ref[h, 0])

    for p in range(N_HEADS_A // 2):
        o = jnp.where(lane < HEAD_DIM, acc_ref[2 * p] / l_ref[2 * p],
                      acc_ref[2 * p + 1] / l_ref[2 * p + 1])
        o_ref[0, :, p * LANES:(p + 1) * LANES] = o.astype(o_ref.dtype)


def _dsa_attn(qa, qi, misc_f32, ka, va, ki, bnear, bfar, *, qb, qb0, topk):
    b, t, _ = qa.shape
    s_pad = ka.shape[1]
    nq = t // qb
    nkb = s_pad // KB
    idx_bits = int(math.ceil(math.log2(s_pad))) + 1
    kern = functools.partial(_dsa_kernel, qb=qb, qb0=qb0, topk=float(topk), idx_bits=idx_bits)
    return pl.pallas_call(
        kern,
        grid=(b, nq),
        in_specs=[
            pl.BlockSpec((N_HEADS_A, 2, qb, KB), lambda bi, i: (0, 0, 0, 0)),
            pl.BlockSpec((N_HEADS_A, 1, KB), lambda bi, i: (0, 0, 0)),
            pl.BlockSpec((1, qb, WA), lambda bi, i: (bi, i, 0)),
            pl.BlockSpec((1, qb, WQI), lambda bi, i: (bi, i, 0)),
            pl.BlockSpec((1, qb, LANES), lambda bi, i: (bi, i, 0)),
            pl.BlockSpec((1, s_pad, WA), lambda bi, i: (bi, 0, 0)),
            pl.BlockSpec((1, s_pad, WA), lambda bi, i: (bi, 0, 0)),
            pl.BlockSpec((1, s_pad, LANES), lambda bi, i: (bi, 0, 0)),
        ],
        out_specs=pl.BlockSpec((1, qb, WA), lambda bi, i: (bi, i, 0)),
        out_shape=jax.ShapeDtypeStruct((b, t, WA), BF16),
        scratch_shapes=[
            pltpu.VMEM((nkb, qb, KB), I32),
            pltpu.VMEM((IDX_HEADS, qb, KB), F32),
            pltpu.VMEM((IDX_HEADS, qb, LANES), BF16),
            pltpu.VMEM((N_HEADS_A, qb, LANES), BF16),
            pltpu.VMEM((qb, 1), I32),
            pltpu.VMEM((N_HEADS_A, qb, 1), F32),
            pltpu.VMEM((N_HEADS_A, qb, 1), F32),
            pltpu.VMEM((N_HEADS_A, qb, LANES), F32),
        ],
        compiler_params=_cparams(("parallel", "arbitrary")),
        name="dsa_attn",
    )(bnear, bfar, qa, qi, misc_f32, ka, va, ki)


def _outproj_kernel(oa_ref, ob_ref, oc_ref, wa_ref, wb_ref, wc_ref, g_ref, x_ref, y_ref):
    mix = _dot(oa_ref[...], wa_ref[...]) + _dot(ob_ref[...], wb_ref[...]) + _dot(oc_ref[...], wc_ref[...])
    ms = jnp.mean(mix * mix, axis=-1, keepdims=True)
    y_ref[...] = x_ref[...] + mix * lax.rsqrt(ms + EPS) * g_ref[...]


def _outproj(oa, ob, oc, w_out, g, x2d):
    m, d = x2d.shape
    tm = min(512, m)
    w = w_out.astype(BF16)
    row = lambda i: (i, 0)
    const = lambda i: (0, 0)
    return pl.pallas_call(
        _outproj_kernel,
        grid=(m // tm,),
        in_specs=[pl.BlockSpec((tm, WA), row), pl.BlockSpec((tm, WB), row), pl.BlockSpec((tm, WC), row),
                  pl.BlockSpec((WA, d), const), pl.BlockSpec((WB, d), const), pl.BlockSpec((WC, d), const),
                  pl.BlockSpec((1, d), const), pl.BlockSpec((tm, d), row)],
        out_specs=pl.BlockSpec((tm, d), row),
        out_shape=jax.ShapeDtypeStruct((m, d), F32),
        compiler_params=_cparams(("parallel",)),
        name="outproj",
    )(oa, ob, oc, w[:WA], w[WA:WA + WB], w[WA + WB:], g.reshape(1, d), x2d)


HALO = 16


def _ffn_kernel(x_ref, xh_ref, stg_ref, stv_ref, gpre_ref, gpost_ref, wg_ref, wv_ref, cwg_ref,
                cwv_ref, cbg_ref, cbv_ref, wd_ref, y_ref, convg_ref, convv_ref,
                h_ref, ug_ref, uv_ref, f_ref, *, tm, tiles_per_seq):
    i = pl.program_id(0)
    f = pl.program_id(1)
    first_of_seq = (i % tiles_per_seq) == 0

    def norm(x):
        ms = jnp.mean(x * x, axis=-1, keepdims=True)
        return (x * lax.rsqrt(ms + EPS) * gpre_ref[...]).astype(BF16)

    @pl.when(f == 0)
    def _():
        h_ref[0:HALO, :] = norm(xh_ref[...])
        h_ref[HALO:, :] = norm(x_ref[...])
        f_ref[...] = jnp.zeros(f_ref.shape, F32)

    h = h_ref[...]
    ug_ref[...] = _dot(h, wg_ref[...])
    uv_ref[...] = _dot(h, wv_ref[...])

    @pl.when(first_of_seq)
    def _():
        ug_ref[0:HALO, :] = stg_ref[0]
        uv_ref[0:HALO, :] = stv_ref[0]

    def conv(u_ref, cw_ref, cb_ref):
        u = u_ref[...]
        u1 = pltpu.roll(u, 1, 0)[HALO:, :]
        u2 = pltpu.roll(u, 2, 0)[HALO:, :]
        cw = cw_ref[...]
        return cb_ref[...] + (cw[0:1] * u2 + cw[1:2] * u1 + cw[2:3] * u[HALO:, :])

    gate = conv(ug_ref, cwg_ref, cbg_ref)
    val = conv(uv_ref, cwv_ref, cbv_ref)
    c0 = math.sqrt(2.0 / math.pi)
    gelu = 0.5 * gate * (1.0 + jnp.tanh(c0 * (gate + 0.044715 * (gate * gate * gate))))
    f_ref[...] += _dot((gelu * val).astype(BF16), wd_ref[...])

    convg_ref[0] = ug_ref[tm:, :]
    convv_ref[0] = uv_ref[tm:, :]

    @pl.when(f == pl.num_programs(1) - 1)
    def _():
        ff = f_ref[...]
        ms = jnp.mean(ff * ff, axis=-1, keepdims=True)
        y_ref[...] = x_ref[...] + ff * lax.rsqrt(ms + EPS) * gpost_ref[...]


def _ffn(x2d, state, g_pre, g_post, w_up, conv_w, conv_b, w_down, *, seq):
    m, d = x2d.shape
    b = m // seq
    d_ff = w_down.shape[0]
    tn = 256
    assert d_ff % tn == 0
    nf = d_ff // tn
    tm = min(512, seq)
    assert seq % tm == 0 and tm % HALO == 0
    tiles_per_seq = seq // tm
    st = jnp.pad(state.astype(F32), ((0, 0), (HALO - (CONV_W - 1), 0), (0, 0)))
    wu = w_up.astype(BF16)
    wd = w_down.astype(BF16)
    kern = functools.partial(_ffn_kernel, tm=tm, tiles_per_seq=tiles_per_seq)
    halo_blocks = tm // HALO
    y, conv_g, conv_v = pl.pallas_call(
        kern,
        grid=(m // tm, nf),
        in_specs=[
            pl.BlockSpec((tm, d), lambda i, f: (i, 0)),
            pl.BlockSpec((HALO, d), lambda i, f: (jnp.maximum(i * halo_blocks - 1, 0), 0)),
            pl.BlockSpec((1, HALO, tn), lambda i, f: (i // tiles_per_seq, 0, f)),
            pl.BlockSpec((1, HALO, tn), lambda i, f: (i // tiles_per_seq, 0, nf + f)),
            pl.BlockSpec((1, d), lambda i, f: (0, 0)),
            pl.BlockSpec((1, d), lambda i, f: (0, 0)),
            pl.BlockSpec((d, tn), lambda i, f: (0, f)),
            pl.BlockSpec((d, tn), lambda i, f: (0, nf + f)),
            pl.BlockSpec((CONV_W, tn), lambda i, f: (0, f)),
            pl.BlockSpec((CONV_W, tn), lambda i, f: (0, nf + f)),
            pl.BlockSpec((1, tn), lambda i, f: (0, f)),
            pl.BlockSpec((1, tn), lambda i, f: (0, nf + f)),
            pl.BlockSpec((tn, d), lambda i, f: (f, 0)),
        ],
        out_specs=[pl.BlockSpec((tm, d), lambda i, f: (i, 0)),
                   pl.BlockSpec((1, HALO, tn), lambda i, f: (i // tiles_per_seq, 0, f)),
                   pl.BlockSpec((1, HALO, tn), lambda i, f: (i // tiles_per_seq, 0, f))],
        out_shape=[jax.ShapeDtypeStruct((m, d), F32),
                   jax.ShapeDtypeStruct((b, HALO, d_ff), F32),
                   jax.ShapeDtypeStruct((b, HALO, d_ff), F32)],
        scratch_shapes=[pltpu.VMEM((HALO + tm, d), BF16),
                        pltpu.VMEM((HALO + tm, tn), F32),
                        pltpu.VMEM((HALO + tm, tn), F32),
                        pltpu.VMEM((tm, d), F32)],
        compiler_params=_cparams(("parallel", "arbitrary")),
        name="ffn",
    )(x2d, x2d, st, st, g_pre.reshape(1, d), g_post.reshape(1, d), wu, wu,
      conv_w, conv_w, conv_b.reshape(1, -1), conv_b.reshape(1, -1), wd)
    keep = slice(HALO - (CONV_W - 1), HALO)
    return y, jnp.concatenate([conv_g[:, keep], conv_v[:, keep]], axis=-1)


def _lambda_init(l):
    return 0.8 - 0.6 * math.exp(-0.3 * l)


def _with_past(past, new, lanes=None):
    b, t, w = new.shape
    if past is None:
        return new
    p = past.shape[1]
    pk = past.reshape(b, p, -1).astype(BF16)
    if pk.shape[2] < w:
        pk = jnp.pad(pk, ((0, 0), (0, 0), (0, w - pk.shape[2])))
    s = p + t
    s_pad = -(-s // KB) * KB
    return jnp.concatenate([pk, new, jnp.zeros((b, s_pad - s, w), BF16)], axis=1)


def _layer(x, past, l, prm):
    b, t, d = x.shape
    p_len = 0 if past is None else past["k_a"].shape[1]
    s = p_len + t
    qb = min(KB, t)
    assert p_len % KB == 0 and t % qb == 0 and (qb == KB or t == qb)
    qb0 = p_len // KB
    s_pad = -(-s // KB) * KB
    n_valid = s - (s_pad - KB)
    topk = min(TOPK_MAX, s // 4)
    m = b * t

    pr = _proj(x.reshape(m, d), prm["g_pre_mix"][l], _pad_w_in(prm["w_in"][l]))
    r3 = lambda a: a.reshape(b, t, a.shape[-1])
    get = lambda name: None if past is None else past[name]

    bias_a_near, bias_a_far = _bias_tiles(prm["rel_bias"][:, :N_HEADS_A], qb, n_valid)
    bias_c_near, bias_c_far = _bias_tiles(prm["rel_bias"][:, N_HEADS_A:], qb, n_valid)

    o_a = _dsa_attn(r3(pr["qa_bf16"]), r3(pr["qi_f32"]), r3(pr["misc_f32"]),
                    _with_past(get("k_a"), r3(pr["ka_bf16"])),
                    _with_past(get("v_a"), r3(pr["va_bf16"])),
                    _with_past(get("k_i"), r3(pr["misc_bf16"])),
                    bias_a_near, bias_a_far, qb=qb, qb0=qb0, topk=topk)
    o_b = _sb_attn(r3(pr["qb_bf16"]),
                   _with_past(get("k_b"), r3(pr["kb_bf16"])),
                   _with_past(get("v_b"), r3(pr["vb_bf16"])), qb=qb, qb0=qb0)
    lamv = jnp.pad(jnp.stack([prm["lambda_q1"][l], prm["lambda_k1"][l],
                              prm["lambda_q2"][l], prm["lambda_k2"][l]]).astype(F32),
                   ((0, 0), (0, LANES - DC_HALF)))
    gain2 = jnp.tile(prm["subln_gain"][l].astype(F32), 2).reshape(1, LANES)
    o_c = _diff_attn(r3(pr["qc_bf16"]),
                     _with_past(get("k_c"), r3(pr["kc_bf16"])),
                     _with_past(get("v_c"), r3(pr["vc_bf16"])),
                     bias_c_near, bias_c_far, lamv, gain2, qb=qb, qb0=qb0, lam0=_lambda_init(l))

    x1 = _outproj(o_a.reshape(m, WA), o_b.reshape(m, WB), o_c.reshape(m, WC),
                  prm["w_out"][l], prm["g_post_mix"][l], x.reshape(m, d))
    state = (jnp.zeros((b, CONV_W - 1, prm["w_up"].shape[2]), F32) if past is None
             else past["conv"])
    x2, new_conv = _ffn(x1, state, prm["g_pre_ffn"][l], prm["g_post_ffn"][l], prm["w_up"][l],
                        prm["conv_w"][l], prm["conv_b"][l], prm["w_down"][l], seq=t)
    new = (pr["ka_f32"].reshape(b, t, N_HEADS_A, HEAD_DIM),
           pr["va_f32"].reshape(b, t, N_HEADS_A, HEAD_DIM),
           pr["misc_f32"][:, :IDX_DIM].reshape(b, t, IDX_DIM),
           pr["kb_f32"].reshape(b, t, N_HEADS_B, HEAD_DIM),
           pr["vb_f32"].reshape(b, t, N_HEADS_B, HEAD_DIM),
           pr["kc_f32"].reshape(b, t, N_HEADS_C, HEAD_DIM),
           pr["vc_f32"].reshape(b, t, N_HEADS_C, HEAD_DIM),
           new_conv)
    return x2.reshape(b, t, d), new


def _run_trunk(x, caches, prm):
    depth = prm["w_in"].shape[0]
    news = []
    for l in range(depth):
        past = None if caches is None else {k: v[l] for k, v in caches.items()}
        x, new = _layer(x, past, l, prm)
        news.append(new)
    return x, [jnp.stack([n[i] for n in news]) for i in range(len(news[0]))]


def kernel(x_prompt, x_sample, cache_k_a, cache_v_a, cache_idx_k, cache_k_b, cache_v_b, cache_k_c,
           cache_v_c, state_ffn_conv, w_in, w_out, rel_bias, lambda_q1, lambda_k1, lambda_q2,
           lambda_k2, subln_gain, g_pre_mix, g_post_mix, g_pre_ffn, g_post_ffn, w_up, conv_w,
           conv_b, w_down):
    prm = dict(w_in=w_in, w_out=w_out, rel_bias=rel_bias, lambda_q1=lambda_q1, lambda_k1=lambda_k1,
               lambda_q2=lambda_q2, lambda_k2=lambda_k2, subln_gain=subln_gain, g_pre_mix=g_pre_mix,
               g_post_mix=g_post_mix, g_pre_ffn=g_pre_ffn, g_post_ffn=g_post_ffn, w_up=w_up,
               conv_w=conv_w, conv_b=conv_b, w_down=w_down)
    y_prompt, p_new = _run_trunk(x_prompt, None, prm)
    caches = dict(k_a=cache_k_a, v_a=cache_v_a, k_i=cache_idx_k, k_b=cache_k_b, v_b=cache_v_b,
                  k_c=cache_k_c, v_c=cache_v_c, conv=state_ffn_conv)
    y_sample, s_new = _run_trunk(x_sample, caches, prm)
    return (y_prompt, y_sample, *p_new, *s_new)
```

```python
import functools
import math

import numpy as np
import jax
import jax.numpy as jnp
from jax import lax
from jax.experimental import pallas as pl
from jax.experimental.pallas import tpu as pltpu

F32 = jnp.float32
BF16 = jnp.bfloat16
I32 = jnp.int32

LANES = 128
VMEM_LIMIT = 56 * 1024 * 1024

CHUNK = 64
HEAD_DIM = 64
N_HEADS_A = 6
N_HEADS_B = 6
N_HEADS_C = 4
DC_HALF = HEAD_DIM // 2
IDX_HEADS = 8
IDX_DIM = 32
TOPK_MAX = 256
NUM_BUCKETS = 32
MAX_DISTANCE = 128
CONV_W = 3
EPS = 1e-6

WA = N_HEADS_A * HEAD_DIM
WB = N_HEADS_B * HEAD_DIM
WC = N_HEADS_C * HEAD_DIM
WQI = IDX_HEADS * IDX_DIM

NEG = -1e30
INT_MIN = -2 ** 31
KEY_NEG_INF = int(np.int32(np.float32(-np.inf).view(np.int32)) ^ np.int32(0x7FFFFFFF))
EXP_ZERO = -104.0
LOG2E = math.log2(math.e)

KB = 128
TAIL = 2 * KB
SUP = 4 * KB
PADF = SUP - KB
assert TAIL - KB >= MAX_DISTANCE
NT_DIMS = (((1,), (1,)), ((), ()))


def _cparams(sem):
    return pltpu.CompilerParams(dimension_semantics=sem, vmem_limit_bytes=VMEM_LIMIT)


def _dot_nt(a, b):
    return lax.dot_general(a, b, NT_DIMS, preferred_element_type=F32)


def _dot(a, b):
    return jnp.dot(a, b, preferred_element_type=F32)


_PROJ_GROUPS = (
    ("qa", 0, WA, HEAD_DIM ** -0.5, False, True),
    ("ka", 384, WA, 1.0, True, True),
    ("va", 768, WA, 1.0, True, True),
    ("qi", 1152, WQI, 1.0, True, False),
    ("misc", 1408, LANES, 1.0, True, True),
    ("qb", 1536, WB, HEAD_DIM ** -0.5, False, True),
    ("kb", 1920, WB, 1.0, True, True),
    ("vb", 2304, WB, 1.0, True, True),
    ("qc", 2688, WC, 1.0, False, True),
    ("kc", 2944, WC, 1.0, True, True),
    ("vc", 3200, WC, 1.0, True, True),
)
_PROJ_NPAD = 3456


def _proj_out_names():
    names = []
    for name, _, _, _, f32o, bf16o in _PROJ_GROUPS:
        if f32o:
            names.append(name + "_f32")
        if bf16o:
            names.append(name + "_bf16")
    return names


def _proj_kernel(x_ref, g_ref, w_ref, *out_refs):
    x = x_ref[...]
    ms = jnp.mean(x * x, axis=-1, keepdims=True)
    h = (x * lax.rsqrt(ms + EPS) * g_ref[...]).astype(BF16)
    k = 0
    for _, off, width, scale, f32o, bf16o in _PROJ_GROUPS:
        r = _dot(h, w_ref[:, off:off + width])
        if f32o:
            out_refs[k][...] = r
            k += 1
        if bf16o:
            out_refs[k][...] = (r * scale).astype(BF16) if scale != 1.0 else r.astype(BF16)
            k += 1


def _pad_w_in(w):
    d = w.shape[0]
    misc = jnp.concatenate([w[:, 1416:1448], w[:, 1408:1416],
                            jnp.zeros((d, LANES - IDX_DIM - IDX_HEADS), w.dtype)], axis=1)
    return jnp.concatenate([w[:, 0:1408], misc, w[:, 1448:]], axis=1).astype(BF16)


def _proj(x2d, g, w_pad):
    m, d = x2d.shape
    tm = min(512, m)
    assert m % tm == 0
    out_shapes, out_specs = [], []
    for _, _, width, _, f32o, bf16o in _PROJ_GROUPS:
        for want, dt in ((f32o, F32), (bf16o, BF16)):
            if want:
                out_shapes.append(jax.ShapeDtypeStruct((m, width), dt))
                out_specs.append(pl.BlockSpec((tm, width), lambda i: (i, 0)))
    outs = pl.pallas_call(
        _proj_kernel,
        grid=(m // tm,),
        in_specs=[pl.BlockSpec((tm, d), lambda i: (i, 0)),
                  pl.BlockSpec((1, d), lambda i: (0, 0)),
                  pl.BlockSpec((d, _PROJ_NPAD), lambda i: (0, 0))],
        out_specs=out_specs,
        out_shape=out_shapes,
        compiler_params=_cparams(("parallel",)),
        name="proj",
    )(x2d, g.reshape(1, d), w_pad)
    return dict(zip(_proj_out_names(), outs))


def _t5_bucket(rel):
    half = NUM_BUCKETS // 2
    max_exact = half // 2
    ret = jnp.where(rel > 0, half, 0)
    n = jnp.abs(rel)
    nf = jnp.maximum(n, 1).astype(F32)
    large = max_exact + (jnp.log(nf / max_exact) / math.log(MAX_DISTANCE / max_exact)
                         * (half - max_exact)).astype(I32)
    large = jnp.minimum(large, half - 1)
    return ret + jnp.where(n < max_exact, n, large)


def _bias_tiles(tab, qb, n_valid):
    qq = jnp.arange(qb, dtype=I32)[:, None]
    kk = jnp.arange(TAIL, dtype=I32)[None, :] - (TAIL - KB)
    tile = tab[_t5_bucket(kk - qq)].astype(F32).transpose(2, 0, 1) * LOG2E
    vis = ((kk // CHUNK) <= (qq // CHUNK)) & (kk < n_valid)
    tile = jnp.where(vis[None], tile, NEG)
    tile1 = jnp.where((kk < 0)[None], NEG, tile)
    far = tab[_t5_bucket(jnp.array(-MAX_DISTANCE, I32))].astype(F32) * LOG2E
    return jnp.stack([tile, tile1]), far


def _front_mask_rows():
    col = jnp.arange(SUP, dtype=I32)[None, None, :]
    v = jnp.arange(SUP // KB, dtype=I32)[:, None, None]
    return jnp.where(col < v * KB, NEG, 0.0).astype(F32)


def _windows(qbi):
    tail_start = (qbi + 1) * KB - TAIL
    n_far = (tail_start + SUP - 1) // SUP
    n_front = (n_far * SUP - tail_start) // KB
    return tail_start, n_far, n_front


def _lane_chunks(x):
    return [x[:, c * LANES:(c + 1) * LANES] for c in range(x.shape[1] // LANES)]


def _attn_step(x, ccol, v, m_ref, l_ref, acc_ref):
    chunks = _lane_chunks(x)
    mx = functools.reduce(jnp.maximum, chunks)
    m_old = m_ref[...]
    m_new = jnp.maximum(m_old, jnp.max(mx, axis=1, keepdims=True) + ccol)
    p = jnp.exp2(x + (ccol - m_new))
    alpha = jnp.exp2(m_old - m_new)
    l_ref[...] = alpha * l_ref[...] + functools.reduce(jnp.add, _lane_chunks(p))
    acc_ref[...] = alpha * acc_ref[...] + _dot(p.astype(BF16), v)
    m_ref[...] = m_new


def _key_spec(s_tot, width, index_map):
    return pl.BlockSpec((1, s_tot, width), index_map, pipeline_mode=pl.Buffered(1))


def _diff_kernel(lam_ref, gain_ref, btail_ref, cfar_ref, fmask_ref, q_ref, k_ref, v_ref, o_ref,
                 m_ref, l_ref, acc_ref, *, qb, qb0, lam0):
    qbi = qb0 + pl.program_id(2)
    tail_start, n_far, n_front = _windows(qbi)
    q = q_ref[0]
    lane = lax.broadcasted_iota(I32, (qb, LANES), 1)
    qs = jnp.concatenate(
        [jnp.where((lane >= DC_HALF * c) & (lane < DC_HALF * (c + 1)), q, jnp.zeros_like(q))
         for c in range(4)], axis=0)
    m_ref[...] = jnp.full(m_ref.shape, NEG, F32)
    l_ref[...] = jnp.zeros(l_ref.shape, F32)
    acc_ref[...] = jnp.zeros(acc_ref.shape, F32)
    a = DC_HALF ** -0.5 * LOG2E

    def window(start, width):
        off = pl.multiple_of(start + PADF, KB)
        return k_ref[0, pl.ds(off, width), :], v_ref[0, pl.ds(off, width), :]

    k, v = window(tail_start, TAIL)
    bt = btail_ref[jnp.where(qbi == 0, 1, 0)]
    x = (_dot_nt(qs, k) * a).reshape(2, 2, qb, TAIL) + bt[:, None]
    _attn_step(x.reshape(4 * qb, TAIL), 0.0, v, m_ref, l_ref, acc_ref)

    ccol = cfar_ref[0]

    def far_body(j, carry):
        k, v = window(tail_start - SUP * (j + 1), SUP)
        _attn_step(_dot_nt(qs, k) * a, ccol, v, m_ref, l_ref, acc_ref)
        return carry

    lax.fori_loop(0, n_far - 1, far_body, 0)

    @pl.when(n_far >= 1)
    def _():
        k, v = window(tail_start - SUP * n_far, SUP)
        _attn_step(_dot_nt(qs, k) * a + fmask_ref[n_front], ccol, v, m_ref, l_ref, acc_ref)

    lam_v = lam_ref[...]
    s1 = jnp.sum(lam_v[0:1] * lam_v[1:2], axis=1, keepdims=True)
    s2 = jnp.sum(lam_v[2:3] * lam_v[3:4], axis=1, keepdims=True)
    lam = jnp.exp(s1) - jnp.exp(s2) + lam0
    o4 = acc_ref[...] / jnp.sum(l_ref[...], axis=1, keepdims=True)
    o0 = o4[0:qb] - lam * o4[qb:2 * qb]
    o1 = o4[2 * qb:3 * qb] - lam * o4[3 * qb:4 * qb]
    first = lane < HEAD_DIM
    o = jnp.where(first, o0, o1)
    sq = o * o
    ms0 = jnp.sum(jnp.where(first, sq, 0.0), axis=1, keepdims=True)
    ms1 = jnp.sum(jnp.where(first, 0.0, sq), axis=1, keepdims=True)
    ms = jnp.where(first, ms0, ms1) * (1.0 / HEAD_DIM)
    y = o * lax.rsqrt(ms + EPS) * gain_ref[...]
    o_ref[0] = (y * (1.0 - lam0)).astype(o_ref.dtype)


def _diff_attn(q, k, v, btail, cfar, lamv, gain2, *, qb, qb0, lam0):
    b, t, _ = q.shape
    s_tot = k.shape[1]
    nq = t // qb
    npair = N_HEADS_C // 2
    ccol = jnp.repeat(cfar.reshape(npair, 2), 2 * qb, axis=1).reshape(npair, 4 * qb, 1)
    btail = btail.reshape(2, npair, 2, qb, TAIL)
    kern = functools.partial(_diff_kernel, qb=qb, qb0=qb0, lam0=lam0)
    return pl.pallas_call(
        kern,
        grid=(b, npair, nq),
        in_specs=[
            pl.BlockSpec((4, LANES), lambda bi, p, i: (0, 0)),
            pl.BlockSpec((1, LANES), lambda bi, p, i: (0, 0)),
            pl.BlockSpec((2, None, 2, qb, TAIL), lambda bi, p, i: (0, p, 0, 0, 0)),
            pl.BlockSpec((1, 4 * qb, 1), lambda bi, p, i: (p, 0, 0)),
            pl.BlockSpec((SUP // KB, 1, SUP), lambda bi, p, i: (0, 0, 0)),
            pl.BlockSpec((1, qb, LANES), lambda bi, p, i: (bi, i, p)),
            _key_spec(s_tot, LANES, lambda bi, p, i: (bi, 0, p)),
            _key_spec(s_tot, LANES, lambda bi, p, i: (bi, 0, p)),
        ],
        out_specs=pl.BlockSpec((1, qb, LANES), lambda bi, p, i: (bi, i, p)),
        out_shape=jax.ShapeDtypeStruct((b, t, WC), BF16),
        scratch_shapes=[pltpu.VMEM((4 * qb, 1), F32), pltpu.VMEM((4 * qb, LANES), F32),
                        pltpu.VMEM((4 * qb, LANES), F32)],
        compiler_params=_cparams(("parallel", "parallel", "arbitrary")),
        name="diff_attn",
    )(lamv, gain2, btail, ccol, _front_mask_rows(), q, k, v)


def _sb_kernel(q_ref, k_ref, v_ref, o_ref, carry_ref, acc_ref, *, qb, qb0):
    qbi = qb0 + pl.program_id(2)
    q = q_ref[0]
    lane = lax.broadcasted_iota(I32, (qb, LANES), 1)
    causal = lax.broadcasted_iota(I32, (qb, KB), 1) < lax.broadcasted_iota(I32, (qb, KB), 0)
    tri = jnp.where(lax.broadcasted_iota(I32, (KB, KB), 0) > lax.broadcasted_iota(I32, (KB, KB), 1),
                    1.0, 0.0).astype(BF16)
    outs = []
    for hh in range(2):
        qz = jnp.where((lane >= HEAD_DIM * hh) & (lane < HEAD_DIM * (hh + 1)), q, jnp.zeros_like(q))
        carry_ref[...] = jnp.zeros(carry_ref.shape, F32)
        acc_ref[...] = jnp.zeros(acc_ref.shape, F32)

        def block(j, masked, qz=qz):
            off = pl.multiple_of(j * KB, KB)
            k = k_ref[0, pl.ds(off, KB), :]
            v = v_ref[0, pl.ds(off, KB), :]
            z = _dot_nt(qz, k)
            sp = jnp.maximum(z, 0.0) + jnp.log1p(jnp.exp(-jnp.abs(z)))
            lg = -sp
            if masked:
                lg = jnp.where(causal, lg, 0.0)
            l1 = lg.astype(BF16)
            r1 = lg - l1.astype(F32)
            l2 = r1.astype(BF16)
            l3 = (r1 - l2.astype(F32)).astype(BF16)
            tail = _dot(l1, tri) + _dot(l2, tri) + _dot(l3, tri)
            carry = carry_ref[...]
            a = jnp.exp(z - sp + tail + carry)
            if masked:
                a = jnp.where(causal, a, 0.0)
            acc_ref[...] += _dot(a.astype(BF16), v)
            carry_new = carry + jnp.sum(lg, axis=1, keepdims=True)
            carry_ref[...] = carry_new
            return jnp.max(carry_new)

        mx0 = block(qbi, True)

        def cond(st):
            return jnp.logical_and(st[0] >= 0, st[1] > EXP_ZERO)

        def body(st):
            return st[0] - 1, block(st[0], False)

        lax.while_loop(cond, body, (qbi - 1, mx0))
        outs.append(acc_ref[...])
    o_ref[0] = jnp.where(lane < HEAD_DIM, outs[0], outs[1]).astype(o_ref.dtype)


def _sb_attn(q, k, v, *, qb, qb0):
    b, t, _ = q.shape
    s_pad = k.shape[1]
    nq = t // qb
    kern = functools.partial(_sb_kernel, qb=qb, qb0=qb0)
    return pl.pallas_call(
        kern,
        grid=(b, N_HEADS_B // 2, nq),
        in_specs=[
            pl.BlockSpec((1, qb, LANES), lambda bi, p, i: (bi, i, p)),
            pl.BlockSpec((1, s_pad, LANES), lambda bi, p, i: (bi, 0, p)),
            pl.BlockSpec((1, s_pad, LANES), lambda bi, p, i: (bi, 0, p)),
        ],
        out_specs=pl.BlockSpec((1, qb, LANES), lambda bi, p, i: (bi, i, p)),
        out_shape=jax.ShapeDtypeStruct((b, t, WB), BF16),
        scratch_shapes=[pltpu.VMEM((qb, 1), F32), pltpu.VMEM((qb, LANES), F32)],
        compiler_params=_cparams(("parallel", "parallel", "arbitrary")),
        name="sb_attn",
    )(q, k, v)


def _dsa_kernel(btail_ref, cfar_ref, fmask_ref, qa_ref, qi_ref, wi_ref, ka_ref, va_ref, ki_ref,
                o_ref, kxt_ref, kxf_ref, wb_ref, r_ref, cnt_ref, cs_ref, m_ref, l_ref, acc_ref,
                *, qb, qb0, topk, idx_bits):
    npair = N_HEADS_A // 2
    qbi = qb0 + pl.program_id(1)
    tail_start, n_far, n_front = _windows(qbi)
    lane = lax.broadcasted_iota(I32, (qb, LANES), 1)

    def window(ref, start, width):
        off = pl.multiple_of(start + PADF, KB)
        return ref[0, pl.ds(off, width), :]

    qi = qi_ref[0]
    parts = []
    for h in range(IDX_HEADS):
        g, sh = divmod(h * IDX_DIM, LANES)
        part = qi[:, g * LANES:(g + 1) * LANES]
        if sh:
            part = pltpu.roll(part, LANES - sh, 1)
        parts.append(jnp.where(lane < IDX_DIM, part, 0.0).astype(BF16))
    qis = jnp.concatenate(parts, axis=0)
    wi = wi_ref[0] * (IDX_HEADS ** -0.5 * IDX_DIM ** -0.5)
    for h in range(IDX_HEADS):
        wb_ref[h * qb:(h + 1) * qb, :] = jnp.broadcast_to(wi[:, IDX_DIM + h:IDX_DIM + h + 1],
                                                           (qb, LANES))
    qa = qa_ref[0]
    qas = []
    for p in range(npair):
        part = qa[:, p * LANES:(p + 1) * LANES]
        qas.append(jnp.concatenate(
            [jnp.where(lane < HEAD_DIM, part, jnp.zeros_like(part)),
             jnp.where(lane < HEAD_DIM, jnp.zeros_like(part), part)], axis=0))

    def keys_of(start, width, hidden):
        d = _dot_nt(qis, window(ki_ref, start, width))
        wb = wb_ref[...]
        cols = []
        for dc in _lane_chunks(d):
            r = jnp.maximum(dc, 0.0) * wb
            cols.append(functools.reduce(jnp.add, [r[h * qb:(h + 1) * qb] for h in range(IDX_HEADS)]))
        sc = jnp.concatenate(cols, axis=1)
        sc = jnp.where(sc == 0.0, 0.0, sc)
        bits = lax.bitcast_convert_type(sc, I32)
        key = bits ^ ((bits >> 31) & 0x7FFFFFFF)
        return jnp.where(hidden, KEY_NEG_INF, key)

    bt0 = btail_ref[jnp.where(qbi == 0, 1, 0), 0]
    kxt_ref[...] = keys_of(tail_start, TAIL, bt0 < 0.5 * NEG)

    def score_body(j, carry):
        kxf_ref[j] = keys_of(tail_start - SUP * (j + 1), SUP, False)
        return carry

    lax.fori_loop(0, n_far - 1, score_body, 0)

    @pl.when(n_far >= 1)
    def _():
        kxf_ref[n_far - 1] = keys_of(tail_start - SUP * n_far, SUP, fmask_ref[n_front] < 0.5 * NEG)

    col_t = lax.broadcasted_iota(I32, (qb, TAIL), 1)
    col_f = lax.broadcasted_iota(I32, (qb, SUP), 1)

    def count(pred_t, pred_f):
        def body(j, acc):
            hit = jnp.where(pred_f(kxf_ref[j], col_f + (tail_start - SUP * (j + 1))), 1.0, 0.0)
            return acc + functools.reduce(jnp.add, _lane_chunks(hit))
        acc = functools.reduce(jnp.add, _lane_chunks(
            jnp.where(pred_t(kxt_ref[...], col_t + tail_start), 1.0, 0.0)))
        acc = lax.fori_loop(0, n_far, body, acc)
        return jnp.sum(acc, axis=1, keepdims=True)

    def count1(make_pred):
        return count(make_pred(TAIL), make_pred(SUP))

    wide = lambda a, w: jnp.broadcast_to(a, (qb, w))

    def ge(a):
        return lambda w: (lambda kx, idx, aw=wide(a, w): kx >= aw)

    def gt(a):
        return lambda w: (lambda kx, idx, aw=wide(a, w): kx > aw)

    def eq(a):
        return lambda w: (lambda kx, idx, aw=wide(a, w): kx == aw)

    def eq_before(a, c):
        return lambda w: (lambda kx, idx, aw=wide(a, w), cw=wide(c, w): (kx == aw) & (idx < cw))

    r_ref[...] = jnp.full((qb, 1), INT_MIN, I32)
    cnt_ref[...] = jnp.full((qb, 1), float(TAIL + SUP * 64), F32)

    def bit_cond(st):
        return jnp.logical_and(st[0] < 32, st[1] > 0.0)

    def bit_body(st):
        t = st[0]
        r = r_ref[...]
        cand = r + lax.shift_left(jnp.int32(1), 31 - t)
        cnt = count1(ge(cand))
        take = cnt >= topk
        r_ref[...] = jnp.where(take, cand, r)
        cnt_new = jnp.where(take, cnt, cnt_ref[...])
        cnt_ref[...] = cnt_new
        return t + 1, jnp.max(jnp.where(cnt_new == topk, 0.0, 1.0))

    lax.while_loop(bit_cond, bit_body, (jnp.int32(0), jnp.float32(1.0)))
    r1 = r_ref[...]

    need = topk - count1(gt(r1))
    n_eq = count1(eq(r1))
    cs_ref[...] = jnp.full((qb, 1), 2 ** idx_bits - 1, I32)
    overflow = jnp.where((n_eq > need) & (r1 > KEY_NEG_INF), 1.0, 0.0)

    @pl.when(jnp.max(overflow) > 0.0)
    def _():
        def tie_body(t, c):
            cand = c + lax.shift_left(jnp.int32(1), idx_bits - 1 - t)
            cnt = count1(eq_before(r1, cand))
            return jnp.where(cnt < need, cand, c)
        cs_ref[...] = lax.fori_loop(0, idx_bits, tie_body, jnp.zeros((qb, 1), I32))

    cs1 = cs_ref[...]

    m_ref[...] = jnp.full(m_ref.shape, NEG, F32)
    l_ref[...] = jnp.zeros(l_ref.shape, F32)
    acc_ref[...] = jnp.zeros(acc_ref.shape, F32)

    def sel_mask(kx, idx, w):
        sel = (kx > wide(r1, w)) | ((kx == wide(r1, w)) & (idx <= wide(cs1, w)))
        return jnp.where(sel & (kx > KEY_NEG_INF), 0.0, NEG)

    def attend(start, width, madd, bias, ccols):
        k = window(ka_ref, start, width)
        v = window(va_ref, start, width)
        for p in range(npair):
            x = (_dot_nt(qas[p], k[:, p * LANES:(p + 1) * LANES]) * LOG2E).reshape(2, qb, width) + madd[None]
            if bias is not None:
                x = x + bias[2 * p:2 * p + 2]
            _attn_step(x.reshape(2 * qb, width), ccols[p], v[:, p * LANES:(p + 1) * LANES],
                       m_ref.at[p], l_ref.at[p], acc_ref.at[p])

    bt = btail_ref[jnp.where(qbi == 0, 1, 0)]
    attend(tail_start, TAIL, sel_mask(kxt_ref[...], col_t + tail_start, TAIL), bt, [0.0] * npair)

    ccols = [cfar_ref[p] for p in range(npair)]

    def far_body(j, carry):
        start = tail_start - SUP * (j + 1)
        attend(start, SUP, sel_mask(kxf_ref[j], col_f + start, SUP), None, ccols)
        return carry

    lax.fori_loop(0, n_far, far_body, 0)

    for p in range(npair):
        o2 = acc_ref[p] / jnp.sum(l_ref[p], axis=1, keepdims=True)
        o = jnp.where(lane < HEAD_DIM, o2[0:qb], o2[qb:2 * qb])
        o_ref[0, :, p * LANES:(p + 1) * LANES] = o.astype(o_ref.dtype)


def _dsa_attn(qa, qi, misc_f32, ka, va, ki, btail, cfar, *, qb, qb0, topk):
    b, t, _ = qa.shape
    s_tot = ka.shape[1]
    nq = t // qb
    npair = N_HEADS_A // 2
    n_far_max = max(1, -(-(s_tot - PADF - TAIL) // SUP))
    assert n_far_max <= 64
    idx_bits = int(math.ceil(math.log2(s_tot))) + 1
    ccol = jnp.repeat(cfar.reshape(npair, 2), qb, axis=1).reshape(npair, 2 * qb, 1)
    kern = functools.partial(_dsa_kernel, qb=qb, qb0=qb0, topk=float(topk), idx_bits=idx_bits)
    return pl.pallas_call(
        kern,
        grid=(b, nq),
        in_specs=[
            pl.BlockSpec((2, N_HEADS_A, qb, TAIL), lambda bi, i: (0, 0, 0, 0)),
            pl.BlockSpec((npair, 2 * qb, 1), lambda bi, i: (0, 0, 0)),
            pl.BlockSpec((SUP // KB, 1, SUP), lambda bi, i: (0, 0, 0)),
            pl.BlockSpec((1, qb, WA), lambda bi, i: (bi, i, 0)),
            pl.BlockSpec((1, qb, WQI), lambda bi, i: (bi, i, 0)),
            pl.BlockSpec((1, qb, LANES), lambda bi, i: (bi, i, 0)),
            _key_spec(s_tot, WA, lambda bi, i: (bi, 0, 0)),
            _key_spec(s_tot, WA, lambda bi, i: (bi, 0, 0)),
            _key_spec(s_tot, LANES, lambda bi, i: (bi, 0, 0)),
        ],
        out_specs=pl.BlockSpec((1, qb, WA), lambda bi, i: (bi, i, 0)),
        out_shape=jax.ShapeDtypeStruct((b, t, WA), BF16),
        scratch_shapes=[
            pltpu.VMEM((qb, TAIL), I32),
            pltpu.VMEM((n_far_max, qb, SUP), I32),
            pltpu.VMEM((IDX_HEADS * qb, LANES), F32),
            pltpu.VMEM((qb, 1), I32),
            pltpu.VMEM((qb, 1), F32),
            pltpu.VMEM((qb, 1), I32),
            pltpu.VMEM((npair, 2 * qb, 1), F32),
            pltpu.VMEM((npair, 2 * qb, LANES), F32),
            pltpu.VMEM((npair, 2 * qb, LANES), F32),
        ],
        compiler_params=_cparams(("parallel", "arbitrary")),
        name="dsa_attn",
    )(btail, ccol, _front_mask_rows(), qa, qi, misc_f32, ka, va, ki)


def _outproj_kernel(oa_ref, ob_ref, oc_ref, wa_ref, wb_ref, wc_ref, g_ref, x_ref, y_ref):
    mix = _dot(oa_ref[...], wa_ref[...]) + _dot(ob_ref[...], wb_ref[...]) + _dot(oc_ref[...], wc_ref[...])
    ms = jnp.mean(mix * mix, axis=-1, keepdims=True)
    y_ref[...] = x_ref[...] + mix * lax.rsqrt(ms + EPS) * g_ref[...]


def _outproj(oa, ob, oc, w_out, g, x2d):
    m, d = x2d.shape
    tm = min(512, m)
    w = w_out.astype(BF16)
    row = lambda i: (i, 0)
    const = lambda i: (0, 0)
    return pl.pallas_call(
        _outproj_kernel,
        grid=(m // tm,),
        in_specs=[pl.BlockSpec((tm, WA), row), pl.BlockSpec((tm, WB), row), pl.BlockSpec((tm, WC), row),
                  pl.BlockSpec((WA, d), const), pl.BlockSpec((WB, d), const), pl.BlockSpec((WC, d), const),
                  pl.BlockSpec((1, d), const), pl.BlockSpec((tm, d), row)],
        out_specs=pl.BlockSpec((tm, d), row),
        out_shape=jax.ShapeDtypeStruct((m, d), F32),
        compiler_params=_cparams(("parallel",)),
        name="outproj",
    )(oa, ob, oc, w[:WA], w[WA:WA + WB], w[WA + WB:], g.reshape(1, d), x2d)


HALO = 16


def _ffn_kernel(x_ref, xh_ref, stg_ref, stv_ref, gpre_ref, gpost_ref, wg_ref, wv_ref, cwg_ref,
                cwv_ref, cbg_ref, cbv_ref, wd_ref, y_ref, convg_ref, convv_ref,
                h_ref, ug_ref, uv_ref, f_ref, *, tm, tiles_per_seq):
    i = pl.program_id(0)
    f = pl.program_id(1)
    first_of_seq = (i % tiles_per_seq) == 0

    def norm(x):
        ms = jnp.mean(x * x, axis=-1, keepdims=True)
        return (x * lax.rsqrt(ms + EPS) * gpre_ref[...]).astype(BF16)

    @pl.when(f == 0)
    def _():
        h_ref[0:HALO, :] = norm(xh_ref[...])
        h_ref[HALO:, :] = norm(x_ref[...])
        f_ref[...] = jnp.zeros(f_ref.shape, F32)

    h = h_ref[...]
    ug_ref[...] = _dot(h, wg_ref[...])
    uv_ref[...] = _dot(h, wv_ref[...])

    @pl.when(first_of_seq)
    def _():
        ug_ref[0:HALO, :] = stg_ref[0]
        uv_ref[0:HALO, :] = stv_ref[0]

    def conv(u_ref, cw_ref, cb_ref):
        u = u_ref[...]
        u1 = pltpu.roll(u, 1, 0)[HALO:, :]
        u2 = pltpu.roll(u, 2, 0)[HALO:, :]
        cw = cw_ref[...]
        return cb_ref[...] + (cw[0:1] * u2 + cw[1:2] * u1 + cw[2:3] * u[HALO:, :])

    gate = conv(ug_ref, cwg_ref, cbg_ref)
    val = conv(uv_ref, cwv_ref, cbv_ref)
    c0 = math.sqrt(2.0 / math.pi)
    gelu = 0.5 * gate * (1.0 + jnp.tanh(c0 * (gate + 0.044715 * (gate * gate * gate))))
    f_ref[...] += _dot((gelu * val).astype(BF16), wd_ref[...])

    convg_ref[0] = ug_ref[tm:, :]
    convv_ref[0] = uv_ref[tm:, :]

    @pl.when(f == pl.num_programs(1) - 1)
    def _():
        ff = f_ref[...]
        ms = jnp.mean(ff * ff, axis=-1, keepdims=True)
        y_ref[...] = x_ref[...] + ff * lax.rsqrt(ms + EPS) * gpost_ref[...]


def _ffn(x2d, state, g_pre, g_post, w_up, conv_w, conv_b, w_down, *, seq):
    m, d = x2d.shape
    b = m // seq
    d_ff = w_down.shape[0]
    tn = 256
    assert d_ff % tn == 0
    nf = d_ff // tn
    tm = min(512, seq)
    assert seq % tm == 0 and tm % HALO == 0
    tiles_per_seq = seq // tm
    n_tiles = m // tm
    st = jnp.pad(state.astype(F32), ((0, 0), (HALO - (CONV_W - 1), 0), (0, 0)))
    wu = w_up.astype(BF16)
    wd = w_down.astype(BF16)
    kern = functools.partial(_ffn_kernel, tm=tm, tiles_per_seq=tiles_per_seq)
    halo_blocks = tm // HALO
    y, conv_g, conv_v = pl.pallas_call(
        kern,
        grid=(n_tiles, nf),
        in_specs=[
            pl.BlockSpec((tm, d), lambda i, f: (i, 0)),
            pl.BlockSpec((HALO, d), lambda i, f: (jnp.maximum(i * halo_blocks - 1, 0), 0)),
            pl.BlockSpec((1, HALO, tn), lambda i, f: (i // tiles_per_seq, 0, f)),
            pl.BlockSpec((1, HALO, tn), lambda i, f: (i // tiles_per_seq, 0, nf + f)),
            pl.BlockSpec((1, d), lambda i, f: (0, 0)),
            pl.BlockSpec((1, d), lambda i, f: (0, 0)),
            pl.BlockSpec((d, tn), lambda i, f: (0, f)),
            pl.BlockSpec((d, tn), lambda i, f: (0, nf + f)),
            pl.BlockSpec((CONV_W, tn), lambda i, f: (0, f)),
            pl.BlockSpec((CONV_W, tn), lambda i, f: (0, nf + f)),
            pl.BlockSpec((1, tn), lambda i, f: (0, f)),
            pl.BlockSpec((1, tn), lambda i, f: (0, nf + f)),
            pl.BlockSpec((tn, d), lambda i, f: (f, 0)),
        ],
        out_specs=[pl.BlockSpec((tm, d), lambda i, f: (i, 0)),
                   pl.BlockSpec((1, HALO, tn), lambda i, f: (i, 0, f)),
                   pl.BlockSpec((1, HALO, tn), lambda i, f: (i, 0, f))],
        out_shape=[jax.ShapeDtypeStruct((m, d), F32),
                   jax.ShapeDtypeStruct((n_tiles, HALO, d_ff), F32),
                   jax.ShapeDtypeStruct((n_tiles, HALO, d_ff), F32)],
        scratch_shapes=[pltpu.VMEM((HALO + tm, d), BF16),
                        pltpu.VMEM((HALO + tm, tn), F32),
                        pltpu.VMEM((HALO + tm, tn), F32),
                        pltpu.VMEM((tm, d), F32)],
        compiler_params=_cparams(("parallel", "arbitrary")),
        name="ffn",
    )(x2d, x2d, st, st, g_pre.reshape(1, d), g_post.reshape(1, d), wu, wu,
      conv_w, conv_w, conv_b.reshape(1, -1), conv_b.reshape(1, -1), wd)
    last = lambda c: c.reshape(b, tiles_per_seq, HALO, d_ff)[:, -1, HALO - (CONV_W - 1):]
    return y, jnp.concatenate([last(conv_g), last(conv_v)], axis=-1)


def _lambda_init(l):
    return 0.8 - 0.6 * math.exp(-0.3 * l)


def _with_past(past, new, front=0):
    b, t, w = new.shape
    rows = [] if front == 0 else [jnp.zeros((b, front, w), BF16)]
    p = 0
    if past is not None:
        p = past.shape[1]
        pk = past.reshape(b, p, -1).astype(BF16)
        if pk.shape[2] < w:
            pk = jnp.pad(pk, ((0, 0), (0, 0), (0, w - pk.shape[2])))
        rows.append(pk)
    rows.append(new)
    s = p + t
    s_pad = -(-s // KB) * KB
    if s_pad > s:
        rows.append(jnp.zeros((b, s_pad - s, w), BF16))
    return rows[0] if len(rows) == 1 else jnp.concatenate(rows, axis=1)


def _layer(x, past, l, prm):
    b, t, d = x.shape
    p_len = 0 if past is None else past["k_a"].shape[1]
    s = p_len + t
    qb = min(KB, t)
    assert p_len % KB == 0 and t % qb == 0 and (qb == KB or t == qb)
    qb0 = p_len // KB
    s_pad = -(-s // KB) * KB
    n_valid = s - (s_pad - KB)
    topk = min(TOPK_MAX, s // 4)
    m = b * t

    pr = _proj(x.reshape(m, d), prm["g_pre_mix"][l], _pad_w_in(prm["w_in"][l]))
    r3 = lambda a: a.reshape(b, t, a.shape[-1])
    get = lambda name: None if past is None else past[name]

    btail_a, cfar_a = _bias_tiles(prm["rel_bias"][:, :N_HEADS_A], qb, n_valid)
    btail_c, cfar_c = _bias_tiles(prm["rel_bias"][:, N_HEADS_A:], qb, n_valid)

    o_a = _dsa_attn(r3(pr["qa_bf16"]), r3(pr["qi_f32"]), r3(pr["misc_f32"]),
                    _with_past(get("k_a"), r3(pr["ka_bf16"]), PADF),
                    _with_past(get("v_a"), r3(pr["va_bf16"]), PADF),
                    _with_past(get("k_i"), r3(pr["misc_bf16"]), PADF),
                    btail_a, cfar_a, qb=qb, qb0=qb0, topk=topk)
    o_b = _sb_attn(r3(pr["qb_bf16"]),
                   _with_past(get("k_b"), r3(pr["kb_bf16"])),
                   _with_past(get("v_b"), r3(pr["vb_bf16"])), qb=qb, qb0=qb0)
    lamv = jnp.pad(jnp.stack([prm["lambda_q1"][l], prm["lambda_k1"][l],
                              prm["lambda_q2"][l], prm["lambda_k2"][l]]).astype(F32),
                   ((0, 0), (0, LANES - DC_HALF)))
    gain2 = jnp.tile(prm["subln_gain"][l].astype(F32), 2).reshape(1, LANES)
    o_c = _diff_attn(r3(pr["qc_bf16"]),
                     _with_past(get("k_c"), r3(pr["kc_bf16"]), PADF),
                     _with_past(get("v_c"), r3(pr["vc_bf16"]), PADF),
                     btail_c, cfar_c, lamv, gain2, qb=qb, qb0=qb0, lam0=_lambda_init(l))

    x1 = _outproj(o_a.reshape(m, WA), o_b.reshape(m, WB), o_c.reshape(m, WC),
                  prm["w_out"][l], prm["g_post_mix"][l], x.reshape(m, d))
    state = (jnp.zeros((b, CONV_W - 1, prm["w_up"].shape[2]), F32) if past is None
             else past["conv"])
    x2, new_conv = _ffn(x1, state, prm["g_pre_ffn"][l], prm["g_post_ffn"][l], prm["w_up"][l],
                        prm["conv_w"][l], prm["conv_b"][l], prm["w_down"][l], seq=t)
    new = (pr["ka_f32"].reshape(b, t, N_HEADS_A, HEAD_DIM),
           pr["va_f32"].reshape(b, t, N_HEADS_A, HEAD_DIM),
           pr["misc_f32"][:, :IDX_DIM].reshape(b, t, IDX_DIM),
           pr["kb_f32"].reshape(b, t, N_HEADS_B, HEAD_DIM),
           pr["vb_f32"].reshape(b, t, N_HEADS_B, HEAD_DIM),
           pr["kc_f32"].reshape(b, t, N_HEADS_C, HEAD_DIM),
           pr["vc_f32"].reshape(b, t, N_HEADS_C, HEAD_DIM),
           new_conv)
    return x2.reshape(b, t, d), new


def _run_trunk(x, caches, prm):
    depth = prm["w_in"].shape[0]
    news = []
    for l in range(depth):
        past = None if caches is None else {k: v[l] for k, v in caches.items()}
        x, new = _layer(x, past, l, prm)
        news.append(new)
    return x, [jnp.stack([n[i] for n in news]) for i in range(len(news[0]))]


def kernel(x_prompt, x_sample, cache_k_a, cache_v_a, cache_idx_k, cache_k_b, cache_v_b, cache_k_c,
           cache_v_c, state_ffn_conv, w_in, w_out, rel_bias, lambda_q1, lambda_k1, lambda_q2,
           lambda_k2, subln_gain, g_pre_mix, g_post_mix, g_pre_ffn, g_post_ffn, w_up, conv_w,
           conv_b, w_down):
    prm = dict(w_in=w_in, w_out=w_out, rel_bias=rel_bias, lambda_q1=lambda_q1, lambda_k1=lambda_k1,
               lambda_q2=lambda_q2, lambda_k2=lambda_k2, subln_gain=subln_gain, g_pre_mix=g_pre_mix,
               g_post_mix=g_post_mix, g_pre_ffn=g_pre_ffn, g_post_ffn=g_post_ffn, w_up=w_up,
               conv_w=conv_w, conv_b=conv_b, w_down=w_down)
    y_prompt, p_new = _run_trunk(x_prompt, None, prm)
    caches = dict(k_a=cache_k_a, v_a=cache_v_a, k_i=cache_idx_k, k_b=cache_k_b, v_b=cache_v_b,
                  k_c=cache_k_c, v_c=cache_v_c, conv=state_ffn_conv)
    y_sample, s_new = _run_trunk(x_sample, caches, prm)
    return (y_prompt, y_sample, *p_new, *s_new)
```

```python
import functools
import math

import numpy as np
import jax
import jax.numpy as jnp
from jax import lax
from jax.experimental import pallas as pl
from jax.experimental.pallas import tpu as pltpu

F32 = jnp.float32
BF16 = jnp.bfloat16
I32 = jnp.int32

LANES = 128
VMEM_LIMIT = 56 * 1024 * 1024

CHUNK = 64
HEAD_DIM = 64
N_HEADS_A = 6
N_HEADS_B = 6
N_HEADS_C = 4
DC_HALF = HEAD_DIM // 2
IDX_HEADS = 8
IDX_DIM = 32
TOPK_MAX = 256
NUM_BUCKETS = 32
MAX_DISTANCE = 128
CONV_W = 3
EPS = 1e-6

WA = N_HEADS_A * HEAD_DIM
WB = N_HEADS_B * HEAD_DIM
WC = N_HEADS_C * HEAD_DIM
WQI = IDX_HEADS * IDX_DIM

NEG = -1e30
INT_MIN = -2 ** 31
KEY_NEG_INF = int(np.int32(np.float32(-np.inf).view(np.int32)) ^ np.int32(0x7FFFFFFF))
EXP_ZERO = -104.0
LOG2E = math.log2(math.e)

KB = 128
TAIL = 2 * KB
SUP = 4 * KB
PADF = SUP - KB
assert TAIL - KB >= MAX_DISTANCE
NT_DIMS = (((1,), (1,)), ((), ()))


def _cparams(sem):
    return pltpu.CompilerParams(dimension_semantics=sem, vmem_limit_bytes=VMEM_LIMIT)


def _dot_nt(a, b):
    return lax.dot_general(a, b, NT_DIMS, preferred_element_type=F32)


def _dot(a, b):
    return jnp.dot(a, b, preferred_element_type=F32)


_PROJ_GROUPS = (
    ("qa", 0, WA, HEAD_DIM ** -0.5, False, True),
    ("ka", 384, WA, 1.0, True, True),
    ("va", 768, WA, 1.0, True, True),
    ("qi", 1152, WQI, 1.0, True, False),
    ("misc", 1408, LANES, 1.0, True, True),
    ("qb", 1536, WB, HEAD_DIM ** -0.5, False, True),
    ("kb", 1920, WB, 1.0, True, True),
    ("vb", 2304, WB, 1.0, True, True),
    ("qc", 2688, WC, 1.0, False, True),
    ("kc", 2944, WC, 1.0, True, True),
    ("vc", 3200, WC, 1.0, True, True),
)
_PROJ_NPAD = 3456


def _proj_out_names():
    names = []
    for name, _, _, _, f32o, bf16o in _PROJ_GROUPS:
        if f32o:
            names.append(name + "_f32")
        if bf16o:
            names.append(name + "_bf16")
    return names


def _proj_kernel(x_ref, g_ref, w_ref, *out_refs):
    x = x_ref[...]
    ms = jnp.mean(x * x, axis=-1, keepdims=True)
    h = (x * lax.rsqrt(ms + EPS) * g_ref[...]).astype(BF16)
    k = 0
    for _, off, width, scale, f32o, bf16o in _PROJ_GROUPS:
        r = _dot(h, w_ref[:, off:off + width])
        if f32o:
            out_refs[k][...] = r
            k += 1
        if bf16o:
            out_refs[k][...] = (r * scale).astype(BF16) if scale != 1.0 else r.astype(BF16)
            k += 1


def _pad_w_in(w):
    d = w.shape[0]
    misc = jnp.concatenate([w[:, 1416:1448], w[:, 1408:1416],
                            jnp.zeros((d, LANES - IDX_DIM - IDX_HEADS), w.dtype)], axis=1)
    return jnp.concatenate([w[:, 0:1408], misc, w[:, 1448:]], axis=1).astype(BF16)


def _proj(x2d, g, w_pad):
    m, d = x2d.shape
    tm = min(512, m)
    assert m % tm == 0
    out_shapes, out_specs = [], []
    for _, _, width, _, f32o, bf16o in _PROJ_GROUPS:
        for want, dt in ((f32o, F32), (bf16o, BF16)):
            if want:
                out_shapes.append(jax.ShapeDtypeStruct((m, width), dt))
                out_specs.append(pl.BlockSpec((tm, width), lambda i: (i, 0)))
    outs = pl.pallas_call(
        _proj_kernel,
        grid=(m // tm,),
        in_specs=[pl.BlockSpec((tm, d), lambda i: (i, 0)),
                  pl.BlockSpec((1, d), lambda i: (0, 0)),
                  pl.BlockSpec((d, _PROJ_NPAD), lambda i: (0, 0))],
        out_specs=out_specs,
        out_shape=out_shapes,
        compiler_params=_cparams(("parallel",)),
        name="proj",
    )(x2d, g.reshape(1, d), w_pad)
    return dict(zip(_proj_out_names(), outs))


def _t5_bucket(rel):
    half = NUM_BUCKETS // 2
    max_exact = half // 2
    ret = jnp.where(rel > 0, half, 0)
    n = jnp.abs(rel)
    nf = jnp.maximum(n, 1).astype(F32)
    large = max_exact + (jnp.log(nf / max_exact) / math.log(MAX_DISTANCE / max_exact)
                         * (half - max_exact)).astype(I32)
    large = jnp.minimum(large, half - 1)
    return ret + jnp.where(n < max_exact, n, large)


def _bias_tiles(tab, qb, n_valid):
    qq = jnp.arange(qb, dtype=I32)[:, None]
    kk = jnp.arange(TAIL, dtype=I32)[None, :] - (TAIL - KB)
    tile = tab[_t5_bucket(kk - qq)].astype(F32).transpose(2, 0, 1) * LOG2E
    vis = ((kk // CHUNK) <= (qq // CHUNK)) & (kk < n_valid)
    tile = jnp.where(vis[None], tile, NEG)
    tile1 = jnp.where((kk < 0)[None], NEG, tile)
    far = tab[_t5_bucket(jnp.array(-MAX_DISTANCE, I32))].astype(F32) * LOG2E
    return jnp.stack([tile, tile1]), far


def _front_mask_rows():
    col = jnp.arange(SUP, dtype=I32)[None, None, :]
    v = jnp.arange(SUP // KB, dtype=I32)[:, None, None]
    return jnp.where(col < v * KB, NEG, 0.0).astype(F32)


def _windows(qbi):
    tail_start = (qbi + 1) * KB - TAIL
    n_far = (tail_start + SUP - 1) // SUP
    n_front = (n_far * SUP - tail_start) // KB
    return tail_start, n_far, n_front


def _lane_chunks(x):
    return [x[:, c * LANES:(c + 1) * LANES] for c in range(x.shape[1] // LANES)]


def _attn_step(x, ccol, v, m_ref, l_ref, acc_ref):
    chunks = _lane_chunks(x)
    mx = functools.reduce(jnp.maximum, chunks)
    m_old = m_ref[...]
    m_new = jnp.maximum(m_old, jnp.max(mx, axis=1, keepdims=True) + ccol)
    p = jnp.exp2(x + (ccol - m_new))
    alpha = jnp.exp2(m_old - m_new)
    l_ref[...] = alpha * l_ref[...] + functools.reduce(jnp.add, _lane_chunks(p))
    acc_ref[...] = alpha * acc_ref[...] + _dot(p.astype(BF16), v)
    m_ref[...] = m_new


FOLD_CHAINS = 4
DIFF_GROUPS = 1


def _fold_rows(op, x):
    w, r = x.shape
    x3 = x.reshape(w // 8, 8, r)
    chains = [x3[i] for i in range(FOLD_CHAINS)]
    for i in range(FOLD_CHAINS, w // 8):
        chains[i % FOLD_CHAINS] = op(chains[i % FOLD_CHAINS], x3[i])
    return functools.reduce(op, chains)


def _attn_step_t(x, crow, vt, m_ref, l_ref, acc_ref):
    m_old = m_ref[...]
    m_new = jnp.maximum(m_old, jnp.max(_fold_rows(jnp.maximum, x), axis=0, keepdims=True) + crow)
    p = jnp.exp2(x + (crow - m_new))
    alpha = jnp.exp2(m_old - m_new)
    l_ref[...] = alpha * l_ref[...] + _fold_rows(jnp.add, p)
    acc_ref[...] = alpha * acc_ref[...] + _dot(vt, p.astype(BF16))
    m_ref[...] = m_new


def _front_mask_t(n_front, shape):
    return jnp.where(lax.broadcasted_iota(I32, shape, 0) < n_front * KB, NEG, 0.0)


def _blocked_t(v):
    b, s, w = v.shape
    return v.reshape(b, s // KB, KB, w).transpose(0, 1, 3, 2)


def _window_t(k_ref, vt_ref, start, nblk):
    off = pl.multiple_of(start + PADF, KB)
    k = k_ref[0, pl.ds(off, nblk * KB), :]
    v4 = vt_ref[0, pl.ds((start + PADF) // KB, nblk)]
    vt = jnp.concatenate([v4[n] for n in range(nblk)], axis=1)
    return k, vt


def _key_spec(s_tot, width, index_map):
    return pl.BlockSpec((1, s_tot, width), index_map, pipeline_mode=pl.Buffered(1))


def _diff_kernel(lam_ref, gain_ref, btail_ref, cfar_ref, fmask_ref, q_ref, k_ref, v_ref, o_ref,
                 m_ref, l_ref, acc_ref, *, qb, qb0, lam0):
    qbi = qb0 + pl.program_id(2)
    tail_start, n_far, n_front = _windows(qbi)
    q = q_ref[0]
    lane = lax.broadcasted_iota(I32, (qb, LANES), 1)
    qs = jnp.concatenate(
        [jnp.where((lane >= DC_HALF * c) & (lane < DC_HALF * (c + 1)), q, jnp.zeros_like(q))
         for c in range(4)], axis=0)
    m_ref[...] = jnp.full(m_ref.shape, NEG, F32)
    l_ref[...] = jnp.zeros(l_ref.shape, F32)
    acc_ref[...] = jnp.zeros(acc_ref.shape, F32)
    a = DC_HALF ** -0.5 * LOG2E

    def window(start, width):
        off = pl.multiple_of(start + PADF, KB)
        return k_ref[0, pl.ds(off, width), :], v_ref[0, pl.ds(off, width), :]

    k, v = window(tail_start, TAIL)
    bt = btail_ref[jnp.where(qbi == 0, 1, 0)]
    x = (_dot_nt(qs, k) * a).reshape(2, 2, qb, TAIL) + bt[:, None]
    _attn_step(x.reshape(4 * qb, TAIL), 0.0, v, m_ref, l_ref, acc_ref)

    ccol = cfar_ref[0]

    def far_body(j, carry):
        k, v = window(tail_start - SUP * (j + 1), SUP)
        _attn_step(_dot_nt(qs, k) * a, ccol, v, m_ref, l_ref, acc_ref)
        return carry

    lax.fori_loop(0, n_far - 1, far_body, 0)

    @pl.when(n_far >= 1)
    def _():
        k, v = window(tail_start - SUP * n_far, SUP)
        _attn_step(_dot_nt(qs, k) * a + fmask_ref[n_front], ccol, v, m_ref, l_ref, acc_ref)

    lam_v = lam_ref[...]
    s1 = jnp.sum(lam_v[0:1] * lam_v[1:2], axis=1, keepdims=True)
    s2 = jnp.sum(lam_v[2:3] * lam_v[3:4], axis=1, keepdims=True)
    lam = jnp.exp(s1) - jnp.exp(s2) + lam0
    o4 = acc_ref[...] / jnp.sum(l_ref[...], axis=1, keepdims=True)
    o0 = o4[0:qb] - lam * o4[qb:2 * qb]
    o1 = o4[2 * qb:3 * qb] - lam * o4[3 * qb:4 * qb]
    first = lane < HEAD_DIM
    o = jnp.where(first, o0, o1)
    sq = o * o
    ms0 = jnp.sum(jnp.where(first, sq, 0.0), axis=1, keepdims=True)
    ms1 = jnp.sum(jnp.where(first, 0.0, sq), axis=1, keepdims=True)
    ms = jnp.where(first, ms0, ms1) * (1.0 / HEAD_DIM)
    y = o * lax.rsqrt(ms + EPS) * gain_ref[...]
    o_ref[0] = (y * (1.0 - lam0)).astype(o_ref.dtype)


def _diff_attn(q, k, v, btail, cfar, lamv, gain2, *, qb, qb0, lam0):
    b, t, _ = q.shape
    s_tot = k.shape[1]
    nq = t // qb
    npair = N_HEADS_C // 2
    ccol = jnp.repeat(cfar.reshape(npair, 2), 2 * qb, axis=1).reshape(npair, 4 * qb, 1)
    btail = btail.reshape(2, npair, 2, qb, TAIL)
    kern = functools.partial(_diff_kernel, qb=qb, qb0=qb0, lam0=lam0)
    return pl.pallas_call(
        kern,
        grid=(b, npair, nq),
        in_specs=[
            pl.BlockSpec((4, LANES), lambda bi, p, i: (0, 0)),
            pl.BlockSpec((1, LANES), lambda bi, p, i: (0, 0)),
            pl.BlockSpec((2, None, 2, qb, TAIL), lambda bi, p, i: (0, p, 0, 0, 0)),
            pl.BlockSpec((1, 4 * qb, 1), lambda bi, p, i: (p, 0, 0)),
            pl.BlockSpec((SUP // KB, 1, SUP), lambda bi, p, i: (0, 0, 0)),
            pl.BlockSpec((1, qb, LANES), lambda bi, p, i: (bi, i, p)),
            _key_spec(s_tot, LANES, lambda bi, p, i: (bi, 0, p)),
            _key_spec(s_tot, LANES, lambda bi, p, i: (bi, 0, p)),
        ],
        out_specs=pl.BlockSpec((1, qb, LANES), lambda bi, p, i: (bi, i, p)),
        out_shape=jax.ShapeDtypeStruct((b, t, WC), BF16),
        scratch_shapes=[pltpu.VMEM((4 * qb, 1), F32), pltpu.VMEM((4 * qb, LANES), F32),
                        pltpu.VMEM((4 * qb, LANES), F32)],
        compiler_params=_cparams(("parallel", "parallel", "arbitrary")),
        name="diff_attn",
    )(lamv, gain2, btail, ccol, _front_mask_rows(), q, k, v)


def _diff_kernel_t(lam_ref, gain_ref, btail_ref, crow_ref, qt_ref, k_ref, vt_ref, o_ref,
                   m_ref, l_ref, acc_ref, *, qb, qb0, lam0):
    qbi = qb0 + pl.program_id(2)
    tail_start, n_far, n_front = _windows(qbi)
    qt = qt_ref[0]
    row = lax.broadcasted_iota(I32, (LANES, qb), 0)
    qst = jnp.concatenate(
        [jnp.where((row >= DC_HALF * c) & (row < DC_HALF * (c + 1)), qt, jnp.zeros_like(qt))
         for c in range(4)], axis=1)
    m_ref[...] = jnp.full(m_ref.shape, NEG, F32)
    l_ref[...] = jnp.zeros(l_ref.shape, F32)
    acc_ref[...] = jnp.zeros(acc_ref.shape, F32)
    a = DC_HALF ** -0.5 * LOG2E

    gw = 4 * qb // DIFF_GROUPS

    def steps(starts, nblk, add, far):
        wins = [_window_t(k_ref, vt_ref, start, nblk) for start in starts]
        raws = [_dot(k, qst) for k, _ in wins]
        for (_, vt), raw in zip(wins, raws):
            for g in range(DIFF_GROUPS):
                cols = slice(g * gw, (g + 1) * gw)
                x = raw[:, cols] * a
                if add is not None:
                    x = x + add(cols)
                _attn_step_t(x, crow_ref[0, :, cols] if far else 0.0, vt,
                             m_ref.at[g], l_ref.at[g], acc_ref.at[g])

    variant = jnp.where(qbi == 0, 1, 0)
    steps([tail_start], TAIL // KB, lambda cols: btail_ref[variant, :, cols], False)

    far_start = lambda j: tail_start - SUP * (j + 1)
    n_full = n_far - 1

    def far_body(jj, carry):
        steps([far_start(2 * jj), far_start(2 * jj + 1)], SUP // KB, None, True)
        return carry

    lax.fori_loop(0, n_full // 2, far_body, 0)

    @pl.when(jnp.logical_and(n_full > 0, n_full % 2 == 1))
    def _():
        steps([far_start(n_full - 1)], SUP // KB, None, True)

    @pl.when(n_far >= 1)
    def _():
        fm = _front_mask_t(n_front, (SUP, gw))
        steps([far_start(n_far - 1)], SUP // KB, lambda cols: fm, True)

    acc = jnp.concatenate([acc_ref[g] for g in range(DIFF_GROUPS)], axis=1)
    l_sum = jnp.concatenate([jnp.sum(l_ref[g], axis=0, keepdims=True) for g in range(DIFF_GROUPS)], axis=1)

    lam_v = lam_ref[...]
    s1 = jnp.sum(lam_v[0:1] * lam_v[1:2], axis=1, keepdims=True)
    s2 = jnp.sum(lam_v[2:3] * lam_v[3:4], axis=1, keepdims=True)
    lam = jnp.exp(s1) - jnp.exp(s2) + lam0
    o4 = acc / l_sum
    o0 = o4[:, 0:qb] - lam * o4[:, qb:2 * qb]
    o1 = o4[:, 2 * qb:3 * qb] - lam * o4[:, 3 * qb:4 * qb]
    first = row < HEAD_DIM
    o = jnp.where(first, o0, o1)
    sq = o * o
    ms0 = jnp.sum(sq[0:HEAD_DIM], axis=0, keepdims=True)
    ms1 = jnp.sum(sq[HEAD_DIM:], axis=0, keepdims=True)
    ms = jnp.where(first, ms0, ms1) * (1.0 / HEAD_DIM)
    y = o * lax.rsqrt(ms + EPS) * gain_ref[...]
    o_ref[0] = (y * (1.0 - lam0)).astype(o_ref.dtype)


def _diff_attn_t(q, k, v, btail, cfar, lamv, gain2, *, qb, qb0, lam0):
    b, t, _ = q.shape
    s_tot = k.shape[1]
    nq = t // qb
    npair = N_HEADS_C // 2
    qt = q.transpose(0, 2, 1)
    vt = _blocked_t(v)
    crow = jnp.repeat(cfar.reshape(npair, 2), 2 * qb, axis=1).reshape(npair, 1, 4 * qb)
    bt = btail.reshape(2, npair, 2, qb, TAIL).transpose(0, 1, 4, 2, 3)
    bt = jnp.repeat(bt[:, :, :, :, None, :], 2, axis=4).reshape(2, npair, TAIL, 4 * qb)
    gain_t = jnp.broadcast_to(gain2.reshape(LANES, 1), (LANES, qb))
    kern = functools.partial(_diff_kernel_t, qb=qb, qb0=qb0, lam0=lam0)
    out_t = pl.pallas_call(
        kern,
        grid=(b, npair, nq),
        in_specs=[
            pl.BlockSpec((4, LANES), lambda bi, p, i: (0, 0)),
            pl.BlockSpec((LANES, qb), lambda bi, p, i: (0, 0)),
            pl.BlockSpec((2, None, TAIL, 4 * qb), lambda bi, p, i: (0, p, 0, 0)),
            pl.BlockSpec((1, 1, 4 * qb), lambda bi, p, i: (p, 0, 0)),
            pl.BlockSpec((1, LANES, qb), lambda bi, p, i: (bi, p, i)),
            _key_spec(s_tot, LANES, lambda bi, p, i: (bi, 0, p)),
            pl.BlockSpec((1, s_tot // KB, LANES, KB), lambda bi, p, i: (bi, 0, p, 0),
                         pipeline_mode=pl.Buffered(1)),
        ],
        out_specs=pl.BlockSpec((1, LANES, qb), lambda bi, p, i: (bi, p, i)),
        out_shape=jax.ShapeDtypeStruct((b, WC, t), BF16),
        scratch_shapes=[pltpu.VMEM((DIFF_GROUPS, 1, 4 * qb // DIFF_GROUPS), F32),
                        pltpu.VMEM((DIFF_GROUPS, 8, 4 * qb // DIFF_GROUPS), F32),
                        pltpu.VMEM((DIFF_GROUPS, LANES, 4 * qb // DIFF_GROUPS), F32)],
        compiler_params=_cparams(("parallel", "parallel", "arbitrary")),
        name="diff_attn_t",
    )(lamv, gain_t, bt, crow, qt, k, vt)
    return out_t.transpose(0, 2, 1)


def _sb_kernel(q_ref, k_ref, v_ref, o_ref, carry_ref, acc_ref, *, qb, qb0):
    qbi = qb0 + pl.program_id(2)
    q = q_ref[0]
    lane = lax.broadcasted_iota(I32, (qb, LANES), 1)
    causal = lax.broadcasted_iota(I32, (qb, KB), 1) < lax.broadcasted_iota(I32, (qb, KB), 0)
    tri = jnp.where(lax.broadcasted_iota(I32, (KB, KB), 0) > lax.broadcasted_iota(I32, (KB, KB), 1),
                    1.0, 0.0).astype(BF16)
    outs = []
    for hh in range(2):
        qz = jnp.where((lane >= HEAD_DIM * hh) & (lane < HEAD_DIM * (hh + 1)), q, jnp.zeros_like(q))
        carry_ref[...] = jnp.zeros(carry_ref.shape, F32)
        acc_ref[...] = jnp.zeros(acc_ref.shape, F32)

        def block(j, masked, qz=qz):
            off = pl.multiple_of(j * KB, KB)
            k = k_ref[0, pl.ds(off, KB), :]
            v = v_ref[0, pl.ds(off, KB), :]
            z = _dot_nt(qz, k)
            sp = jnp.maximum(z, 0.0) + jnp.log1p(jnp.exp(-jnp.abs(z)))
            lg = -sp
            if masked:
                lg = jnp.where(causal, lg, 0.0)
            l1 = lg.astype(BF16)
            r1 = lg - l1.astype(F32)
            l2 = r1.astype(BF16)
            l3 = (r1 - l2.astype(F32)).astype(BF16)
            tail = _dot(l1, tri) + _dot(l2, tri) + _dot(l3, tri)
            carry = carry_ref[...]
            a = jnp.exp(z - sp + tail + carry)
            if masked:
                a = jnp.where(causal, a, 0.0)
            acc_ref[...] += _dot(a.astype(BF16), v)
            carry_new = carry + jnp.sum(lg, axis=1, keepdims=True)
            carry_ref[...] = carry_new
            return jnp.max(carry_new)

        mx0 = block(qbi, True)

        def cond(st):
            return jnp.logical_and(st[0] >= 0, st[1] > EXP_ZERO)

        def body(st):
            return st[0] - 1, block(st[0], False)

        lax.while_loop(cond, body, (qbi - 1, mx0))
        outs.append(acc_ref[...])
    o_ref[0] = jnp.where(lane < HEAD_DIM, outs[0], outs[1]).astype(o_ref.dtype)


def _sb_attn(q, k, v, *, qb, qb0):
    b, t, _ = q.shape
    s_pad = k.shape[1]
    nq = t // qb
    kern = functools.partial(_sb_kernel, qb=qb, qb0=qb0)
    return pl.pallas_call(
        kern,
        grid=(b, N_HEADS_B // 2, nq),
        in_specs=[
            pl.BlockSpec((1, qb, LANES), lambda bi, p, i: (bi, i, p)),
            pl.BlockSpec((1, s_pad, LANES), lambda bi, p, i: (bi, 0, p)),
            pl.BlockSpec((1, s_pad, LANES), lambda bi, p, i: (bi, 0, p)),
        ],
        out_specs=pl.BlockSpec((1, qb, LANES), lambda bi, p, i: (bi, i, p)),
        out_shape=jax.ShapeDtypeStruct((b, t, WB), BF16),
        scratch_shapes=[pltpu.VMEM((qb, 1), F32), pltpu.VMEM((qb, LANES), F32)],
        compiler_params=_cparams(("parallel", "parallel", "arbitrary")),
        name="sb_attn",
    )(q, k, v)


def _dsa_kernel(btail_ref, cfar_ref, fmask_ref, qa_ref, qi_ref, wi_ref, ka_ref, va_ref, ki_ref,
                o_ref, kxt_ref, kxf_ref, wb_ref, r_ref, cnt_ref, cs_ref, m_ref, l_ref, acc_ref,
                *, qb, qb0, topk, idx_bits):
    npair = N_HEADS_A // 2
    qbi = qb0 + pl.program_id(1)
    tail_start, n_far, n_front = _windows(qbi)
    lane = lax.broadcasted_iota(I32, (qb, LANES), 1)

    def window(ref, start, width):
        off = pl.multiple_of(start + PADF, KB)
        return ref[0, pl.ds(off, width), :]

    qi = qi_ref[0]
    parts = []
    for h in range(IDX_HEADS):
        g, sh = divmod(h * IDX_DIM, LANES)
        part = qi[:, g * LANES:(g + 1) * LANES]
        if sh:
            part = pltpu.roll(part, LANES - sh, 1)
        parts.append(jnp.where(lane < IDX_DIM, part, 0.0).astype(BF16))
    qis = jnp.concatenate(parts, axis=0)
    wi = wi_ref[0] * (IDX_HEADS ** -0.5 * IDX_DIM ** -0.5)
    for h in range(IDX_HEADS):
        wb_ref[h * qb:(h + 1) * qb, :] = jnp.broadcast_to(wi[:, IDX_DIM + h:IDX_DIM + h + 1],
                                                           (qb, LANES))
    qa = qa_ref[0]
    qas = []
    for p in range(npair):
        part = qa[:, p * LANES:(p + 1) * LANES]
        qas.append(jnp.concatenate(
            [jnp.where(lane < HEAD_DIM, part, jnp.zeros_like(part)),
             jnp.where(lane < HEAD_DIM, jnp.zeros_like(part), part)], axis=0))

    def keys_of(start, width, hidden):
        d = _dot_nt(qis, window(ki_ref, start, width))
        wb = wb_ref[...]
        cols = []
        for dc in _lane_chunks(d):
            r = jnp.maximum(dc, 0.0) * wb
            cols.append(functools.reduce(jnp.add, [r[h * qb:(h + 1) * qb] for h in range(IDX_HEADS)]))
        sc = jnp.concatenate(cols, axis=1)
        sc = jnp.where(sc == 0.0, 0.0, sc)
        bits = lax.bitcast_convert_type(sc, I32)
        key = bits ^ ((bits >> 31) & 0x7FFFFFFF)
        return jnp.where(hidden, KEY_NEG_INF, key)

    bt0 = btail_ref[jnp.where(qbi == 0, 1, 0), 0]
    kxt_ref[...] = keys_of(tail_start, TAIL, bt0 < 0.5 * NEG)

    def score_body(j, carry):
        kxf_ref[j] = keys_of(tail_start - SUP * (j + 1), SUP, False)
        return carry

    lax.fori_loop(0, n_far - 1, score_body, 0)

    @pl.when(n_far >= 1)
    def _():
        kxf_ref[n_far - 1] = keys_of(tail_start - SUP * n_far, SUP, fmask_ref[n_front] < 0.5 * NEG)

    col_t = lax.broadcasted_iota(I32, (qb, TAIL), 1)
    col_f = lax.broadcasted_iota(I32, (qb, SUP), 1)

    def count(pred_t, pred_f):
        def body(j, acc):
            hit = jnp.where(pred_f(kxf_ref[j], col_f + (tail_start - SUP * (j + 1))), 1.0, 0.0)
            return acc + functools.reduce(jnp.add, _lane_chunks(hit))
        acc = functools.reduce(jnp.add, _lane_chunks(
            jnp.where(pred_t(kxt_ref[...], col_t + tail_start), 1.0, 0.0)))
        acc = lax.fori_loop(0, n_far, body, acc)
        return jnp.sum(acc, axis=1, keepdims=True)

    def count1(make_pred):
        return count(make_pred(TAIL), make_pred(SUP))

    wide = lambda a, w: jnp.broadcast_to(a, (qb, w))

    def ge(a):
        return lambda w: (lambda kx, idx, aw=wide(a, w): kx >= aw)

    def gt(a):
        return lambda w: (lambda kx, idx, aw=wide(a, w): kx > aw)

    def eq(a):
        return lambda w: (lambda kx, idx, aw=wide(a, w): kx == aw)

    def eq_before(a, c):
        return lambda w: (lambda kx, idx, aw=wide(a, w), cw=wide(c, w): (kx == aw) & (idx < cw))

    r_ref[...] = jnp.full((qb, 1), INT_MIN, I32)
    cnt_ref[...] = jnp.full((qb, 1), float(TAIL + SUP * 64), F32)

    def bit_cond(st):
        return jnp.logical_and(st[0] < 32, st[1] > 0.0)

    def bit_body(st):
        t = st[0]
        r = r_ref[...]
        cand = r + lax.shift_left(jnp.int32(1), 31 - t)
        cnt = count1(ge(cand))
        take = cnt >= topk
        r_ref[...] = jnp.where(take, cand, r)
        cnt_new = jnp.where(take, cnt, cnt_ref[...])
        cnt_ref[...] = cnt_new
        return t + 1, jnp.max(jnp.where(cnt_new == topk, 0.0, 1.0))

    lax.while_loop(bit_cond, bit_body, (jnp.int32(0), jnp.float32(1.0)))
    r1 = r_ref[...]

    need = topk - count1(gt(r1))
    n_eq = count1(eq(r1))
    cs_ref[...] = jnp.full((qb, 1), 2 ** idx_bits - 1, I32)
    overflow = jnp.where((n_eq > need) & (r1 > KEY_NEG_INF), 1.0, 0.0)

    @pl.when(jnp.max(overflow) > 0.0)
    def _():
        def tie_body(t, c):
            cand = c + lax.shift_left(jnp.int32(1), idx_bits - 1 - t)
            cnt = count1(eq_before(r1, cand))
            return jnp.where(cnt < need, cand, c)
        cs_ref[...] = lax.fori_loop(0, idx_bits, tie_body, jnp.zeros((qb, 1), I32))

    cs1 = cs_ref[...]

    m_ref[...] = jnp.full(m_ref.shape, NEG, F32)
    l_ref[...] = jnp.zeros(l_ref.shape, F32)
    acc_ref[...] = jnp.zeros(acc_ref.shape, F32)

    def sel_mask(kx, idx, w):
        sel = (kx > wide(r1, w)) | ((kx == wide(r1, w)) & (idx <= wide(cs1, w)))
        return jnp.where(sel & (kx > KEY_NEG_INF), 0.0, NEG)

    def attend(start, width, madd, bias, ccols):
        k = window(ka_ref, start, width)
        v = window(va_ref, start, width)
        for p in range(npair):
            x = (_dot_nt(qas[p], k[:, p * LANES:(p + 1) * LANES]) * LOG2E).reshape(2, qb, width) + madd[None]
            if bias is not None:
                x = x + bias[2 * p:2 * p + 2]
            _attn_step(x.reshape(2 * qb, width), ccols[p], v[:, p * LANES:(p + 1) * LANES],
                       m_ref.at[p], l_ref.at[p], acc_ref.at[p])

    bt = btail_ref[jnp.where(qbi == 0, 1, 0)]
    attend(tail_start, TAIL, sel_mask(kxt_ref[...], col_t + tail_start, TAIL), bt, [0.0] * npair)

    ccols = [cfar_ref[p] for p in range(npair)]

    def far_body(j, carry):
        start = tail_start - SUP * (j + 1)
        attend(start, SUP, sel_mask(kxf_ref[j], col_f + start, SUP), None, ccols)
        return carry

    lax.fori_loop(0, n_far, far_body, 0)

    for p in range(npair):
        o2 = acc_ref[p] / jnp.sum(l_ref[p], axis=1, keepdims=True)
        o = jnp.where(lane < HEAD_DIM, o2[0:qb], o2[qb:2 * qb])
        o_ref[0, :, p * LANES:(p + 1) * LANES] = o.astype(o_ref.dtype)


def _dsa_attn(qa, qi, misc_f32, ka, va, ki, btail, cfar, *, qb, qb0, topk):
    b, t, _ = qa.shape
    s_tot = ka.shape[1]
    nq = t // qb
    npair = N_HEADS_A // 2
    n_far_max = max(1, -(-(s_tot - PADF - TAIL) // SUP))
    assert n_far_max <= 64
    idx_bits = int(math.ceil(math.log2(s_tot))) + 1
    ccol = jnp.repeat(cfar.reshape(npair, 2), qb, axis=1).reshape(npair, 2 * qb, 1)
    kern = functools.partial(_dsa_kernel, qb=qb, qb0=qb0, topk=float(topk), idx_bits=idx_bits)
    return pl.pallas_call(
        kern,
        grid=(b, nq),
        in_specs=[
            pl.BlockSpec((2, N_HEADS_A, qb, TAIL), lambda bi, i: (0, 0, 0, 0)),
            pl.BlockSpec((npair, 2 * qb, 1), lambda bi, i: (0, 0, 0)),
            pl.BlockSpec((SUP // KB, 1, SUP), lambda bi, i: (0, 0, 0)),
            pl.BlockSpec((1, qb, WA), lambda bi, i: (bi, i, 0)),
            pl.BlockSpec((1, qb, WQI), lambda bi, i: (bi, i, 0)),
            pl.BlockSpec((1, qb, LANES), lambda bi, i: (bi, i, 0)),
            _key_spec(s_tot, WA, lambda bi, i: (bi, 0, 0)),
            _key_spec(s_tot, WA, lambda bi, i: (bi, 0, 0)),
            _key_spec(s_tot, LANES, lambda bi, i: (bi, 0, 0)),
        ],
        out_specs=pl.BlockSpec((1, qb, WA), lambda bi, i: (bi, i, 0)),
        out_shape=jax.ShapeDtypeStruct((b, t, WA), BF16),
        scratch_shapes=[
            pltpu.VMEM((qb, TAIL), I32),
            pltpu.VMEM((n_far_max, qb, SUP), I32),
            pltpu.VMEM((IDX_HEADS * qb, LANES), F32),
            pltpu.VMEM((qb, 1), I32),
            pltpu.VMEM((qb, 1), F32),
            pltpu.VMEM((qb, 1), I32),
            pltpu.VMEM((npair, 2 * qb, 1), F32),
            pltpu.VMEM((npair, 2 * qb, LANES), F32),
            pltpu.VMEM((npair, 2 * qb, LANES), F32),
        ],
        compiler_params=_cparams(("parallel", "arbitrary")),
        name="dsa_attn",
    )(btail, ccol, _front_mask_rows(), qa, qi, misc_f32, ka, va, ki)


def _dsa_kernel_t(btail_ref, crow_ref, qat_ref, qit_ref, wit_ref, ka_ref, vat_ref, ki_ref, o_ref,
                  kxt_ref, kxf_ref, m_ref, l_ref, acc_ref, *, qb, qb0, topk, idx_bits):
    npair = N_HEADS_A // 2
    qbi = qb0 + pl.program_id(1)
    tail_start, n_far, n_front = _windows(qbi)
    row = lax.broadcasted_iota(I32, (LANES, qb), 0)

    qit = qit_ref[0]
    zpad = jnp.zeros((LANES - IDX_DIM, qb), BF16)
    qist = jnp.concatenate(
        [jnp.concatenate([qit[h * IDX_DIM:(h + 1) * IDX_DIM].astype(BF16), zpad], axis=0)
         for h in range(IDX_HEADS)], axis=1)
    wit = wit_ref[0] * (IDX_HEADS ** -0.5 * IDX_DIM ** -0.5)
    qat = qat_ref[0]
    qast = []
    for p in range(npair):
        part = qat[p * LANES:(p + 1) * LANES]
        qast.append(jnp.concatenate(
            [jnp.where(row < HEAD_DIM, part, jnp.zeros_like(part)),
             jnp.where(row < HEAD_DIM, jnp.zeros_like(part), part)], axis=1))

    def ki_window(start, width):
        return ki_ref[0, pl.ds(pl.multiple_of(start + PADF, KB), width), :]

    def keys_of(start, width, hidden):
        d = _dot(ki_window(start, width), qist)
        sc = functools.reduce(jnp.add, [
            jnp.maximum(d[:, h * qb:(h + 1) * qb], 0.0) * wit[h:h + 1] for h in range(IDX_HEADS)])
        sc = jnp.where(sc == 0.0, 0.0, sc)
        bits = lax.bitcast_convert_type(sc, I32)
        key = bits ^ ((bits >> 31) & 0x7FFFFFFF)
        return key if hidden is None else jnp.where(hidden, KEY_NEG_INF, key)

    variant = jnp.where(qbi == 0, 1, 0)
    kxt_ref[...] = keys_of(tail_start, TAIL, btail_ref[variant, 0, :, 0:qb] < 0.5 * NEG)

    def score_body(j, carry):
        kxf_ref[j] = keys_of(tail_start - SUP * (j + 1), SUP, None)
        return carry

    lax.fori_loop(0, n_far - 1, score_body, 0)

    @pl.when(n_far >= 1)
    def _():
        kxf_ref[n_far - 1] = keys_of(tail_start - SUP * n_far, SUP,
                                     lax.broadcasted_iota(I32, (SUP, qb), 0) < n_front * KB)

    row_t = lax.broadcasted_iota(I32, (TAIL, qb), 0)
    row_f = lax.broadcasted_iota(I32, (SUP, qb), 0)

    def hits(pred, kx, idx):
        return _fold_rows(jnp.add, jnp.where(pred(kx, idx), 1.0, 0.0))

    def count(pred):
        def body(j, acc):
            return acc + hits(pred, kxf_ref[j], row_f + (tail_start - SUP * (j + 1)))
        acc = lax.fori_loop(0, n_far, body, hits(pred, kxt_ref[...], row_t + tail_start))
        return jnp.sum(acc, axis=0, keepdims=True)

    def bit_cond(st):
        return jnp.logical_and(st[0] < 32, st[3] > 0.0)

    def bit_body(st):
        t, r, cnt_r, _ = st
        cand = r + lax.shift_left(jnp.int32(1), 31 - t)
        cnt = count(lambda kx, idx: kx >= cand)
        take = cnt >= topk
        cnt_new = jnp.where(take, cnt, cnt_r)
        return t + 1, jnp.where(take, cand, r), cnt_new, jnp.max(jnp.where(cnt_new == topk, 0.0, 1.0))

    _, r1, _, _ = lax.while_loop(
        bit_cond, bit_body,
        (jnp.int32(0), jnp.full((1, qb), INT_MIN, I32), jnp.full((1, qb), float(2 ** 30), F32),
         jnp.float32(1.0)))

    need = topk - count(lambda kx, idx: kx > r1)
    n_eq = count(lambda kx, idx: kx == r1)
    overflow = jnp.where((n_eq > need) & (r1 > KEY_NEG_INF), 1.0, 0.0)

    def tie_search(_):
        def tie_body(t, c):
            cand = c + lax.shift_left(jnp.int32(1), idx_bits - 1 - t)
            cnt = count(lambda kx, idx: (kx == r1) & (idx < cand))
            return jnp.where(cnt < need, cand, c)
        return lax.fori_loop(0, idx_bits, tie_body, jnp.zeros((1, qb), I32))

    cs1 = lax.cond(jnp.max(overflow) > 0.0, tie_search,
                   lambda _: jnp.full((1, qb), 2 ** idx_bits - 1, I32), 0)

    m_ref[...] = jnp.full(m_ref.shape, NEG, F32)
    l_ref[...] = jnp.zeros(l_ref.shape, F32)
    acc_ref[...] = jnp.zeros(acc_ref.shape, F32)

    def sel_mask(kx, idx):
        sel = (kx > r1) | ((kx == r1) & (idx <= cs1))
        madd = jnp.where(sel & (kx > KEY_NEG_INF), 0.0, NEG)
        return jnp.concatenate([madd, madd], axis=1)

    def attend(starts, nblk, madds, bias, crows):
        wins = [_window_t(ka_ref, vat_ref, start, nblk) for start in starts]
        raws = [[_dot(k[:, p * LANES:(p + 1) * LANES], qast[p]) for p in range(npair)] for k, _ in wins]
        for (_, vt), raw, madd2 in zip(wins, raws, madds):
            for p in range(npair):
                x = raw[p] * LOG2E + madd2
                if bias is not None:
                    x = x + bias[:, 2 * p * qb:2 * (p + 1) * qb]
                _attn_step_t(x, crows[p], vt[p * LANES:(p + 1) * LANES],
                             m_ref.at[p], l_ref.at[p], acc_ref.at[p])

    attend([tail_start], TAIL // KB, [sel_mask(kxt_ref[...], row_t + tail_start)],
           btail_ref[variant, 0], [0.0] * npair)

    crows = [crow_ref[p] for p in range(npair)]
    far_start = lambda j: tail_start - SUP * (j + 1)

    def attend_far(js):
        attend([far_start(j) for j in js], SUP // KB,
               [sel_mask(kxf_ref[j], row_f + far_start(j)) for j in js], None, crows)

    def far_body(jj, carry):
        attend_far([2 * jj, 2 * jj + 1])
        return carry

    lax.fori_loop(0, n_far // 2, far_body, 0)

    @pl.when(n_far % 2 == 1)
    def _():
        attend_far([n_far - 1])

    for p in range(npair):
        o2 = acc_ref[p] / jnp.sum(l_ref[p], axis=0, keepdims=True)
        o = jnp.where(row < HEAD_DIM, o2[:, 0:qb], o2[:, qb:2 * qb])
        o_ref[0, p * LANES:(p + 1) * LANES, :] = o.astype(o_ref.dtype)


def _dsa_attn_t(qa, qi, misc_f32, ka, va, ki, btail, cfar, *, qb, qb0, topk):
    b, t, _ = qa.shape
    s_tot = ka.shape[1]
    nq = t // qb
    npair = N_HEADS_A // 2
    n_far_max = max(1, -(-(s_tot - PADF - TAIL) // SUP))
    idx_bits = int(math.ceil(math.log2(s_tot))) + 1
    qat = qa.transpose(0, 2, 1)
    qit = qi.transpose(0, 2, 1)
    wit = misc_f32[:, :, IDX_DIM:IDX_DIM + IDX_HEADS].transpose(0, 2, 1)
    vat = _blocked_t(va)
    crow = jnp.repeat(cfar.reshape(npair, 2), qb, axis=1).reshape(npair, 1, 2 * qb)
    bt = btail.transpose(0, 3, 1, 2).reshape(2, 1, TAIL, N_HEADS_A * qb)
    kern = functools.partial(_dsa_kernel_t, qb=qb, qb0=qb0, topk=float(topk), idx_bits=idx_bits)
    out_t = pl.pallas_call(
        kern,
        grid=(b, nq),
        in_specs=[
            pl.BlockSpec((2, 1, TAIL, N_HEADS_A * qb), lambda bi, i: (0, 0, 0, 0)),
            pl.BlockSpec((npair, 1, 2 * qb), lambda bi, i: (0, 0, 0)),
            pl.BlockSpec((1, WA, qb), lambda bi, i: (bi, 0, i)),
            pl.BlockSpec((1, WQI, qb), lambda bi, i: (bi, 0, i)),
            pl.BlockSpec((1, IDX_HEADS, qb), lambda bi, i: (bi, 0, i)),
            _key_spec(s_tot, WA, lambda bi, i: (bi, 0, 0)),
            pl.BlockSpec((1, s_tot // KB, WA, KB), lambda bi, i: (bi, 0, 0, 0),
                         pipeline_mode=pl.Buffered(1)),
            _key_spec(s_tot, LANES, lambda bi, i: (bi, 0, 0)),
        ],
        out_specs=pl.BlockSpec((1, WA, qb), lambda bi, i: (bi, 0, i)),
        out_shape=jax.ShapeDtypeStruct((b, WA, t), BF16),
        scratch_shapes=[
            pltpu.VMEM((TAIL, qb), I32),
            pltpu.VMEM((n_far_max, SUP, qb), I32),
            pltpu.VMEM((npair, 1, 2 * qb), F32),
            pltpu.VMEM((npair, 8, 2 * qb), F32),
            pltpu.VMEM((npair, LANES, 2 * qb), F32),
        ],
        compiler_params=_cparams(("parallel", "arbitrary")),
        name="dsa_attn_t",
    )(bt, crow, qat, qit, wit, ka, vat, ki)
    return out_t.transpose(0, 2, 1)


def _outproj_kernel(oa_ref, ob_ref, oc_ref, wa_ref, wb_ref, wc_ref, g_ref, x_ref, y_ref):
    mix = _dot(oa_ref[...], wa_ref[...]) + _dot(ob_ref[...], wb_ref[...]) + _dot(oc_ref[...], wc_ref[...])
    ms = jnp.mean(mix * mix, axis=-1, keepdims=True)
    y_ref[...] = x_ref[...] + mix * lax.rsqrt(ms + EPS) * g_ref[...]


def _outproj(oa, ob, oc, w_out, g, x2d):
    m, d = x2d.shape
    tm = min(512, m)
    w = w_out.astype(BF16)
    row = lambda i: (i, 0)
    const = lambda i: (0, 0)
    return pl.pallas_call(
        _outproj_kernel,
        grid=(m // tm,),
        in_specs=[pl.BlockSpec((tm, WA), row), pl.BlockSpec((tm, WB), row), pl.BlockSpec((tm, WC), row),
                  pl.BlockSpec((WA, d), const), pl.BlockSpec((WB, d), const), pl.BlockSpec((WC, d), const),
                  pl.BlockSpec((1, d), const), pl.BlockSpec((tm, d), row)],
        out_specs=pl.BlockSpec((tm, d), row),
        out_shape=jax.ShapeDtypeStruct((m, d), F32),
        compiler_params=_cparams(("parallel",)),
        name="outproj",
    )(oa, ob, oc, w[:WA], w[WA:WA + WB], w[WA + WB:], g.reshape(1, d), x2d)


HALO = 16


def _ffn_kernel(x_ref, xh_ref, stg_ref, stv_ref, gpre_ref, gpost_ref, wg_ref, wv_ref, cwg_ref,
                cwv_ref, cbg_ref, cbv_ref, wd_ref, y_ref, convg_ref, convv_ref,
                h_ref, ug_ref, uv_ref, f_ref, *, tm, tiles_per_seq):
    i = pl.program_id(0)
    f = pl.program_id(1)
    first_of_seq = (i % tiles_per_seq) == 0

    def norm(x):
        ms = jnp.mean(x * x, axis=-1, keepdims=True)
        return (x * lax.rsqrt(ms + EPS) * gpre_ref[...]).astype(BF16)

    @pl.when(f == 0)
    def _():
        h_ref[0:HALO, :] = norm(xh_ref[...])
        h_ref[HALO:, :] = norm(x_ref[...])
        f_ref[...] = jnp.zeros(f_ref.shape, F32)

    h = h_ref[...]
    ug_ref[...] = _dot(h, wg_ref[...])
    uv_ref[...] = _dot(h, wv_ref[...])

    @pl.when(first_of_seq)
    def _():
        ug_ref[0:HALO, :] = stg_ref[0]
        uv_ref[0:HALO, :] = stv_ref[0]

    def conv(u_ref, cw_ref, cb_ref):
        u = u_ref[...]
        u1 = pltpu.roll(u, 1, 0)[HALO:, :]
        u2 = pltpu.roll(u, 2, 0)[HALO:, :]
        cw = cw_ref[...]
        return cb_ref[...] + (cw[0:1] * u2 + cw[1:2] * u1 + cw[2:3] * u[HALO:, :])

    gate = conv(ug_ref, cwg_ref, cbg_ref)
    val = conv(uv_ref, cwv_ref, cbv_ref)
    c0 = math.sqrt(2.0 / math.pi)
    gelu = 0.5 * gate * (1.0 + jnp.tanh(c0 * (gate + 0.044715 * (gate * gate * gate))))
    f_ref[...] += _dot((gelu * val).astype(BF16), wd_ref[...])

    convg_ref[0] = ug_ref[tm:, :]
    convv_ref[0] = uv_ref[tm:, :]

    @pl.when(f == pl.num_programs(1) - 1)
    def _():
        ff = f_ref[...]
        ms = jnp.mean(ff * ff, axis=-1, keepdims=True)
        y_ref[...] = x_ref[...] + ff * lax.rsqrt(ms + EPS) * gpost_ref[...]


def _ffn(x2d, state, g_pre, g_post, w_up, conv_w, conv_b, w_down, *, seq):
    m, d = x2d.shape
    b = m // seq
    d_ff = w_down.shape[0]
    tn = 256
    assert d_ff % tn == 0
    nf = d_ff // tn
    tm = min(512, seq)
    assert seq % tm == 0 and tm % HALO == 0
    tiles_per_seq = seq // tm
    n_tiles = m // tm
    st = jnp.pad(state.astype(F32), ((0, 0), (HALO - (CONV_W - 1), 0), (0, 0)))
    wu = w_up.astype(BF16)
    wd = w_down.astype(BF16)
    kern = functools.partial(_ffn_kernel, tm=tm, tiles_per_seq=tiles_per_seq)
    halo_blocks = tm // HALO
    y, conv_g, conv_v = pl.pallas_call(
        kern,
        grid=(n_tiles, nf),
        in_specs=[
            pl.BlockSpec((tm, d), lambda i, f: (i, 0)),
            pl.BlockSpec((HALO, d), lambda i, f: (jnp.maximum(i * halo_blocks - 1, 0), 0)),
            pl.BlockSpec((1, HALO, tn), lambda i, f: (i // tiles_per_seq, 0, f)),
            pl.BlockSpec((1, HALO, tn), lambda i, f: (i // tiles_per_seq, 0, nf + f)),
            pl.BlockSpec((1, d), lambda i, f: (0, 0)),
            pl.BlockSpec((1, d), lambda i, f: (0, 0)),
            pl.BlockSpec((d, tn), lambda i, f: (0, f)),
            pl.BlockSpec((d, tn), lambda i, f: (0, nf + f)),
            pl.BlockSpec((CONV_W, tn), lambda i, f: (0, f)),
            pl.BlockSpec((CONV_W, tn), lambda i, f: (0, nf + f)),
            pl.BlockSpec((1, tn), lambda i, f: (0, f)),
            pl.BlockSpec((1, tn), lambda i, f: (0, nf + f)),
            pl.BlockSpec((tn, d), lambda i, f: (f, 0)),
        ],
        out_specs=[pl.BlockSpec((tm, d), lambda i, f: (i, 0)),
                   pl.BlockSpec((1, HALO, tn), lambda i, f: (i, 0, f)),
                   pl.BlockSpec((1, HALO, tn), lambda i, f: (i, 0, f))],
        out_shape=[jax.ShapeDtypeStruct((m, d), F32),
                   jax.ShapeDtypeStruct((n_tiles, HALO, d_ff), F32),
                   jax.ShapeDtypeStruct((n_tiles, HALO, d_ff), F32)],
        scratch_shapes=[pltpu.VMEM((HALO + tm, d), BF16),
                        pltpu.VMEM((HALO + tm, tn), F32),
                        pltpu.VMEM((HALO + tm, tn), F32),
                        pltpu.VMEM((tm, d), F32)],
        compiler_params=_cparams(("parallel", "arbitrary")),
        name="ffn",
    )(x2d, x2d, st, st, g_pre.reshape(1, d), g_post.reshape(1, d), wu, wu,
      conv_w, conv_w, conv_b.reshape(1, -1), conv_b.reshape(1, -1), wd)
    last = lambda c: c.reshape(b, tiles_per_seq, HALO, d_ff)[:, -1, HALO - (CONV_W - 1):]
    return y, jnp.concatenate([last(conv_g), last(conv_v)], axis=-1)


def _lambda_init(l):
    return 0.8 - 0.6 * math.exp(-0.3 * l)


def _with_past(past, new, front=0):
    b, t, w = new.shape
    rows = [] if front == 0 else [jnp.zeros((b, front, w), BF16)]
    p = 0
    if past is not None:
        p = past.shape[1]
        pk = past.reshape(b, p, -1).astype(BF16)
        if pk.shape[2] < w:
            pk = jnp.pad(pk, ((0, 0), (0, 0), (0, w - pk.shape[2])))
        rows.append(pk)
    rows.append(new)
    s = p + t
    s_pad = -(-s // KB) * KB
    if s_pad > s:
        rows.append(jnp.zeros((b, s_pad - s, w), BF16))
    return rows[0] if len(rows) == 1 else jnp.concatenate(rows, axis=1)


def _layer(x, past, l, prm):
    b, t, d = x.shape
    p_len = 0 if past is None else past["k_a"].shape[1]
    s = p_len + t
    qb = min(KB, t)
    assert p_len % KB == 0 and t % qb == 0 and (qb == KB or t == qb)
    qb0 = p_len // KB
    s_pad = -(-s // KB) * KB
    n_valid = s - (s_pad - KB)
    topk = min(TOPK_MAX, s // 4)
    m = b * t

    pr = _proj(x.reshape(m, d), prm["g_pre_mix"][l], _pad_w_in(prm["w_in"][l]))
    r3 = lambda a: a.reshape(b, t, a.shape[-1])
    get = lambda name: None if past is None else past[name]

    btail_a, cfar_a = _bias_tiles(prm["rel_bias"][:, :N_HEADS_A], qb, n_valid)
    btail_c, cfar_c = _bias_tiles(prm["rel_bias"][:, N_HEADS_A:], qb, n_valid)

    dsa_attn, diff_attn = (_dsa_attn_t, _diff_attn_t) if qb == KB else (_dsa_attn, _diff_attn)
    o_a = dsa_attn(r3(pr["qa_bf16"]), r3(pr["qi_f32"]), r3(pr["misc_f32"]),
                    _with_past(get("k_a"), r3(pr["ka_bf16"]), PADF),
                    _with_past(get("v_a"), r3(pr["va_bf16"]), PADF),
                    _with_past(get("k_i"), r3(pr["misc_bf16"]), PADF),
                    btail_a, cfar_a, qb=qb, qb0=qb0, topk=topk)
    o_b = _sb_attn(r3(pr["qb_bf16"]),
                   _with_past(get("k_b"), r3(pr["kb_bf16"])),
                   _with_past(get("v_b"), r3(pr["vb_bf16"])), qb=qb, qb0=qb0)
    lamv = jnp.pad(jnp.stack([prm["lambda_q1"][l], prm["lambda_k1"][l],
                              prm["lambda_q2"][l], prm["lambda_k2"][l]]).astype(F32),
                   ((0, 0), (0, LANES - DC_HALF)))
    gain2 = jnp.tile(prm["subln_gain"][l].astype(F32), 2).reshape(1, LANES)
    o_c = diff_attn(r3(pr["qc_bf16"]),
                     _with_past(get("k_c"), r3(pr["kc_bf16"]), PADF),
                     _with_past(get("v_c"), r3(pr["vc_bf16"]), PADF),
                     btail_c, cfar_c, lamv, gain2, qb=qb, qb0=qb0, lam0=_lambda_init(l))

    x1 = _outproj(o_a.reshape(m, WA), o_b.reshape(m, WB), o_c.reshape(m, WC),
                  prm["w_out"][l], prm["g_post_mix"][l], x.reshape(m, d))
    state = (jnp.zeros((b, CONV_W - 1, prm["w_up"].shape[2]), F32) if past is None
             else past["conv"])
    x2, new_conv = _ffn(x1, state, prm["g_pre_ffn"][l], prm["g_post_ffn"][l], prm["w_up"][l],
                        prm["conv_w"][l], prm["conv_b"][l], prm["w_down"][l], seq=t)
    new = (pr["ka_f32"].reshape(b, t, N_HEADS_A, HEAD_DIM),
           pr["va_f32"].reshape(b, t, N_HEADS_A, HEAD_DIM),
           pr["misc_f32"][:, :IDX_DIM].reshape(b, t, IDX_DIM),
           pr["kb_f32"].reshape(b, t, N_HEADS_B, HEAD_DIM),
           pr["vb_f32"].reshape(b, t, N_HEADS_B, HEAD_DIM),
           pr["kc_f32"].reshape(b, t, N_HEADS_C, HEAD_DIM),
           pr["vc_f32"].reshape(b, t, N_HEADS_C, HEAD_DIM),
           new_conv)
    return x2.reshape(b, t, d), new


def _run_trunk(x, caches, prm):
    depth = prm["w_in"].shape[0]
    news = []
    for l in range(depth):
        past = None if caches is None else {k: v[l] for k, v in caches.items()}
        x, new = _layer(x, past, l, prm)
        news.append(new)
    return x, [jnp.stack([n[i] for n in news]) for i in range(len(news[0]))]


def kernel(x_prompt, x_sample, cache_k_a, cache_v_a, cache_idx_k, cache_k_b, cache_v_b, cache_k_c,
           cache_v_c, state_ffn_conv, w_in, w_out, rel_bias, lambda_q1, lambda_k1, lambda_q2,
           lambda_k2, subln_gain, g_pre_mix, g_post_mix, g_pre_ffn, g_post_ffn, w_up, conv_w,
           conv_b, w_down):
    prm = dict(w_in=w_in, w_out=w_out, rel_bias=rel_bias, lambda_q1=lambda_q1, lambda_k1=lambda_k1,
               lambda_q2=lambda_q2, lambda_k2=lambda_k2, subln_gain=subln_gain, g_pre_mix=g_pre_mix,
               g_post_mix=g_post_mix, g_pre_ffn=g_pre_ffn, g_post_ffn=g_post_ffn, w_up=w_up,
               conv_w=conv_w, conv_b=conv_b, w_down=w_down)
    y_prompt, p_new = _run_trunk(x_prompt, None, prm)
    caches = dict(k_a=cache_k_a, v_a=cache_v_a, k_i=cache_idx_k, k_b=cache_k_b, v_b=cache_v_b,
                  k_c=cache_k_c, v_c=cache_v_c, conv=state_ffn_conv)
    y_sample, s_new = _run_trunk(x_sample, caches, prm)
    return (y_prompt, y_sample, *p_new, *s_new)
```

```python
import functools
import math

import numpy as np
import jax
import jax.numpy as jnp
from jax import lax
from jax.experimental import pallas as pl
from jax.experimental.pallas import tpu as pltpu

F32 = jnp.float32
BF16 = jnp.bfloat16
I32 = jnp.int32

LANES = 128
VMEM_LIMIT = 56 * 1024 * 1024

CHUNK = 64
HEAD_DIM = 64
N_HEADS_A = 6
N_HEADS_B = 6
N_HEADS_C = 4
DC_HALF = HEAD_DIM // 2
IDX_HEADS = 8
IDX_DIM = 32
TOPK_MAX = 256
NUM_BUCKETS = 32
MAX_DISTANCE = 128
CONV_W = 3
EPS = 1e-6

WA = N_HEADS_A * HEAD_DIM
WB = N_HEADS_B * HEAD_DIM
WC = N_HEADS_C * HEAD_DIM
WQI = IDX_HEADS * IDX_DIM

NEG = -1e30
INT_MIN = -2 ** 31
KEY_NEG_INF = int(np.int32(np.float32(-np.inf).view(np.int32)) ^ np.int32(0x7FFFFFFF))
EXP_ZERO = -104.0
LOG2E = math.log2(math.e)

KB = 128
TAIL = 2 * KB
SUP = 4 * KB
assert TAIL - KB >= MAX_DISTANCE
NT_DIMS = (((1,), (1,)), ((), ()))


def _cparams(sem):
    return pltpu.CompilerParams(dimension_semantics=sem, vmem_limit_bytes=VMEM_LIMIT)


def _dot_nt(a, b):
    return lax.dot_general(a, b, NT_DIMS, preferred_element_type=F32)


def _dot(a, b):
    return jnp.dot(a, b, preferred_element_type=F32)


_PROJ_GROUPS = (
    ("qa", 0, WA, HEAD_DIM ** -0.5, False, True),
    ("ka", 384, WA, 1.0, True, True),
    ("va", 768, WA, 1.0, True, True),
    ("qi", 1152, WQI, 1.0, True, False),
    ("misc", 1408, LANES, 1.0, True, True),
    ("qb", 1536, WB, HEAD_DIM ** -0.5, False, True),
    ("kb", 1920, WB, 1.0, True, True),
    ("vb", 2304, WB, 1.0, True, True),
    ("qc", 2688, WC, 1.0, False, True),
    ("kc", 2944, WC, 1.0, True, True),
    ("vc", 3200, WC, 1.0, True, True),
)
_PROJ_NPAD = 3456


def _proj_out_names():
    names = []
    for name, _, _, _, f32o, bf16o in _PROJ_GROUPS:
        if f32o:
            names.append(name + "_f32")
        if bf16o:
            names.append(name + "_bf16")
    return names


def _proj_kernel(x_ref, g_ref, w_ref, *out_refs):
    x = x_ref[...]
    ms = jnp.mean(x * x, axis=-1, keepdims=True)
    h = (x * lax.rsqrt(ms + EPS) * g_ref[...]).astype(BF16)
    k = 0
    for _, off, width, scale, f32o, bf16o in _PROJ_GROUPS:
        r = _dot(h, w_ref[:, off:off + width])
        if f32o:
            out_refs[k][...] = r
            k += 1
        if bf16o:
            out_refs[k][...] = (r * scale).astype(BF16) if scale != 1.0 else r.astype(BF16)
            k += 1


def _pad_w_in(w):
    d = w.shape[0]
    misc = jnp.concatenate([w[:, 1416:1448], w[:, 1408:1416],
                            jnp.zeros((d, LANES - IDX_DIM - IDX_HEADS), w.dtype)], axis=1)
    return jnp.concatenate([w[:, 0:1408], misc, w[:, 1448:]], axis=1).astype(BF16)


def _proj(x2d, g, w_pad):
    m, d = x2d.shape
    tm = min(512, m)
    assert m % tm == 0
    out_shapes, out_specs = [], []
    for _, _, width, _, f32o, bf16o in _PROJ_GROUPS:
        for want, dt in ((f32o, F32), (bf16o, BF16)):
            if want:
                out_shapes.append(jax.ShapeDtypeStruct((m, width), dt))
                out_specs.append(pl.BlockSpec((tm, width), lambda i: (i, 0)))
    outs = pl.pallas_call(
        _proj_kernel,
        grid=(m // tm,),
        in_specs=[pl.BlockSpec((tm, d), lambda i: (i, 0)),
                  pl.BlockSpec((1, d), lambda i: (0, 0)),
                  pl.BlockSpec((d, _PROJ_NPAD), lambda i: (0, 0))],
        out_specs=out_specs,
        out_shape=out_shapes,
        compiler_params=_cparams(("parallel",)),
        name="proj",
    )(x2d, g.reshape(1, d), w_pad)
    return dict(zip(_proj_out_names(), outs))


def _t5_bucket(rel):
    half = NUM_BUCKETS // 2
    max_exact = half // 2
    ret = jnp.where(rel > 0, half, 0)
    n = jnp.abs(rel)
    nf = jnp.maximum(n, 1).astype(F32)
    large = max_exact + (jnp.log(nf / max_exact) / math.log(MAX_DISTANCE / max_exact)
                         * (half - max_exact)).astype(I32)
    large = jnp.minimum(large, half - 1)
    return ret + jnp.where(n < max_exact, n, large)


def _bias_tiles(tab, qb, n_valid):
    qq = jnp.arange(qb, dtype=I32)[:, None]
    kk = jnp.arange(TAIL, dtype=I32)[None, :] - (TAIL - KB)
    tile = tab[_t5_bucket(kk - qq)].astype(F32).transpose(2, 0, 1) * LOG2E
    vis = ((kk // CHUNK) <= (qq // CHUNK)) & (kk < n_valid)
    tile = jnp.where(vis[None], tile, NEG)
    tile1 = jnp.concatenate([tile[..., TAIL - KB:], jnp.full(tile.shape[:2] + (TAIL - KB,), NEG, F32)],
                            axis=-1)
    far = tab[_t5_bucket(jnp.array(-MAX_DISTANCE, I32))].astype(F32) * LOG2E
    return jnp.stack([tile, tile1]), far


def _front_mask_rows():
    col = jnp.arange(SUP, dtype=I32)[None, None, :]
    v = jnp.arange(SUP // KB, dtype=I32)[:, None, None]
    return jnp.where(col >= SUP - v * KB, NEG, 0.0).astype(F32)


def _windows(qbi):
    tail_start = jnp.maximum((qbi + 1) * KB - TAIL, 0)
    n_far = (tail_start + SUP - 1) // SUP
    n_front = (n_far * SUP - tail_start) // KB
    return tail_start, n_far, n_front


def _far_start(tail_start, j):
    return jnp.maximum(tail_start - SUP * (j + 1), 0)


def _lane_chunks(x):
    return [x[:, c * LANES:(c + 1) * LANES] for c in range(x.shape[1] // LANES)]


def _attn_step(x, ccol, v, m_ref, l_ref, acc_ref):
    chunks = _lane_chunks(x)
    mx = functools.reduce(jnp.maximum, chunks)
    m_old = m_ref[...]
    m_new = jnp.maximum(m_old, jnp.max(mx, axis=1, keepdims=True) + ccol)
    p = jnp.exp2(x + (ccol - m_new))
    alpha = jnp.exp2(m_old - m_new)
    l_ref[...] = alpha * l_ref[...] + functools.reduce(jnp.add, _lane_chunks(p))
    acc_ref[...] = alpha * acc_ref[...] + _dot(p.astype(BF16), v)
    m_ref[...] = m_new


FOLD_CHAINS = 4
DIFF_GROUPS = 1


def _fold_rows(op, x):
    w, r = x.shape
    x3 = x.reshape(w // 8, 8, r)
    chains = [x3[i] for i in range(FOLD_CHAINS)]
    for i in range(FOLD_CHAINS, w // 8):
        chains[i % FOLD_CHAINS] = op(chains[i % FOLD_CHAINS], x3[i])
    return functools.reduce(op, chains)


def _attn_step_t(x, crow, vt, m_ref, l_ref, acc_ref):
    m_old = m_ref[...]
    m_new = jnp.maximum(m_old, jnp.max(_fold_rows(jnp.maximum, x), axis=0, keepdims=True) + crow)
    p = jnp.exp2(x + (crow - m_new))
    alpha = jnp.exp2(m_old - m_new)
    l_ref[...] = alpha * l_ref[...] + _fold_rows(jnp.add, p)
    acc_ref[...] = alpha * acc_ref[...] + _dot(vt, p.astype(BF16))
    m_ref[...] = m_new


def _front_mask_t(n_front, shape):
    return jnp.where(lax.broadcasted_iota(I32, shape, 0) >= SUP - n_front * KB, NEG, 0.0)


def _blocked_t(v):
    b, s, w = v.shape
    return v.reshape(b, s // KB, KB, w).transpose(0, 1, 3, 2)


def _window_t(k_ref, vt_ref, start, nblk):
    off = pl.multiple_of(start, KB)
    k = k_ref[0, pl.ds(off, nblk * KB), :]
    v4 = vt_ref[0, pl.ds((start) // KB, nblk)]
    vt = jnp.concatenate([v4[n] for n in range(nblk)], axis=1)
    return k, vt


def _key_spec(s_tot, width, index_map):
    return pl.BlockSpec((1, s_tot, width), index_map, pipeline_mode=pl.Buffered(1))


def _diff_kernel(lam_ref, gain_ref, btail_ref, cfar_ref, fmask_ref, q_ref, kp_ref, vp_ref,
                 kt_ref, vt_ref, o_ref, m_ref, l_ref, acc_ref, *, qb, qb0, lam0):
    qbi = qb0 + pl.program_id(2)
    tail_start, n_far, n_front = _windows(qbi)
    q = q_ref[0]
    lane = lax.broadcasted_iota(I32, (qb, LANES), 1)
    qs = jnp.concatenate(
        [jnp.where((lane >= DC_HALF * c) & (lane < DC_HALF * (c + 1)), q, jnp.zeros_like(q))
         for c in range(4)], axis=0)
    m_ref[...] = jnp.full(m_ref.shape, NEG, F32)
    l_ref[...] = jnp.zeros(l_ref.shape, F32)
    acc_ref[...] = jnp.zeros(acc_ref.shape, F32)
    a = DC_HALF ** -0.5 * LOG2E

    def window(start, width):
        off = pl.multiple_of(start, KB)
        return (kp_ref[0, pl.ds(off, width), :].astype(BF16),
                vp_ref[0, pl.ds(off, width), :].astype(BF16))

    k, v = kt_ref[0], vt_ref[0]
    bt = btail_ref[jnp.where(qbi == 0, 1, 0)]
    x = (_dot_nt(qs, k) * a).reshape(2, 2, qb, TAIL) + bt[:, None]
    _attn_step(x.reshape(4 * qb, TAIL), 0.0, v, m_ref, l_ref, acc_ref)

    ccol = cfar_ref[0]

    def far_body(j, carry):
        k, v = window(_far_start(tail_start, j), SUP)
        _attn_step(_dot_nt(qs, k) * a, ccol, v, m_ref, l_ref, acc_ref)
        return carry

    lax.fori_loop(0, n_far - 1, far_body, 0)

    @pl.when(n_far >= 1)
    def _():
        k, v = window(_far_start(tail_start, n_far - 1), SUP)
        _attn_step(_dot_nt(qs, k) * a + fmask_ref[n_front], ccol, v, m_ref, l_ref, acc_ref)

    lam_v = lam_ref[...]
    s1 = jnp.sum(lam_v[0:1] * lam_v[1:2], axis=1, keepdims=True)
    s2 = jnp.sum(lam_v[2:3] * lam_v[3:4], axis=1, keepdims=True)
    lam = jnp.exp(s1) - jnp.exp(s2) + lam0
    o4 = acc_ref[...] / jnp.sum(l_ref[...], axis=1, keepdims=True)
    o0 = o4[0:qb] - lam * o4[qb:2 * qb]
    o1 = o4[2 * qb:3 * qb] - lam * o4[3 * qb:4 * qb]
    first = lane < HEAD_DIM
    o = jnp.where(first, o0, o1)
    sq = o * o
    ms0 = jnp.sum(jnp.where(first, sq, 0.0), axis=1, keepdims=True)
    ms1 = jnp.sum(jnp.where(first, 0.0, sq), axis=1, keepdims=True)
    ms = jnp.where(first, ms0, ms1) * (1.0 / HEAD_DIM)
    y = o * lax.rsqrt(ms + EPS) * gain_ref[...]
    o_ref[0] = (y * (1.0 - lam0)).astype(o_ref.dtype)


def _diff_attn(q, k, v, btail, cfar, lamv, gain2, *, qb, qb0, lam0):
    (kp, kt), (vp, vt) = k, v
    b, t, _ = q.shape
    p_len = kp.shape[1]
    nq = t // qb
    assert nq == 1
    npair = N_HEADS_C // 2
    past_spec = pl.BlockSpec((1, p_len, LANES), lambda bi, p, i: (bi, 0, p), pipeline_mode=pl.Buffered(1))
    tail_spec = pl.BlockSpec((1, TAIL, LANES), lambda bi, p, i: (bi, 0, p))
    ccol = jnp.repeat(cfar.reshape(npair, 2), 2 * qb, axis=1).reshape(npair, 4 * qb, 1)
    btail = btail.reshape(2, npair, 2, qb, TAIL)
    kern = functools.partial(_diff_kernel, qb=qb, qb0=qb0, lam0=lam0)
    return pl.pallas_call(
        kern,
        grid=(b, npair, nq),
        in_specs=[
            pl.BlockSpec((4, LANES), lambda bi, p, i: (0, 0)),
            pl.BlockSpec((1, LANES), lambda bi, p, i: (0, 0)),
            pl.BlockSpec((2, None, 2, qb, TAIL), lambda bi, p, i: (0, p, 0, 0, 0)),
            pl.BlockSpec((1, 4 * qb, 1), lambda bi, p, i: (p, 0, 0)),
            pl.BlockSpec((SUP // KB, 1, SUP), lambda bi, p, i: (0, 0, 0)),
            pl.BlockSpec((1, qb, LANES), lambda bi, p, i: (bi, i, p)),
            past_spec, past_spec, tail_spec, tail_spec,
        ],
        out_specs=pl.BlockSpec((1, qb, LANES), lambda bi, p, i: (bi, i, p)),
        out_shape=jax.ShapeDtypeStruct((b, t, WC), BF16),
        scratch_shapes=[pltpu.VMEM((4 * qb, 1), F32), pltpu.VMEM((4 * qb, LANES), F32),
                        pltpu.VMEM((4 * qb, LANES), F32)],
        compiler_params=_cparams(("parallel", "parallel", "arbitrary")),
        name="diff_attn",
    )(lamv, gain2, btail, ccol, _front_mask_rows(), q, kp, vp, kt, vt)


def _diff_kernel_t(lam_ref, gain_ref, btail_ref, crow_ref, qt_ref, k_ref, vt_ref, o_ref,
                   m_ref, l_ref, acc_ref, *, qb, qb0, lam0):
    qbi = qb0 + pl.program_id(2)
    tail_start, n_far, n_front = _windows(qbi)
    qt = qt_ref[0]
    row = lax.broadcasted_iota(I32, (LANES, qb), 0)
    qst = jnp.concatenate(
        [jnp.where((row >= DC_HALF * c) & (row < DC_HALF * (c + 1)), qt, jnp.zeros_like(qt))
         for c in range(4)], axis=1)
    m_ref[...] = jnp.full(m_ref.shape, NEG, F32)
    l_ref[...] = jnp.zeros(l_ref.shape, F32)
    acc_ref[...] = jnp.zeros(acc_ref.shape, F32)
    a = DC_HALF ** -0.5 * LOG2E

    gw = 4 * qb // DIFF_GROUPS

    def steps(starts, nblk, add, far):
        wins = [_window_t(k_ref, vt_ref, start, nblk) for start in starts]
        raws = [_dot(k, qst) for k, _ in wins]
        for (_, vt), raw in zip(wins, raws):
            for g in range(DIFF_GROUPS):
                cols = slice(g * gw, (g + 1) * gw)
                x = raw[:, cols] * a
                if add is not None:
                    x = x + add(cols)
                _attn_step_t(x, crow_ref[0, :, cols] if far else 0.0, vt,
                             m_ref.at[g], l_ref.at[g], acc_ref.at[g])

    variant = jnp.where(qbi == 0, 1, 0)
    steps([tail_start], TAIL // KB, lambda cols: btail_ref[variant, :, cols], False)

    far_start = lambda j: _far_start(tail_start, j)
    n_full = n_far - 1

    def far_body(jj, carry):
        steps([far_start(2 * jj), far_start(2 * jj + 1)], SUP // KB, None, True)
        return carry

    lax.fori_loop(0, n_full // 2, far_body, 0)

    @pl.when(jnp.logical_and(n_full > 0, n_full % 2 == 1))
    def _():
        steps([far_start(n_full - 1)], SUP // KB, None, True)

    @pl.when(n_far >= 1)
    def _():
        fm = _front_mask_t(n_front, (SUP, gw))
        steps([far_start(n_far - 1)], SUP // KB, lambda cols: fm, True)

    acc = jnp.concatenate([acc_ref[g] for g in range(DIFF_GROUPS)], axis=1)
    l_sum = jnp.concatenate([jnp.sum(l_ref[g], axis=0, keepdims=True) for g in range(DIFF_GROUPS)], axis=1)

    lam_v = lam_ref[...]
    s1 = jnp.sum(lam_v[0:1] * lam_v[1:2], axis=1, keepdims=True)
    s2 = jnp.sum(lam_v[2:3] * lam_v[3:4], axis=1, keepdims=True)
    lam = jnp.exp(s1) - jnp.exp(s2) + lam0
    o4 = acc / l_sum
    o0 = o4[:, 0:qb] - lam * o4[:, qb:2 * qb]
    o1 = o4[:, 2 * qb:3 * qb] - lam * o4[:, 3 * qb:4 * qb]
    first = row < HEAD_DIM
    o = jnp.where(first, o0, o1)
    sq = o * o
    ms0 = jnp.sum(sq[0:HEAD_DIM], axis=0, keepdims=True)
    ms1 = jnp.sum(sq[HEAD_DIM:], axis=0, keepdims=True)
    ms = jnp.where(first, ms0, ms1) * (1.0 / HEAD_DIM)
    y = o * lax.rsqrt(ms + EPS) * gain_ref[...]
    o_ref[0] = (y * (1.0 - lam0)).astype(o_ref.dtype)


def _diff_attn_t(q, k, v, btail, cfar, lamv, gain2, *, qb, qb0, lam0):
    b, t, _ = q.shape
    s_tot = k.shape[1]
    nq = t // qb
    npair = N_HEADS_C // 2
    qt = q.transpose(0, 2, 1)
    vt = _blocked_t(v)
    crow = jnp.repeat(cfar.reshape(npair, 2), 2 * qb, axis=1).reshape(npair, 1, 4 * qb)
    bt = btail.reshape(2, npair, 2, qb, TAIL).transpose(0, 1, 4, 2, 3)
    bt = jnp.repeat(bt[:, :, :, :, None, :], 2, axis=4).reshape(2, npair, TAIL, 4 * qb)
    gain_t = jnp.broadcast_to(gain2.reshape(LANES, 1), (LANES, qb))
    kern = functools.partial(_diff_kernel_t, qb=qb, qb0=qb0, lam0=lam0)
    out_t = pl.pallas_call(
        kern,
        grid=(b, npair, nq),
        in_specs=[
            pl.BlockSpec((4, LANES), lambda bi, p, i: (0, 0)),
            pl.BlockSpec((LANES, qb), lambda bi, p, i: (0, 0)),
            pl.BlockSpec((2, None, TAIL, 4 * qb), lambda bi, p, i: (0, p, 0, 0)),
            pl.BlockSpec((1, 1, 4 * qb), lambda bi, p, i: (p, 0, 0)),
            pl.BlockSpec((1, LANES, qb), lambda bi, p, i: (bi, p, i)),
            _key_spec(s_tot, LANES, lambda bi, p, i: (bi, 0, p)),
            pl.BlockSpec((1, s_tot // KB, LANES, KB), lambda bi, p, i: (bi, 0, p, 0),
                         pipeline_mode=pl.Buffered(1)),
        ],
        out_specs=pl.BlockSpec((1, LANES, qb), lambda bi, p, i: (bi, p, i)),
        out_shape=jax.ShapeDtypeStruct((b, WC, t), BF16),
        scratch_shapes=[pltpu.VMEM((DIFF_GROUPS, 1, 4 * qb // DIFF_GROUPS), F32),
                        pltpu.VMEM((DIFF_GROUPS, 8, 4 * qb // DIFF_GROUPS), F32),
                        pltpu.VMEM((DIFF_GROUPS, LANES, 4 * qb // DIFF_GROUPS), F32)],
        compiler_params=_cparams(("parallel", "parallel", "arbitrary")),
        name="diff_attn_t",
    )(lamv, gain_t, bt, crow, qt, k, vt)
    return out_t.transpose(0, 2, 1)


def _sb_kernel(q_ref, k_ref, v_ref, *rest, qb, qb0, has_past):
    if has_past:
        kn_ref, vn_ref, o_ref, carry_ref, acc_ref = rest
    else:
        o_ref, carry_ref, acc_ref = rest
    npair = N_HEADS_B // 2
    qbi = qb0 + pl.program_id(1)
    lane = lax.broadcasted_iota(I32, (qb, LANES), 1)
    rows = lax.broadcasted_iota(I32, (2 * qb, KB), 0)
    causal = lax.broadcasted_iota(I32, (2 * qb, KB), 1) < jnp.where(rows >= qb, rows - qb, rows)
    tri = jnp.where(lax.broadcasted_iota(I32, (KB, KB), 0) > lax.broadcasted_iota(I32, (KB, KB), 1),
                    1.0, 0.0).astype(BF16)
    q = q_ref[0]
    qs = []
    for p in range(npair):
        part = q[:, p * LANES:(p + 1) * LANES]
        qs.append(jnp.concatenate([jnp.where(lane < HEAD_DIM, part, jnp.zeros_like(part)),
                                   jnp.where(lane < HEAD_DIM, jnp.zeros_like(part), part)], axis=0))
    carry_ref[...] = jnp.zeros(carry_ref.shape, F32)
    acc_ref[...] = jnp.zeros(acc_ref.shape, F32)

    def block(j, masked):
        if masked and has_past:
            k, v = kn_ref[0], vn_ref[0]
        else:
            off = pl.multiple_of(j * KB, KB)
            k = k_ref[0, pl.ds(off, KB), :].astype(BF16)
            v = v_ref[0, pl.ds(off, KB), :].astype(BF16)
        pairs = range(npair)
        zs = [_dot_nt(qs[p], k[:, p * LANES:(p + 1) * LANES]) for p in pairs]
        sps = [jnp.maximum(z, 0.0) + jnp.log1p(jnp.exp(-jnp.abs(z))) for z in zs]
        lgs = [-sp for sp in sps]
        if masked:
            lgs = [jnp.where(causal, lg, 0.0) for lg in lgs]
        tails = []
        for lg in lgs:
            l1 = lg.astype(BF16)
            r1 = lg - l1.astype(F32)
            l2 = r1.astype(BF16)
            l3 = (r1 - l2.astype(F32)).astype(BF16)
            tails.append(_dot(l1, tri) + _dot(l2, tri) + _dot(l3, tri))
        carries = [carry_ref[p] for p in pairs]
        ws = [jnp.exp(zs[p] - sps[p] + tails[p] + carries[p]) for p in pairs]
        if masked:
            ws = [jnp.where(causal, w, 0.0) for w in ws]
        pvs = [_dot(ws[p].astype(BF16), v[:, p * LANES:(p + 1) * LANES]) for p in pairs]
        worst = None
        for p in pairs:
            acc_ref[p] += pvs[p]
            carry_new = carries[p] + jnp.sum(lgs[p], axis=1, keepdims=True)
            carry_ref[p] = carry_new
            worst = carry_new if worst is None else jnp.maximum(worst, carry_new)
        return jnp.max(worst)

    mx0 = block(qbi, True)

    def cond(st):
        return jnp.logical_and(st[0] >= 0, st[1] > EXP_ZERO)

    def body(st):
        return st[0] - 1, block(st[0], False)

    lax.while_loop(cond, body, (qbi - 1, mx0))
    for p in range(npair):
        acc = acc_ref[p]
        o_ref[0, :, p * LANES:(p + 1) * LANES] = jnp.where(
            lane < HEAD_DIM, acc[0:qb], acc[qb:2 * qb]).astype(o_ref.dtype)


def _sb_attn(q, k, v, new=None, *, qb, qb0):
    b, t, _ = q.shape
    s_pad = k.shape[1]
    nq = t // qb
    npair = N_HEADS_B // 2
    kern = functools.partial(_sb_kernel, qb=qb, qb0=qb0, has_past=new is not None)
    new_specs = [] if new is None else [pl.BlockSpec((1, KB, WB), lambda bi, i: (bi, 0, 0))] * 2
    return pl.pallas_call(
        kern,
        grid=(b, nq),
        in_specs=[
            pl.BlockSpec((1, qb, WB), lambda bi, i: (bi, i, 0)),
            _key_spec(s_pad, WB, lambda bi, i: (bi, 0, 0)),
            _key_spec(s_pad, WB, lambda bi, i: (bi, 0, 0)),
        ] + new_specs,
        out_specs=pl.BlockSpec((1, qb, WB), lambda bi, i: (bi, i, 0)),
        out_shape=jax.ShapeDtypeStruct((b, t, WB), BF16),
        scratch_shapes=[pltpu.VMEM((npair, 2 * qb, 1), F32), pltpu.VMEM((npair, 2 * qb, LANES), F32)],
        compiler_params=_cparams(("parallel", "arbitrary")),
        name="sb_attn",
    )(q, k, v, *(() if new is None else new))


def _dsa_kernel(btail_ref, cfar_ref, fmask_ref, qa_ref, qi_ref, wi_ref, kap_ref, vap_ref, kip_ref,
                kat_ref, vat_ref, kit_ref, o_ref, kxt_ref, kxf_ref, wb_ref, r_ref, cnt_ref, cs_ref,
                m_ref, l_ref, acc_ref, *, qb, qb0, topk, idx_bits):
    npair = N_HEADS_A // 2
    qbi = qb0 + pl.program_id(1)
    tail_start, n_far, n_front = _windows(qbi)
    lane = lax.broadcasted_iota(I32, (qb, LANES), 1)

    def past(ref, start, width):
        off = pl.multiple_of(start, KB)
        return ref[0, pl.ds(off, width), :].astype(BF16)

    qi = qi_ref[0]
    parts = []
    for h in range(IDX_HEADS):
        g, sh = divmod(h * IDX_DIM, LANES)
        part = qi[:, g * LANES:(g + 1) * LANES]
        if sh:
            part = pltpu.roll(part, LANES - sh, 1)
        parts.append(jnp.where(lane < IDX_DIM, part, 0.0).astype(BF16))
    qis = jnp.concatenate(parts, axis=0)
    wi = wi_ref[0] * (IDX_HEADS ** -0.5 * IDX_DIM ** -0.5)
    for h in range(IDX_HEADS):
        wb_ref[h * qb:(h + 1) * qb, :] = jnp.broadcast_to(wi[:, IDX_DIM + h:IDX_DIM + h + 1],
                                                           (qb, LANES))
    qa = qa_ref[0]
    qas = []
    for p in range(npair):
        part = qa[:, p * LANES:(p + 1) * LANES]
        qas.append(jnp.concatenate(
            [jnp.where(lane < HEAD_DIM, part, jnp.zeros_like(part)),
             jnp.where(lane < HEAD_DIM, jnp.zeros_like(part), part)], axis=0))

    qis_past = qis[:, 0:IDX_DIM]

    def keys_of(start, width, hidden):
        if width == TAIL:
            d = _dot_nt(qis, kit_ref[0])
        else:
            d = _dot_nt(qis_past, past(kip_ref, start, width))
        wb = wb_ref[...]
        cols = []
        for dc in _lane_chunks(d):
            r = jnp.maximum(dc, 0.0) * wb
            cols.append(functools.reduce(jnp.add, [r[h * qb:(h + 1) * qb] for h in range(IDX_HEADS)]))
        sc = jnp.concatenate(cols, axis=1)
        sc = jnp.where(sc == 0.0, 0.0, sc)
        bits = lax.bitcast_convert_type(sc, I32)
        key = bits ^ ((bits >> 31) & 0x7FFFFFFF)
        return jnp.where(hidden, KEY_NEG_INF, key)

    bt0 = btail_ref[jnp.where(qbi == 0, 1, 0), 0]
    kxt_ref[...] = keys_of(tail_start, TAIL, bt0 < 0.5 * NEG)

    def score_body(j, carry):
        kxf_ref[j] = keys_of(_far_start(tail_start, j), SUP, False)
        return carry

    lax.fori_loop(0, n_far - 1, score_body, 0)

    @pl.when(n_far >= 1)
    def _():
        kxf_ref[n_far - 1] = keys_of(_far_start(tail_start, n_far - 1), SUP, fmask_ref[n_front] < 0.5 * NEG)

    col_t = lax.broadcasted_iota(I32, (qb, TAIL), 1)
    col_f = lax.broadcasted_iota(I32, (qb, SUP), 1)

    def count(pred_t, pred_f):
        def body(j, acc):
            hit = jnp.where(pred_f(kxf_ref[j], col_f + (_far_start(tail_start, j))), 1.0, 0.0)
            return acc + functools.reduce(jnp.add, _lane_chunks(hit))
        acc = functools.reduce(jnp.add, _lane_chunks(
            jnp.where(pred_t(kxt_ref[...], col_t + tail_start), 1.0, 0.0)))
        acc = lax.fori_loop(0, n_far, body, acc)
        return jnp.sum(acc, axis=1, keepdims=True)

    def count1(make_pred):
        return count(make_pred(TAIL), make_pred(SUP))

    wide = lambda a, w: jnp.broadcast_to(a, (qb, w))

    def ge(a):
        return lambda w: (lambda kx, idx, aw=wide(a, w): kx >= aw)

    def gt(a):
        return lambda w: (lambda kx, idx, aw=wide(a, w): kx > aw)

    def eq(a):
        return lambda w: (lambda kx, idx, aw=wide(a, w): kx == aw)

    def eq_before(a, c):
        return lambda w: (lambda kx, idx, aw=wide(a, w), cw=wide(c, w): (kx == aw) & (idx < cw))

    r_ref[...] = jnp.full((qb, 1), INT_MIN, I32)
    cnt_ref[...] = jnp.full((qb, 1), float(TAIL + SUP * 64), F32)

    def bit_cond(st):
        return jnp.logical_and(st[0] < 32, st[1] > 0.0)

    def bit_body(st):
        t = st[0]
        r = r_ref[...]
        cand = r + lax.shift_left(jnp.int32(1), 31 - t)
        cnt = count1(ge(cand))
        take = cnt >= topk
        r_ref[...] = jnp.where(take, cand, r)
        cnt_new = jnp.where(take, cnt, cnt_ref[...])
        cnt_ref[...] = cnt_new
        return t + 1, jnp.max(jnp.where(cnt_new == topk, 0.0, 1.0))

    lax.while_loop(bit_cond, bit_body, (jnp.int32(0), jnp.float32(1.0)))
    r1 = r_ref[...]

    need = topk - count1(gt(r1))
    n_eq = count1(eq(r1))
    cs_ref[...] = jnp.full((qb, 1), 2 ** idx_bits - 1, I32)
    overflow = jnp.where((n_eq > need) & (r1 > KEY_NEG_INF), 1.0, 0.0)

    @pl.when(jnp.max(overflow) > 0.0)
    def _():
        def tie_body(t, c):
            cand = c + lax.shift_left(jnp.int32(1), idx_bits - 1 - t)
            cnt = count1(eq_before(r1, cand))
            return jnp.where(cnt < need, cand, c)
        cs_ref[...] = lax.fori_loop(0, idx_bits, tie_body, jnp.zeros((qb, 1), I32))

    cs1 = cs_ref[...]

    m_ref[...] = jnp.full(m_ref.shape, NEG, F32)
    l_ref[...] = jnp.zeros(l_ref.shape, F32)
    acc_ref[...] = jnp.zeros(acc_ref.shape, F32)

    def sel_mask(kx, idx, w):
        sel = (kx > wide(r1, w)) | ((kx == wide(r1, w)) & (idx <= wide(cs1, w)))
        return jnp.where(sel & (kx > KEY_NEG_INF), 0.0, NEG)

    def attend(start, width, madd, bias, ccols):
        if width == TAIL:
            k, v = kat_ref[0], vat_ref[0]
        else:
            k, v = past(kap_ref, start, width), past(vap_ref, start, width)
        for p in range(npair):
            x = (_dot_nt(qas[p], k[:, p * LANES:(p + 1) * LANES]) * LOG2E).reshape(2, qb, width) + madd[None]
            if bias is not None:
                x = x + bias[2 * p:2 * p + 2]
            _attn_step(x.reshape(2 * qb, width), ccols[p], v[:, p * LANES:(p + 1) * LANES],
                       m_ref.at[p], l_ref.at[p], acc_ref.at[p])

    bt = btail_ref[jnp.where(qbi == 0, 1, 0)]
    attend(tail_start, TAIL, sel_mask(kxt_ref[...], col_t + tail_start, TAIL), bt, [0.0] * npair)

    ccols = [cfar_ref[p] for p in range(npair)]

    def far_body(j, carry):
        start = _far_start(tail_start, j)
        attend(start, SUP, sel_mask(kxf_ref[j], col_f + start, SUP), None, ccols)
        return carry

    lax.fori_loop(0, n_far, far_body, 0)

    for p in range(npair):
        o2 = acc_ref[p] / jnp.sum(l_ref[p], axis=1, keepdims=True)
        o = jnp.where(lane < HEAD_DIM, o2[0:qb], o2[qb:2 * qb])
        o_ref[0, :, p * LANES:(p + 1) * LANES] = o.astype(o_ref.dtype)


def _dsa_attn(qa, qi, misc_f32, ka, va, ki, btail, cfar, *, qb, qb0, topk):
    (kap, kat), (vap, vat), (kip, kit) = ka, va, ki
    b, t, _ = qa.shape
    p_len = kap.shape[1]
    s_tot = p_len + KB
    nq = t // qb
    assert nq == 1
    npair = N_HEADS_A // 2
    n_far_max = max(1, -(-(s_tot - TAIL) // SUP))
    assert n_far_max <= 64
    idx_bits = int(math.ceil(math.log2(s_tot))) + 1
    past_spec = lambda w: pl.BlockSpec((1, p_len, w), lambda bi, i: (bi, 0, 0),
                                       pipeline_mode=pl.Buffered(1))
    tail_spec = lambda w: pl.BlockSpec((1, TAIL, w), lambda bi, i: (bi, 0, 0))
    ccol = jnp.repeat(cfar.reshape(npair, 2), qb, axis=1).reshape(npair, 2 * qb, 1)
    kern = functools.partial(_dsa_kernel, qb=qb, qb0=qb0, topk=float(topk), idx_bits=idx_bits)
    return pl.pallas_call(
        kern,
        grid=(b, nq),
        in_specs=[
            pl.BlockSpec((2, N_HEADS_A, qb, TAIL), lambda bi, i: (0, 0, 0, 0)),
            pl.BlockSpec((npair, 2 * qb, 1), lambda bi, i: (0, 0, 0)),
            pl.BlockSpec((SUP // KB, 1, SUP), lambda bi, i: (0, 0, 0)),
            pl.BlockSpec((1, qb, WA), lambda bi, i: (bi, i, 0)),
            pl.BlockSpec((1, qb, WQI), lambda bi, i: (bi, i, 0)),
            pl.BlockSpec((1, qb, LANES), lambda bi, i: (bi, i, 0)),
            past_spec(WA), past_spec(WA), past_spec(IDX_DIM),
            tail_spec(WA), tail_spec(WA), tail_spec(LANES),
        ],
        out_specs=pl.BlockSpec((1, qb, WA), lambda bi, i: (bi, i, 0)),
        out_shape=jax.ShapeDtypeStruct((b, t, WA), BF16),
        scratch_shapes=[
            pltpu.VMEM((qb, TAIL), I32),
            pltpu.VMEM((n_far_max, qb, SUP), I32),
            pltpu.VMEM((IDX_HEADS * qb, LANES), F32),
            pltpu.VMEM((qb, 1), I32),
            pltpu.VMEM((qb, 1), F32),
            pltpu.VMEM((qb, 1), I32),
            pltpu.VMEM((npair, 2 * qb, 1), F32),
            pltpu.VMEM((npair, 2 * qb, LANES), F32),
            pltpu.VMEM((npair, 2 * qb, LANES), F32),
        ],
        compiler_params=_cparams(("parallel", "arbitrary")),
        name="dsa_attn",
    )(btail, ccol, _front_mask_rows(), qa, qi, misc_f32, kap, vap, kip, kat, vat, kit)


def _dsa_kernel_t(btail_ref, crow_ref, qat_ref, qit_ref, wit_ref, ka_ref, vat_ref, ki_ref, o_ref,
                  kxt_ref, kxf_ref, m_ref, l_ref, acc_ref, *, qb, qb0, topk, idx_bits):
    npair = N_HEADS_A // 2
    qbi = qb0 + pl.program_id(1)
    tail_start, n_far, n_front = _windows(qbi)
    row = lax.broadcasted_iota(I32, (LANES, qb), 0)

    qit = qit_ref[0]
    zpad = jnp.zeros((LANES - IDX_DIM, qb), BF16)
    qist = jnp.concatenate(
        [jnp.concatenate([qit[h * IDX_DIM:(h + 1) * IDX_DIM].astype(BF16), zpad], axis=0)
         for h in range(IDX_HEADS)], axis=1)
    wit = wit_ref[0] * (IDX_HEADS ** -0.5 * IDX_DIM ** -0.5)
    qat = qat_ref[0]
    qast = []
    for p in range(npair):
        part = qat[p * LANES:(p + 1) * LANES]
        qast.append(jnp.concatenate(
            [jnp.where(row < HEAD_DIM, part, jnp.zeros_like(part)),
             jnp.where(row < HEAD_DIM, jnp.zeros_like(part), part)], axis=1))

    def ki_window(start, width):
        return ki_ref[0, pl.ds(pl.multiple_of(start, KB), width), :]

    def keys_of(start, width, hidden):
        d = _dot(ki_window(start, width), qist)
        sc = functools.reduce(jnp.add, [
            jnp.maximum(d[:, h * qb:(h + 1) * qb], 0.0) * wit[h:h + 1] for h in range(IDX_HEADS)])
        sc = jnp.where(sc == 0.0, 0.0, sc)
        bits = lax.bitcast_convert_type(sc, I32)
        key = bits ^ ((bits >> 31) & 0x7FFFFFFF)
        return key if hidden is None else jnp.where(hidden, KEY_NEG_INF, key)

    variant = jnp.where(qbi == 0, 1, 0)
    kxt_ref[...] = keys_of(tail_start, TAIL, btail_ref[variant, 0, :, 0:qb] < 0.5 * NEG)

    def score_body(j, carry):
        kxf_ref[j] = keys_of(_far_start(tail_start, j), SUP, None)
        return carry

    lax.fori_loop(0, n_far - 1, score_body, 0)

    @pl.when(n_far >= 1)
    def _():
        kxf_ref[n_far - 1] = keys_of(_far_start(tail_start, n_far - 1), SUP,
                                     lax.broadcasted_iota(I32, (SUP, qb), 0) >= SUP - n_front * KB)

    row_t = lax.broadcasted_iota(I32, (TAIL, qb), 0)
    row_f = lax.broadcasted_iota(I32, (SUP, qb), 0)

    def hits(pred, kx, idx):
        return _fold_rows(jnp.add, jnp.where(pred(kx, idx), 1.0, 0.0))

    def count(pred):
        def body(j, acc):
            return acc + hits(pred, kxf_ref[j], row_f + (_far_start(tail_start, j)))
        acc = lax.fori_loop(0, n_far, body, hits(pred, kxt_ref[...], row_t + tail_start))
        return jnp.sum(acc, axis=0, keepdims=True)

    def bit_cond(st):
        return jnp.logical_and(st[0] < 32, st[3] > 0.0)

    def bit_body(st):
        t, r, cnt_r, _ = st
        cand = r + lax.shift_left(jnp.int32(1), 31 - t)
        cnt = count(lambda kx, idx: kx >= cand)
        take = cnt >= topk
        cnt_new = jnp.where(take, cnt, cnt_r)
        return t + 1, jnp.where(take, cand, r), cnt_new, jnp.max(jnp.where(cnt_new == topk, 0.0, 1.0))

    _, r1, _, _ = lax.while_loop(
        bit_cond, bit_body,
        (jnp.int32(0), jnp.full((1, qb), INT_MIN, I32), jnp.full((1, qb), float(2 ** 30), F32),
         jnp.float32(1.0)))

    need = topk - count(lambda kx, idx: kx > r1)
    n_eq = count(lambda kx, idx: kx == r1)
    overflow = jnp.where((n_eq > need) & (r1 > KEY_NEG_INF), 1.0, 0.0)

    def tie_search(_):
        def tie_body(t, c):
            cand = c + lax.shift_left(jnp.int32(1), idx_bits - 1 - t)
            cnt = count(lambda kx, idx: (kx == r1) & (idx < cand))
            return jnp.where(cnt < need, cand, c)
        return lax.fori_loop(0, idx_bits, tie_body, jnp.zeros((1, qb), I32))

    cs1 = lax.cond(jnp.max(overflow) > 0.0, tie_search,
                   lambda _: jnp.full((1, qb), 2 ** idx_bits - 1, I32), 0)

    m_ref[...] = jnp.full(m_ref.shape, NEG, F32)
    l_ref[...] = jnp.zeros(l_ref.shape, F32)
    acc_ref[...] = jnp.zeros(acc_ref.shape, F32)

    def sel_mask(kx, idx):
        sel = (kx > r1) | ((kx == r1) & (idx <= cs1))
        madd = jnp.where(sel & (kx > KEY_NEG_INF), 0.0, NEG)
        return jnp.concatenate([madd, madd], axis=1)

    def attend(starts, nblk, madds, bias, crows):
        wins = [_window_t(ka_ref, vat_ref, start, nblk) for start in starts]
        raws = [[_dot(k[:, p * LANES:(p + 1) * LANES], qast[p]) for p in range(npair)] for k, _ in wins]
        for (_, vt), raw, madd2 in zip(wins, raws, madds):
            for p in range(npair):
                x = raw[p] * LOG2E + madd2
                if bias is not None:
                    x = x + bias[:, 2 * p * qb:2 * (p + 1) * qb]
                _attn_step_t(x, crows[p], vt[p * LANES:(p + 1) * LANES],
                             m_ref.at[p], l_ref.at[p], acc_ref.at[p])

    attend([tail_start], TAIL // KB, [sel_mask(kxt_ref[...], row_t + tail_start)],
           btail_ref[variant, 0], [0.0] * npair)

    crows = [crow_ref[p] for p in range(npair)]
    far_start = lambda j: _far_start(tail_start, j)

    def attend_far(js):
        attend([far_start(j) for j in js], SUP // KB,
               [sel_mask(kxf_ref[j], row_f + far_start(j)) for j in js], None, crows)

    def far_body(jj, carry):
        attend_far([2 * jj, 2 * jj + 1])
        return carry

    lax.fori_loop(0, n_far // 2, far_body, 0)

    @pl.when(n_far % 2 == 1)
    def _():
        attend_far([n_far - 1])

    for p in range(npair):
        o2 = acc_ref[p] / jnp.sum(l_ref[p], axis=0, keepdims=True)
        o = jnp.where(row < HEAD_DIM, o2[:, 0:qb], o2[:, qb:2 * qb])
        o_ref[0, p * LANES:(p + 1) * LANES, :] = o.astype(o_ref.dtype)


def _dsa_attn_t(qa, qi, misc_f32, ka, va, ki, btail, cfar, *, qb, qb0, topk):
    b, t, _ = qa.shape
    s_tot = ka.shape[1]
    nq = t // qb
    npair = N_HEADS_A // 2
    n_far_max = max(1, -(-(s_tot - TAIL) // SUP))
    idx_bits = int(math.ceil(math.log2(s_tot))) + 1
    qat = qa.transpose(0, 2, 1)
    qit = qi.transpose(0, 2, 1)
    wit = misc_f32[:, :, IDX_DIM:IDX_DIM + IDX_HEADS].transpose(0, 2, 1)
    vat = _blocked_t(va)
    crow = jnp.repeat(cfar.reshape(npair, 2), qb, axis=1).reshape(npair, 1, 2 * qb)
    bt = btail.transpose(0, 3, 1, 2).reshape(2, 1, TAIL, N_HEADS_A * qb)
    kern = functools.partial(_dsa_kernel_t, qb=qb, qb0=qb0, topk=float(topk), idx_bits=idx_bits)
    out_t = pl.pallas_call(
        kern,
        grid=(b, nq),
        in_specs=[
            pl.BlockSpec((2, 1, TAIL, N_HEADS_A * qb), lambda bi, i: (0, 0, 0, 0)),
            pl.BlockSpec((npair, 1, 2 * qb), lambda bi, i: (0, 0, 0)),
            pl.BlockSpec((1, WA, qb), lambda bi, i: (bi, 0, i)),
            pl.BlockSpec((1, WQI, qb), lambda bi, i: (bi, 0, i)),
            pl.BlockSpec((1, IDX_HEADS, qb), lambda bi, i: (bi, 0, i)),
            _key_spec(s_tot, WA, lambda bi, i: (bi, 0, 0)),
            pl.BlockSpec((1, s_tot // KB, WA, KB), lambda bi, i: (bi, 0, 0, 0),
                         pipeline_mode=pl.Buffered(1)),
            _key_spec(s_tot, LANES, lambda bi, i: (bi, 0, 0)),
        ],
        out_specs=pl.BlockSpec((1, WA, qb), lambda bi, i: (bi, 0, i)),
        out_shape=jax.ShapeDtypeStruct((b, WA, t), BF16),
        scratch_shapes=[
            pltpu.VMEM((TAIL, qb), I32),
            pltpu.VMEM((n_far_max, SUP, qb), I32),
            pltpu.VMEM((npair, 1, 2 * qb), F32),
            pltpu.VMEM((npair, 8, 2 * qb), F32),
            pltpu.VMEM((npair, LANES, 2 * qb), F32),
        ],
        compiler_params=_cparams(("parallel", "arbitrary")),
        name="dsa_attn_t",
    )(bt, crow, qat, qit, wit, ka, vat, ki)
    return out_t.transpose(0, 2, 1)


def _outproj_kernel(oa_ref, ob_ref, oc_ref, wa_ref, wb_ref, wc_ref, g_ref, x_ref, y_ref):
    mix = _dot(oa_ref[...], wa_ref[...]) + _dot(ob_ref[...], wb_ref[...]) + _dot(oc_ref[...], wc_ref[...])
    ms = jnp.mean(mix * mix, axis=-1, keepdims=True)
    y_ref[...] = x_ref[...] + mix * lax.rsqrt(ms + EPS) * g_ref[...]


def _outproj(oa, ob, oc, w_out, g, x2d):
    m, d = x2d.shape
    tm = min(512, m)
    w = w_out.astype(BF16)
    row = lambda i: (i, 0)
    const = lambda i: (0, 0)
    return pl.pallas_call(
        _outproj_kernel,
        grid=(m // tm,),
        in_specs=[pl.BlockSpec((tm, WA), row), pl.BlockSpec((tm, WB), row), pl.BlockSpec((tm, WC), row),
                  pl.BlockSpec((WA, d), const), pl.BlockSpec((WB, d), const), pl.BlockSpec((WC, d), const),
                  pl.BlockSpec((1, d), const), pl.BlockSpec((tm, d), row)],
        out_specs=pl.BlockSpec((tm, d), row),
        out_shape=jax.ShapeDtypeStruct((m, d), F32),
        compiler_params=_cparams(("parallel",)),
        name="outproj",
    )(oa, ob, oc, w[:WA], w[WA:WA + WB], w[WA + WB:], g.reshape(1, d), x2d)


HALO = 16


def _ffn_kernel(x_ref, xh_ref, stg_ref, stv_ref, gpre_ref, gpost_ref, wg_ref, wv_ref, cwg_ref,
                cwv_ref, cbg_ref, cbv_ref, wd_ref, y_ref, convg_ref, convv_ref,
                h_ref, ug_ref, uv_ref, f_ref, *, tm, tiles_per_seq):
    i = pl.program_id(0)
    f = pl.program_id(1)
    first_of_seq = (i % tiles_per_seq) == 0

    def norm(x):
        ms = jnp.mean(x * x, axis=-1, keepdims=True)
        return (x * lax.rsqrt(ms + EPS) * gpre_ref[...]).astype(BF16)

    @pl.when(f == 0)
    def _():
        h_ref[0:HALO, :] = norm(xh_ref[...])
        h_ref[HALO:, :] = norm(x_ref[...])
        f_ref[...] = jnp.zeros(f_ref.shape, F32)

    h = h_ref[...]
    ug_ref[...] = _dot(h, wg_ref[...])
    uv_ref[...] = _dot(h, wv_ref[...])

    @pl.when(first_of_seq)
    def _():
        ug_ref[0:HALO, :] = stg_ref[0]
        uv_ref[0:HALO, :] = stv_ref[0]

    def conv(u_ref, cw_ref, cb_ref):
        u = u_ref[...]
        u1 = pltpu.roll(u, 1, 0)[HALO:, :]
        u2 = pltpu.roll(u, 2, 0)[HALO:, :]
        cw = cw_ref[...]
        return cb_ref[...] + (cw[0:1] * u2 + cw[1:2] * u1 + cw[2:3] * u[HALO:, :])

    gate = conv(ug_ref, cwg_ref, cbg_ref)
    val = conv(uv_ref, cwv_ref, cbv_ref)
    c0 = math.sqrt(2.0 / math.pi)
    gelu = 0.5 * gate * (1.0 + jnp.tanh(c0 * (gate + 0.044715 * (gate * gate * gate))))
    f_ref[...] += _dot((gelu * val).astype(BF16), wd_ref[...])

    convg_ref[0] = ug_ref[tm:, :]
    convv_ref[0] = uv_ref[tm:, :]

    @pl.when(f == pl.num_programs(1) - 1)
    def _():
        ff = f_ref[...]
        ms = jnp.mean(ff * ff, axis=-1, keepdims=True)
        y_ref[...] = x_ref[...] + ff * lax.rsqrt(ms + EPS) * gpost_ref[...]


def _ffn(x2d, state, g_pre, g_post, w_up, conv_w, conv_b, w_down, *, seq):
    m, d = x2d.shape
    b = m // seq
    d_ff = w_down.shape[0]
    tn = 1408
    assert d_ff % tn == 0 and tn % LANES == 0
    nf = d_ff // tn
    tm = min(512, seq)
    assert seq % tm == 0 and tm % HALO == 0
    tiles_per_seq = seq // tm
    n_tiles = m // tm
    st = jnp.pad(state.astype(F32), ((0, 0), (HALO - (CONV_W - 1), 0), (0, 0)))
    wu = w_up.astype(BF16)
    wd = w_down.astype(BF16)
    kern = functools.partial(_ffn_kernel, tm=tm, tiles_per_seq=tiles_per_seq)
    halo_blocks = tm // HALO
    y, conv_g, conv_v = pl.pallas_call(
        kern,
        grid=(n_tiles, nf),
        in_specs=[
            pl.BlockSpec((tm, d), lambda i, f: (i, 0)),
            pl.BlockSpec((HALO, d), lambda i, f: (jnp.maximum(i * halo_blocks - 1, 0), 0)),
            pl.BlockSpec((1, HALO, tn), lambda i, f: (i // tiles_per_seq, 0, f)),
            pl.BlockSpec((1, HALO, tn), lambda i, f: (i // tiles_per_seq, 0, nf + f)),
            pl.BlockSpec((1, d), lambda i, f: (0, 0)),
            pl.BlockSpec((1, d), lambda i, f: (0, 0)),
            pl.BlockSpec((d, tn), lambda i, f: (0, f)),
            pl.BlockSpec((d, tn), lambda i, f: (0, nf + f)),
            pl.BlockSpec((CONV_W, tn), lambda i, f: (0, f)),
            pl.BlockSpec((CONV_W, tn), lambda i, f: (0, nf + f)),
            pl.BlockSpec((1, tn), lambda i, f: (0, f)),
            pl.BlockSpec((1, tn), lambda i, f: (0, nf + f)),
            pl.BlockSpec((tn, d), lambda i, f: (f, 0)),
        ],
        out_specs=[pl.BlockSpec((tm, d), lambda i, f: (i, 0)),
                   pl.BlockSpec((1, HALO, tn), lambda i, f: (i, 0, f)),
                   pl.BlockSpec((1, HALO, tn), lambda i, f: (i, 0, f))],
        out_shape=[jax.ShapeDtypeStruct((m, d), F32),
                   jax.ShapeDtypeStruct((n_tiles, HALO, d_ff), F32),
                   jax.ShapeDtypeStruct((n_tiles, HALO, d_ff), F32)],
        scratch_shapes=[pltpu.VMEM((HALO + tm, d), BF16),
                        pltpu.VMEM((HALO + tm, tn), F32),
                        pltpu.VMEM((HALO + tm, tn), F32),
                        pltpu.VMEM((tm, d), F32)],
        compiler_params=_cparams(("parallel", "arbitrary")),
        name="ffn",
    )(x2d, x2d, st, st, g_pre.reshape(1, d), g_post.reshape(1, d), wu, wu,
      conv_w, conv_w, conv_b.reshape(1, -1), conv_b.reshape(1, -1), wd)
    last = lambda c: c.reshape(b, tiles_per_seq, HALO, d_ff)[:, -1, HALO - (CONV_W - 1):]
    return y, jnp.concatenate([last(conv_g), last(conv_v)], axis=-1)


def _lambda_init(l):
    return 0.8 - 0.6 * math.exp(-0.3 * l)


def _past_and_tail(past, new):
    b, t, w = new.shape
    p = past.shape[1]
    past = past.reshape(b, p, -1)
    last = past[:, p - (TAIL - KB):].astype(BF16)
    if last.shape[2] < w:
        last = jnp.pad(last, ((0, 0), (0, 0), (0, w - last.shape[2])))
    return past, jnp.concatenate([last, new, jnp.zeros((b, KB - t, w), BF16)], axis=1)


def _layer(x, past, l, prm):
    b, t, d = x.shape
    p_len = 0 if past is None else past["k_a"].shape[1]
    s = p_len + t
    qb = min(KB, t)
    assert p_len % KB == 0 and t % qb == 0 and (qb == KB or t == qb)
    qb0 = p_len // KB
    s_pad = -(-s // KB) * KB
    assert s_pad >= max(TAIL, SUP)
    n_valid = s - (s_pad - KB)
    topk = min(TOPK_MAX, s // 4)
    m = b * t

    pr = _proj(x.reshape(m, d), prm["g_pre_mix"][l], _pad_w_in(prm["w_in"][l]))
    r3 = lambda a: a.reshape(b, t, a.shape[-1])

    btail_a, cfar_a = _bias_tiles(prm["rel_bias"][:, :N_HEADS_A], qb, n_valid)
    btail_c, cfar_c = _bias_tiles(prm["rel_bias"][:, N_HEADS_A:], qb, n_valid)

    lamv = jnp.pad(jnp.stack([prm["lambda_q1"][l], prm["lambda_k1"][l],
                              prm["lambda_q2"][l], prm["lambda_k2"][l]]).astype(F32),
                   ((0, 0), (0, LANES - DC_HALF)))
    gain2 = jnp.tile(prm["subln_gain"][l].astype(F32), 2).reshape(1, LANES)
    if past is None:
        assert qb == KB
        o_a = _dsa_attn_t(r3(pr["qa_bf16"]), r3(pr["qi_f32"]), r3(pr["misc_f32"]),
                          r3(pr["ka_bf16"]), r3(pr["va_bf16"]), r3(pr["misc_bf16"]),
                          btail_a, cfar_a, qb=qb, qb0=qb0, topk=topk)
        o_b = _sb_attn(r3(pr["qb_bf16"]), r3(pr["kb_bf16"]), r3(pr["vb_bf16"]), qb=qb, qb0=qb0)
        o_c = _diff_attn_t(r3(pr["qc_bf16"]), r3(pr["kc_bf16"]), r3(pr["vc_bf16"]),
                           btail_c, cfar_c, lamv, gain2, qb=qb, qb0=qb0, lam0=_lambda_init(l))
    else:
        assert t == qb and p_len >= max(TAIL - KB, SUP)
        o_a = _dsa_attn(r3(pr["qa_bf16"]), r3(pr["qi_f32"]), r3(pr["misc_f32"]),
                        _past_and_tail(past["k_a"], r3(pr["ka_bf16"])),
                        _past_and_tail(past["v_a"], r3(pr["va_bf16"])),
                        _past_and_tail(past["k_i"], r3(pr["misc_bf16"])),
                        btail_a, cfar_a, qb=qb, qb0=qb0, topk=topk)
        diag = lambda new: jnp.concatenate([new, jnp.zeros((b, KB - t, new.shape[2]), BF16)], axis=1)
        o_b = _sb_attn(r3(pr["qb_bf16"]), past["k_b"].reshape(b, p_len, WB),
                       past["v_b"].reshape(b, p_len, WB),
                       (diag(r3(pr["kb_bf16"])), diag(r3(pr["vb_bf16"]))), qb=qb, qb0=qb0)
        o_c = _diff_attn(r3(pr["qc_bf16"]),
                         _past_and_tail(past["k_c"], r3(pr["kc_bf16"])),
                         _past_and_tail(past["v_c"], r3(pr["vc_bf16"])),
                         btail_c, cfar_c, lamv, gain2, qb=qb, qb0=qb0, lam0=_lambda_init(l))

    x1 = _outproj(o_a.reshape(m, WA), o_b.reshape(m, WB), o_c.reshape(m, WC),
                  prm["w_out"][l], prm["g_post_mix"][l], x.reshape(m, d))
    state = (jnp.zeros((b, CONV_W - 1, prm["w_up"].shape[2]), F32) if past is None
             else past["conv"])
    x2, new_conv = _ffn(x1, state, prm["g_pre_ffn"][l], prm["g_post_ffn"][l], prm["w_up"][l],
                        prm["conv_w"][l], prm["conv_b"][l], prm["w_down"][l], seq=t)
    new = (pr["ka_f32"].reshape(b, t, N_HEADS_A, HEAD_DIM),
           pr["va_f32"].reshape(b, t, N_HEADS_A, HEAD_DIM),
           pr["misc_f32"][:, :IDX_DIM].reshape(b, t, IDX_DIM),
           pr["kb_f32"].reshape(b, t, N_HEADS_B, HEAD_DIM),
           pr["vb_f32"].reshape(b, t, N_HEADS_B, HEAD_DIM),
           pr["kc_f32"].reshape(b, t, N_HEADS_C, HEAD_DIM),
           pr["vc_f32"].reshape(b, t, N_HEADS_C, HEAD_DIM),
           new_conv)
    return x2.reshape(b, t, d), new


def _run_trunk(x, caches, prm):
    depth = prm["w_in"].shape[0]
    news = []
    for l in range(depth):
        past = None if caches is None else {k: v[l] for k, v in caches.items()}
        x, new = _layer(x, past, l, prm)
        news.append(new)
    return x, [jnp.stack([n[i] for n in news]) for i in range(len(news[0]))]


def kernel(x_prompt, x_sample, cache_k_a, cache_v_a, cache_idx_k, cache_k_b, cache_v_b, cache_k_c,
           cache_v_c, state_ffn_conv, w_in, w_out, rel_bias, lambda_q1, lambda_k1, lambda_q2,
           lambda_k2, subln_gain, g_pre_mix, g_post_mix, g_pre_ffn, g_post_ffn, w_up, conv_w,
           conv_b, w_down):
    prm = dict(w_in=w_in, w_out=w_out, rel_bias=rel_bias, lambda_q1=lambda_q1, lambda_k1=lambda_k1,
               lambda_q2=lambda_q2, lambda_k2=lambda_k2, subln_gain=subln_gain, g_pre_mix=g_pre_mix,
               g_post_mix=g_post_mix, g_pre_ffn=g_pre_ffn, g_post_ffn=g_post_ffn, w_up=w_up,
               conv_w=conv_w, conv_b=conv_b, w_down=w_down)
    y_prompt, p_new = _run_trunk(x_prompt, None, prm)
    caches = dict(k_a=cache_k_a, v_a=cache_v_a, k_i=cache_idx_k, k_b=cache_k_b, v_b=cache_v_b,
                  k_c=cache_k_c, v_c=cache_v_c, conv=state_ffn_conv)
    y_sample, s_new = _run_trunk(x_sample, caches, prm)
    return (y_prompt, y_sample, *p_new, *s_new)
```

```python
import functools
import math

import numpy as np
import jax
import jax.numpy as jnp
from jax import lax
from jax.experimental import pallas as pl
from jax.experimental.pallas import tpu as pltpu

F32 = jnp.float32
BF16 = jnp.bfloat16
I32 = jnp.int32

LANES = 128
VMEM_LIMIT = 56 * 1024 * 1024

CHUNK = 64
HEAD_DIM = 64
N_HEADS_A = 6
N_HEADS_B = 6
N_HEADS_C = 4
DC_HALF = HEAD_DIM // 2
IDX_HEADS = 8
IDX_DIM = 32
TOPK_MAX = 256
NUM_BUCKETS = 32
MAX_DISTANCE = 128
CONV_W = 3
EPS = 1e-6

WA = N_HEADS_A * HEAD_DIM
WB = N_HEADS_B * HEAD_DIM
WC = N_HEADS_C * HEAD_DIM
WQI = IDX_HEADS * IDX_DIM

NEG = -1e30
INT_MIN = -2 ** 31
KEY_NEG_INF = int(np.int32(np.float32(-np.inf).view(np.int32)) ^ np.int32(0x7FFFFFFF))
EXP_ZERO = -104.0
LOG2E = math.log2(math.e)

KB = 128
TAIL = 2 * KB
SUP = 4 * KB
assert TAIL - KB >= MAX_DISTANCE
NT_DIMS = (((1,), (1,)), ((), ()))


def _cparams(sem):
    return pltpu.CompilerParams(dimension_semantics=sem, vmem_limit_bytes=VMEM_LIMIT)


def _dot_nt(a, b):
    return lax.dot_general(a, b, NT_DIMS, preferred_element_type=F32)


def _dot(a, b):
    return jnp.dot(a, b, preferred_element_type=F32)


_PROJ_GROUPS = (
    ("qa", 0, WA, HEAD_DIM ** -0.5, False, True),
    ("ka", 384, WA, 1.0, True, True),
    ("va", 768, WA, 1.0, True, True),
    ("qi", 1152, WQI, 1.0, True, False),
    ("misc", 1408, LANES, 1.0, True, True),
    ("qb", 1536, WB, HEAD_DIM ** -0.5, False, True),
    ("kb", 1920, WB, 1.0, True, True),
    ("vb", 2304, WB, 1.0, True, True),
    ("qc", 2688, WC, 1.0, False, True),
    ("kc", 2944, WC, 1.0, True, True),
    ("vc", 3200, WC, 1.0, True, True),
)
_PROJ_NPAD = 3456


def _proj_out_names():
    names = []
    for name, _, _, _, f32o, bf16o in _PROJ_GROUPS:
        if f32o:
            names.append(name + "_f32")
        if bf16o:
            names.append(name + "_bf16")
    return names


def _proj_kernel(x_ref, g_ref, w_ref, *out_refs):
    x = x_ref[...]
    ms = jnp.mean(x * x, axis=-1, keepdims=True)
    h = (x * lax.rsqrt(ms + EPS) * g_ref[...]).astype(BF16)
    k = 0
    for _, off, width, scale, f32o, bf16o in _PROJ_GROUPS:
        r = _dot(h, w_ref[:, off:off + width])
        if f32o:
            out_refs[k][...] = r
            k += 1
        if bf16o:
            out_refs[k][...] = (r * scale).astype(BF16) if scale != 1.0 else r.astype(BF16)
            k += 1


def _pad_w_in(w):
    d = w.shape[0]
    misc = jnp.concatenate([w[:, 1416:1448], w[:, 1408:1416],
                            jnp.zeros((d, LANES - IDX_DIM - IDX_HEADS), w.dtype)], axis=1)
    return jnp.concatenate([w[:, 0:1408], misc, w[:, 1448:]], axis=1).astype(BF16)


def _proj(x2d, g, w_pad):
    m, d = x2d.shape
    tm = min(512, m)
    assert m % tm == 0
    out_shapes, out_specs = [], []
    for _, _, width, _, f32o, bf16o in _PROJ_GROUPS:
        for want, dt in ((f32o, F32), (bf16o, BF16)):
            if want:
                out_shapes.append(jax.ShapeDtypeStruct((m, width), dt))
                out_specs.append(pl.BlockSpec((tm, width), lambda i: (i, 0)))
    outs = pl.pallas_call(
        _proj_kernel,
        grid=(m // tm,),
        in_specs=[pl.BlockSpec((tm, d), lambda i: (i, 0)),
                  pl.BlockSpec((1, d), lambda i: (0, 0)),
                  pl.BlockSpec((d, _PROJ_NPAD), lambda i: (0, 0))],
        out_specs=out_specs,
        out_shape=out_shapes,
        compiler_params=_cparams(("parallel",)),
        name="proj",
    )(x2d, g.reshape(1, d), w_pad)
    return dict(zip(_proj_out_names(), outs))


def _t5_bucket(rel):
    half = NUM_BUCKETS // 2
    max_exact = half // 2
    ret = jnp.where(rel > 0, half, 0)
    n = jnp.abs(rel)
    nf = jnp.maximum(n, 1).astype(F32)
    large = max_exact + (jnp.log(nf / max_exact) / math.log(MAX_DISTANCE / max_exact)
                         * (half - max_exact)).astype(I32)
    large = jnp.minimum(large, half - 1)
    return ret + jnp.where(n < max_exact, n, large)


def _bias_tiles(tab, qb, n_valid):
    qq = jnp.arange(qb, dtype=I32)[:, None]
    kk = jnp.arange(TAIL, dtype=I32)[None, :] - (TAIL - KB)
    tile = tab[_t5_bucket(kk - qq)].astype(F32).transpose(2, 0, 1) * LOG2E
    vis = ((kk // CHUNK) <= (qq // CHUNK)) & (kk < n_valid)
    tile = jnp.where(vis[None], tile, NEG)
    tile1 = jnp.concatenate([tile[..., TAIL - KB:], jnp.full(tile.shape[:2] + (TAIL - KB,), NEG, F32)],
                            axis=-1)
    far = tab[_t5_bucket(jnp.array(-MAX_DISTANCE, I32))].astype(F32) * LOG2E
    return jnp.stack([tile, tile1]), far


def _front_mask_rows():
    col = jnp.arange(SUP, dtype=I32)[None, None, :]
    v = jnp.arange(SUP // KB, dtype=I32)[:, None, None]
    return jnp.where(col >= SUP - v * KB, NEG, 0.0).astype(F32)


def _windows(qbi):
    tail_start = jnp.maximum((qbi + 1) * KB - TAIL, 0)
    n_far = (tail_start + SUP - 1) // SUP
    n_front = (n_far * SUP - tail_start) // KB
    return tail_start, n_far, n_front


def _far_start(tail_start, j):
    return jnp.maximum(tail_start - SUP * (j + 1), 0)


def _lane_chunks(x):
    return [x[:, c * LANES:(c + 1) * LANES] for c in range(x.shape[1] // LANES)]


def _attn_step(x, ccol, v, m_ref, l_ref, acc_ref):
    chunks = _lane_chunks(x)
    mx = functools.reduce(jnp.maximum, chunks)
    m_old = m_ref[...]
    m_new = jnp.maximum(m_old, jnp.max(mx, axis=1, keepdims=True) + ccol)
    p = jnp.exp2(x + (ccol - m_new))
    alpha = jnp.exp2(m_old - m_new)
    l_ref[...] = alpha * l_ref[...] + functools.reduce(jnp.add, _lane_chunks(p))
    acc_ref[...] = alpha * acc_ref[...] + _dot(p.astype(BF16), v)
    m_ref[...] = m_new


FOLD_CHAINS = 4
DIFF_GROUPS = 1


def _fold_rows(op, x):
    w, r = x.shape
    x3 = x.reshape(w // 8, 8, r)
    chains = [x3[i] for i in range(FOLD_CHAINS)]
    for i in range(FOLD_CHAINS, w // 8):
        chains[i % FOLD_CHAINS] = op(chains[i % FOLD_CHAINS], x3[i])
    return functools.reduce(op, chains)


def _attn_step_t(x, crow, vt, m_ref, l_ref, acc_ref):
    m_old = m_ref[...]
    m_new = jnp.maximum(m_old, jnp.max(_fold_rows(jnp.maximum, x), axis=0, keepdims=True) + crow)
    p = jnp.exp2(x + (crow - m_new))
    alpha = jnp.exp2(m_old - m_new)
    l_ref[...] = alpha * l_ref[...] + _fold_rows(jnp.add, p)
    acc_ref[...] = alpha * acc_ref[...] + _dot(vt, p.astype(BF16))
    m_ref[...] = m_new


def _front_mask_t(n_front, shape):
    return jnp.where(lax.broadcasted_iota(I32, shape, 0) >= SUP - n_front * KB, NEG, 0.0)


def _blocked_t(v):
    b, s, w = v.shape
    return v.reshape(b, s // KB, KB, w).transpose(0, 1, 3, 2)


def _window_t(k_ref, vt_ref, start, nblk):
    off = pl.multiple_of(start, KB)
    k = k_ref[0, pl.ds(off, nblk * KB), :]
    v4 = vt_ref[0, pl.ds((start) // KB, nblk)]
    vt = jnp.concatenate([v4[n] for n in range(nblk)], axis=1)
    return k, vt


def _key_spec(s_tot, width, index_map):
    return pl.BlockSpec((1, s_tot, width), index_map, pipeline_mode=pl.Buffered(1))


def _diff_kernel(lam_ref, gain_ref, btail_ref, cfar_ref, fmask_ref, q_ref, kp_ref, vp_ref,
                 kt_ref, vt_ref, o_ref, m_ref, l_ref, acc_ref, *, qb, qb0, lam0):
    qbi = qb0 + pl.program_id(2)
    tail_start, n_far, n_front = _windows(qbi)
    q = q_ref[0]
    lane = lax.broadcasted_iota(I32, (qb, LANES), 1)
    qs = jnp.concatenate(
        [jnp.where((lane >= DC_HALF * c) & (lane < DC_HALF * (c + 1)), q, jnp.zeros_like(q))
         for c in range(4)], axis=0)
    m_ref[...] = jnp.full(m_ref.shape, NEG, F32)
    l_ref[...] = jnp.zeros(l_ref.shape, F32)
    acc_ref[...] = jnp.zeros(acc_ref.shape, F32)
    a = DC_HALF ** -0.5 * LOG2E

    def window(start, width):
        off = pl.multiple_of(start, KB)
        return (kp_ref[0, pl.ds(off, width), :].astype(BF16),
                vp_ref[0, pl.ds(off, width), :].astype(BF16))

    k, v = kt_ref[0], vt_ref[0]
    bt = btail_ref[jnp.where(qbi == 0, 1, 0)]
    x = (_dot_nt(qs, k) * a).reshape(2, 2, qb, TAIL) + bt[:, None]
    _attn_step(x.reshape(4 * qb, TAIL), 0.0, v, m_ref, l_ref, acc_ref)

    ccol = cfar_ref[0]

    def far_body(j, carry):
        k, v = window(_far_start(tail_start, j), SUP)
        _attn_step(_dot_nt(qs, k) * a, ccol, v, m_ref, l_ref, acc_ref)
        return carry

    lax.fori_loop(0, n_far - 1, far_body, 0)

    @pl.when(n_far >= 1)
    def _():
        k, v = window(_far_start(tail_start, n_far - 1), SUP)
        _attn_step(_dot_nt(qs, k) * a + fmask_ref[n_front], ccol, v, m_ref, l_ref, acc_ref)

    lam_v = lam_ref[...]
    s1 = jnp.sum(lam_v[0:1] * lam_v[1:2], axis=1, keepdims=True)
    s2 = jnp.sum(lam_v[2:3] * lam_v[3:4], axis=1, keepdims=True)
    lam = jnp.exp(s1) - jnp.exp(s2) + lam0
    o4 = acc_ref[...] / jnp.sum(l_ref[...], axis=1, keepdims=True)
    o0 = o4[0:qb] - lam * o4[qb:2 * qb]
    o1 = o4[2 * qb:3 * qb] - lam * o4[3 * qb:4 * qb]
    first = lane < HEAD_DIM
    o = jnp.where(first, o0, o1)
    sq = o * o
    ms0 = jnp.sum(jnp.where(first, sq, 0.0), axis=1, keepdims=True)
    ms1 = jnp.sum(jnp.where(first, 0.0, sq), axis=1, keepdims=True)
    ms = jnp.where(first, ms0, ms1) * (1.0 / HEAD_DIM)
    y = o * lax.rsqrt(ms + EPS) * gain_ref[...]
    o_ref[0] = (y * (1.0 - lam0)).astype(o_ref.dtype)


def _diff_attn(q, k, v, btail, cfar, lamv, gain2, *, qb, qb0, lam0):
    (kp, kt), (vp, vt) = k, v
    b, t, _ = q.shape
    p_len = kp.shape[1]
    nq = t // qb
    assert nq == 1
    npair = N_HEADS_C // 2
    past_spec = pl.BlockSpec((1, p_len, LANES), lambda bi, p, i: (bi, 0, p), pipeline_mode=pl.Buffered(1))
    tail_spec = pl.BlockSpec((1, TAIL, LANES), lambda bi, p, i: (bi, 0, p))
    ccol = jnp.repeat(cfar.reshape(npair, 2), 2 * qb, axis=1).reshape(npair, 4 * qb, 1)
    btail = btail.reshape(2, npair, 2, qb, TAIL)
    kern = functools.partial(_diff_kernel, qb=qb, qb0=qb0, lam0=lam0)
    return pl.pallas_call(
        kern,
        grid=(b, npair, nq),
        in_specs=[
            pl.BlockSpec((4, LANES), lambda bi, p, i: (0, 0)),
            pl.BlockSpec((1, LANES), lambda bi, p, i: (0, 0)),
            pl.BlockSpec((2, None, 2, qb, TAIL), lambda bi, p, i: (0, p, 0, 0, 0)),
            pl.BlockSpec((1, 4 * qb, 1), lambda bi, p, i: (p, 0, 0)),
            pl.BlockSpec((SUP // KB, 1, SUP), lambda bi, p, i: (0, 0, 0)),
            pl.BlockSpec((1, qb, LANES), lambda bi, p, i: (bi, i, p)),
            past_spec, past_spec, tail_spec, tail_spec,
        ],
        out_specs=pl.BlockSpec((1, qb, LANES), lambda bi, p, i: (bi, i, p)),
        out_shape=jax.ShapeDtypeStruct((b, t, WC), BF16),
        scratch_shapes=[pltpu.VMEM((4 * qb, 1), F32), pltpu.VMEM((4 * qb, LANES), F32),
                        pltpu.VMEM((4 * qb, LANES), F32)],
        compiler_params=_cparams(("parallel", "parallel", "arbitrary")),
        name="diff_attn",
    )(lamv, gain2, btail, ccol, _front_mask_rows(), q, kp, vp, kt, vt)


def _diff_kernel_t(lam_ref, gain_ref, btail_ref, crow_ref, qt_ref, k_ref, vt_ref, o_ref,
                   m_ref, l_ref, acc_ref, *, qb, qb0, lam0):
    qbi = qb0 + pl.program_id(2)
    tail_start, n_far, n_front = _windows(qbi)
    qt = qt_ref[0]
    row = lax.broadcasted_iota(I32, (LANES, qb), 0)
    qst = jnp.concatenate(
        [jnp.where((row >= DC_HALF * c) & (row < DC_HALF * (c + 1)), qt, jnp.zeros_like(qt))
         for c in range(4)], axis=1)
    m_ref[...] = jnp.full(m_ref.shape, NEG, F32)
    l_ref[...] = jnp.zeros(l_ref.shape, F32)
    acc_ref[...] = jnp.zeros(acc_ref.shape, F32)
    a = DC_HALF ** -0.5 * LOG2E

    gw = 4 * qb // DIFF_GROUPS

    def steps(starts, nblk, add, far):
        wins = [_window_t(k_ref, vt_ref, start, nblk) for start in starts]
        raws = [_dot(k, qst) for k, _ in wins]
        for (_, vt), raw in zip(wins, raws):
            for g in range(DIFF_GROUPS):
                cols = slice(g * gw, (g + 1) * gw)
                x = raw[:, cols] * a
                if add is not None:
                    x = x + add(cols)
                _attn_step_t(x, crow_ref[0, :, cols] if far else 0.0, vt,
                             m_ref.at[g], l_ref.at[g], acc_ref.at[g])

    variant = jnp.where(qbi == 0, 1, 0)
    steps([tail_start], TAIL // KB, lambda cols: btail_ref[variant, :, cols], False)

    far_start = lambda j: _far_start(tail_start, j)
    n_full = n_far - 1

    def far_body(jj, carry):
        steps([far_start(2 * jj), far_start(2 * jj + 1)], SUP // KB, None, True)
        return carry

    lax.fori_loop(0, n_full // 2, far_body, 0)

    @pl.when(jnp.logical_and(n_full > 0, n_full % 2 == 1))
    def _():
        steps([far_start(n_full - 1)], SUP // KB, None, True)

    @pl.when(n_far >= 1)
    def _():
        fm = _front_mask_t(n_front, (SUP, gw))
        steps([far_start(n_far - 1)], SUP // KB, lambda cols: fm, True)

    acc = jnp.concatenate([acc_ref[g] for g in range(DIFF_GROUPS)], axis=1)
    l_sum = jnp.concatenate([jnp.sum(l_ref[g], axis=0, keepdims=True) for g in range(DIFF_GROUPS)], axis=1)

    lam_v = lam_ref[...]
    s1 = jnp.sum(lam_v[0:1] * lam_v[1:2], axis=1, keepdims=True)
    s2 = jnp.sum(lam_v[2:3] * lam_v[3:4], axis=1, keepdims=True)
    lam = jnp.exp(s1) - jnp.exp(s2) + lam0
    o4 = acc / l_sum
    o0 = o4[:, 0:qb] - lam * o4[:, qb:2 * qb]
    o1 = o4[:, 2 * qb:3 * qb] - lam * o4[:, 3 * qb:4 * qb]
    first = row < HEAD_DIM
    o = jnp.where(first, o0, o1)
    sq = o * o
    ms0 = jnp.sum(sq[0:HEAD_DIM], axis=0, keepdims=True)
    ms1 = jnp.sum(sq[HEAD_DIM:], axis=0, keepdims=True)
    ms = jnp.where(first, ms0, ms1) * (1.0 / HEAD_DIM)
    y = o * lax.rsqrt(ms + EPS) * gain_ref[...]
    o_ref[0] = (y * (1.0 - lam0)).astype(o_ref.dtype)


def _diff_attn_t(q, k, v, btail, cfar, lamv, gain2, *, qb, qb0, lam0):
    b, t, _ = q.shape
    s_tot = k.shape[1]
    nq = t // qb
    npair = N_HEADS_C // 2
    qt = q.transpose(0, 2, 1)
    vt = _blocked_t(v)
    crow = jnp.repeat(cfar.reshape(npair, 2), 2 * qb, axis=1).reshape(npair, 1, 4 * qb)
    bt = btail.reshape(2, npair, 2, qb, TAIL).transpose(0, 1, 4, 2, 3)
    bt = jnp.repeat(bt[:, :, :, :, None, :], 2, axis=4).reshape(2, npair, TAIL, 4 * qb)
    gain_t = jnp.broadcast_to(gain2.reshape(LANES, 1), (LANES, qb))
    kern = functools.partial(_diff_kernel_t, qb=qb, qb0=qb0, lam0=lam0)
    out_t = pl.pallas_call(
        kern,
        grid=(b, npair, nq),
        in_specs=[
            pl.BlockSpec((4, LANES), lambda bi, p, i: (0, 0)),
            pl.BlockSpec((LANES, qb), lambda bi, p, i: (0, 0)),
            pl.BlockSpec((2, None, TAIL, 4 * qb), lambda bi, p, i: (0, p, 0, 0)),
            pl.BlockSpec((1, 1, 4 * qb), lambda bi, p, i: (p, 0, 0)),
            pl.BlockSpec((1, LANES, qb), lambda bi, p, i: (bi, p, i)),
            _key_spec(s_tot, LANES, lambda bi, p, i: (bi, 0, p)),
            pl.BlockSpec((1, s_tot // KB, LANES, KB), lambda bi, p, i: (bi, 0, p, 0),
                         pipeline_mode=pl.Buffered(1)),
        ],
        out_specs=pl.BlockSpec((1, LANES, qb), lambda bi, p, i: (bi, p, i)),
        out_shape=jax.ShapeDtypeStruct((b, WC, t), BF16),
        scratch_shapes=[pltpu.VMEM((DIFF_GROUPS, 1, 4 * qb // DIFF_GROUPS), F32),
                        pltpu.VMEM((DIFF_GROUPS, 8, 4 * qb // DIFF_GROUPS), F32),
                        pltpu.VMEM((DIFF_GROUPS, LANES, 4 * qb // DIFF_GROUPS), F32)],
        compiler_params=_cparams(("parallel", "parallel", "arbitrary")),
        name="diff_attn_t",
    )(lamv, gain_t, bt, crow, qt, k, vt)
    return out_t.transpose(0, 2, 1)


def _sb_kernel(q_ref, k_ref, v_ref, *rest, qb, qb0, has_past):
    if has_past:
        kn_ref, vn_ref, o_ref, carry_ref, acc_ref = rest
    else:
        o_ref, carry_ref, acc_ref = rest
    npair = N_HEADS_B // 2
    qbi = qb0 + pl.program_id(1)
    lane = lax.broadcasted_iota(I32, (qb, LANES), 1)
    rows = lax.broadcasted_iota(I32, (2 * qb, KB), 0)
    causal = lax.broadcasted_iota(I32, (2 * qb, KB), 1) < jnp.where(rows >= qb, rows - qb, rows)
    tri = jnp.where(lax.broadcasted_iota(I32, (KB, KB), 0) > lax.broadcasted_iota(I32, (KB, KB), 1),
                    1.0, 0.0).astype(BF16)
    q = q_ref[0]
    qs = []
    for p in range(npair):
        part = q[:, p * LANES:(p + 1) * LANES]
        qs.append(jnp.concatenate([jnp.where(lane < HEAD_DIM, part, jnp.zeros_like(part)),
                                   jnp.where(lane < HEAD_DIM, jnp.zeros_like(part), part)], axis=0))
    carry_ref[...] = jnp.zeros(carry_ref.shape, F32)
    acc_ref[...] = jnp.zeros(acc_ref.shape, F32)

    def block(j, masked):
        if masked and has_past:
            k, v = kn_ref[0], vn_ref[0]
        else:
            off = pl.multiple_of(j * KB, KB)
            k = k_ref[0, pl.ds(off, KB), :].astype(BF16)
            v = v_ref[0, pl.ds(off, KB), :].astype(BF16)
        pairs = range(npair)
        zs = [_dot_nt(qs[p], k[:, p * LANES:(p + 1) * LANES]) for p in pairs]
        sps = [jnp.maximum(z, 0.0) + jnp.log1p(jnp.exp(-jnp.abs(z))) for z in zs]
        lgs = [-sp for sp in sps]
        if masked:
            lgs = [jnp.where(causal, lg, 0.0) for lg in lgs]
        tails = []
        for lg in lgs:
            l1 = lg.astype(BF16)
            r1 = lg - l1.astype(F32)
            l2 = r1.astype(BF16)
            l3 = (r1 - l2.astype(F32)).astype(BF16)
            tails.append(_dot(l1, tri) + _dot(l2, tri) + _dot(l3, tri))
        carries = [carry_ref[p] for p in pairs]
        ws = [jnp.exp(zs[p] - sps[p] + tails[p] + carries[p]) for p in pairs]
        if masked:
            ws = [jnp.where(causal, w, 0.0) for w in ws]
        pvs = [_dot(ws[p].astype(BF16), v[:, p * LANES:(p + 1) * LANES]) for p in pairs]
        worst = None
        for p in pairs:
            acc_ref[p] += pvs[p]
            carry_new = carries[p] + jnp.sum(lgs[p], axis=1, keepdims=True)
            carry_ref[p] = carry_new
            worst = carry_new if worst is None else jnp.maximum(worst, carry_new)
        return jnp.max(worst)

    mx0 = block(qbi, True)

    def cond(st):
        return jnp.logical_and(st[0] >= 0, st[1] > EXP_ZERO)

    def body(st):
        return st[0] - 1, block(st[0], False)

    lax.while_loop(cond, body, (qbi - 1, mx0))
    for p in range(npair):
        acc = acc_ref[p]
        o_ref[0, :, p * LANES:(p + 1) * LANES] = jnp.where(
            lane < HEAD_DIM, acc[0:qb], acc[qb:2 * qb]).astype(o_ref.dtype)


def _sb_attn(q, k, v, new=None, *, qb, qb0):
    b, t, _ = q.shape
    s_pad = k.shape[1]
    nq = t // qb
    npair = N_HEADS_B // 2
    kern = functools.partial(_sb_kernel, qb=qb, qb0=qb0, has_past=new is not None)
    new_specs = [] if new is None else [pl.BlockSpec((1, KB, WB), lambda bi, i: (bi, 0, 0))] * 2
    return pl.pallas_call(
        kern,
        grid=(b, nq),
        in_specs=[
            pl.BlockSpec((1, qb, WB), lambda bi, i: (bi, i, 0)),
            _key_spec(s_pad, WB, lambda bi, i: (bi, 0, 0)),
            _key_spec(s_pad, WB, lambda bi, i: (bi, 0, 0)),
        ] + new_specs,
        out_specs=pl.BlockSpec((1, qb, WB), lambda bi, i: (bi, i, 0)),
        out_shape=jax.ShapeDtypeStruct((b, t, WB), BF16),
        scratch_shapes=[pltpu.VMEM((npair, 2 * qb, 1), F32), pltpu.VMEM((npair, 2 * qb, LANES), F32)],
        compiler_params=_cparams(("parallel", "arbitrary")),
        name="sb_attn",
    )(q, k, v, *(() if new is None else new))


def _dsa_kernel(btail_ref, cfar_ref, fmask_ref, qa_ref, qi_ref, wi_ref, kap_ref, vap_ref, kip_ref,
                kat_ref, vat_ref, kit_ref, o_ref, kxt_ref, kxf_ref, wb_ref, r_ref, cnt_ref, cs_ref,
                m_ref, l_ref, acc_ref, *, qb, qb0, topk, idx_bits):
    npair = N_HEADS_A // 2
    qbi = qb0 + pl.program_id(1)
    tail_start, n_far, n_front = _windows(qbi)
    lane = lax.broadcasted_iota(I32, (qb, LANES), 1)

    def past(ref, start, width):
        off = pl.multiple_of(start, KB)
        return ref[0, pl.ds(off, width), :].astype(BF16)

    qi = qi_ref[0]
    parts = []
    for h in range(IDX_HEADS):
        g, sh = divmod(h * IDX_DIM, LANES)
        part = qi[:, g * LANES:(g + 1) * LANES]
        if sh:
            part = pltpu.roll(part, LANES - sh, 1)
        parts.append(jnp.where(lane < IDX_DIM, part, 0.0).astype(BF16))
    qis = jnp.concatenate(parts, axis=0)
    wi = wi_ref[0] * (IDX_HEADS ** -0.5 * IDX_DIM ** -0.5)
    for h in range(IDX_HEADS):
        wb_ref[h * qb:(h + 1) * qb, :] = jnp.broadcast_to(wi[:, IDX_DIM + h:IDX_DIM + h + 1],
                                                           (qb, LANES))
    qa = qa_ref[0]
    qas = []
    for p in range(npair):
        part = qa[:, p * LANES:(p + 1) * LANES]
        qas.append(jnp.concatenate(
            [jnp.where(lane < HEAD_DIM, part, jnp.zeros_like(part)),
             jnp.where(lane < HEAD_DIM, jnp.zeros_like(part), part)], axis=0))

    qis_past = qis[:, 0:IDX_DIM]

    def keys_of(start, width, hidden):
        if width == TAIL:
            d = _dot_nt(qis, kit_ref[0])
        else:
            d = _dot_nt(qis_past, past(kip_ref, start, width))
        wb = wb_ref[...]
        cols = []
        for dc in _lane_chunks(d):
            r = jnp.maximum(dc, 0.0) * wb
            cols.append(functools.reduce(jnp.add, [r[h * qb:(h + 1) * qb] for h in range(IDX_HEADS)]))
        sc = jnp.concatenate(cols, axis=1)
        sc = jnp.where(sc == 0.0, 0.0, sc)
        bits = lax.bitcast_convert_type(sc, I32)
        key = bits ^ ((bits >> 31) & 0x7FFFFFFF)
        return jnp.where(hidden, KEY_NEG_INF, key)

    bt0 = btail_ref[jnp.where(qbi == 0, 1, 0), 0]
    kxt_ref[...] = keys_of(tail_start, TAIL, bt0 < 0.5 * NEG)

    def score_body(j, carry):
        kxf_ref[j] = keys_of(_far_start(tail_start, j), SUP, False)
        return carry

    lax.fori_loop(0, n_far - 1, score_body, 0)

    @pl.when(n_far >= 1)
    def _():
        kxf_ref[n_far - 1] = keys_of(_far_start(tail_start, n_far - 1), SUP, fmask_ref[n_front] < 0.5 * NEG)

    col_t = lax.broadcasted_iota(I32, (qb, TAIL), 1)
    col_f = lax.broadcasted_iota(I32, (qb, SUP), 1)

    def count(pred_t, pred_f):
        def body(j, acc):
            hit = jnp.where(pred_f(kxf_ref[j], col_f + (_far_start(tail_start, j))), 1.0, 0.0)
            return acc + functools.reduce(jnp.add, _lane_chunks(hit))
        acc = functools.reduce(jnp.add, _lane_chunks(
            jnp.where(pred_t(kxt_ref[...], col_t + tail_start), 1.0, 0.0)))
        acc = lax.fori_loop(0, n_far, body, acc)
        return jnp.sum(acc, axis=1, keepdims=True)

    def count1(make_pred):
        return count(make_pred(TAIL), make_pred(SUP))

    wide = lambda a, w: jnp.broadcast_to(a, (qb, w))

    def ge(a):
        return lambda w: (lambda kx, idx, aw=wide(a, w): kx >= aw)

    def gt(a):
        return lambda w: (lambda kx, idx, aw=wide(a, w): kx > aw)

    def eq(a):
        return lambda w: (lambda kx, idx, aw=wide(a, w): kx == aw)

    def eq_before(a, c):
        return lambda w: (lambda kx, idx, aw=wide(a, w), cw=wide(c, w): (kx == aw) & (idx < cw))

    r_ref[...] = jnp.full((qb, 1), INT_MIN, I32)
    cnt_ref[...] = jnp.full((qb, 1), float(TAIL + SUP * 64), F32)

    def bit_cond(st):
        return jnp.logical_and(st[0] < 32, st[1] > 0.0)

    def bit_body(st):
        t = st[0]
        r = r_ref[...]
        cand = r + lax.shift_left(jnp.int32(1), 31 - t)
        cnt = count1(ge(cand))
        take = cnt >= topk
        r_ref[...] = jnp.where(take, cand, r)
        cnt_new = jnp.where(take, cnt, cnt_ref[...])
        cnt_ref[...] = cnt_new
        return t + 1, jnp.max(jnp.where(cnt_new == topk, 0.0, 1.0))

    lax.while_loop(bit_cond, bit_body, (jnp.int32(0), jnp.float32(1.0)))
    r1 = r_ref[...]

    need = topk - count1(gt(r1))
    n_eq = count1(eq(r1))
    cs_ref[...] = jnp.full((qb, 1), 2 ** idx_bits - 1, I32)
    overflow = jnp.where((n_eq > need) & (r1 > KEY_NEG_INF), 1.0, 0.0)

    @pl.when(jnp.max(overflow) > 0.0)
    def _():
        def tie_body(t, c):
            cand = c + lax.shift_left(jnp.int32(1), idx_bits - 1 - t)
            cnt = count1(eq_before(r1, cand))
            return jnp.where(cnt < need, cand, c)
        cs_ref[...] = lax.fori_loop(0, idx_bits, tie_body, jnp.zeros((qb, 1), I32))

    cs1 = cs_ref[...]

    m_ref[...] = jnp.full(m_ref.shape, NEG, F32)
    l_ref[...] = jnp.zeros(l_ref.shape, F32)
    acc_ref[...] = jnp.zeros(acc_ref.shape, F32)

    def sel_mask(kx, idx, w):
        sel = (kx > wide(r1, w)) | ((kx == wide(r1, w)) & (idx <= wide(cs1, w)))
        return jnp.where(sel & (kx > KEY_NEG_INF), 0.0, NEG)

    def attend(start, width, madd, bias, ccols):
        if width == TAIL:
            k, v = kat_ref[0], vat_ref[0]
        else:
            k, v = past(kap_ref, start, width), past(vap_ref, start, width)
        for p in range(npair):
            x = (_dot_nt(qas[p], k[:, p * LANES:(p + 1) * LANES]) * LOG2E).reshape(2, qb, width) + madd[None]
            if bias is not None:
                x = x + bias[2 * p:2 * p + 2]
            _attn_step(x.reshape(2 * qb, width), ccols[p], v[:, p * LANES:(p + 1) * LANES],
                       m_ref.at[p], l_ref.at[p], acc_ref.at[p])

    bt = btail_ref[jnp.where(qbi == 0, 1, 0)]
    attend(tail_start, TAIL, sel_mask(kxt_ref[...], col_t + tail_start, TAIL), bt, [0.0] * npair)

    ccols = [cfar_ref[p] for p in range(npair)]

    def far_body(j, carry):
        start = _far_start(tail_start, j)
        attend(start, SUP, sel_mask(kxf_ref[j], col_f + start, SUP), None, ccols)
        return carry

    lax.fori_loop(0, n_far, far_body, 0)

    for p in range(npair):
        o2 = acc_ref[p] / jnp.sum(l_ref[p], axis=1, keepdims=True)
        o = jnp.where(lane < HEAD_DIM, o2[0:qb], o2[qb:2 * qb])
        o_ref[0, :, p * LANES:(p + 1) * LANES] = o.astype(o_ref.dtype)


def _dsa_attn(qa, qi, misc_f32, ka, va, ki, btail, cfar, *, qb, qb0, topk):
    (kap, kat), (vap, vat), (kip, kit) = ka, va, ki
    b, t, _ = qa.shape
    p_len = kap.shape[1]
    s_tot = p_len + KB
    nq = t // qb
    assert nq == 1
    npair = N_HEADS_A // 2
    n_far_max = max(1, -(-(s_tot - TAIL) // SUP))
    assert n_far_max <= 64
    idx_bits = int(math.ceil(math.log2(s_tot))) + 1
    past_spec = lambda w: pl.BlockSpec((1, p_len, w), lambda bi, i: (bi, 0, 0),
                                       pipeline_mode=pl.Buffered(1))
    tail_spec = lambda w: pl.BlockSpec((1, TAIL, w), lambda bi, i: (bi, 0, 0))
    ccol = jnp.repeat(cfar.reshape(npair, 2), qb, axis=1).reshape(npair, 2 * qb, 1)
    kern = functools.partial(_dsa_kernel, qb=qb, qb0=qb0, topk=float(topk), idx_bits=idx_bits)
    return pl.pallas_call(
        kern,
        grid=(b, nq),
        in_specs=[
            pl.BlockSpec((2, N_HEADS_A, qb, TAIL), lambda bi, i: (0, 0, 0, 0)),
            pl.BlockSpec((npair, 2 * qb, 1), lambda bi, i: (0, 0, 0)),
            pl.BlockSpec((SUP // KB, 1, SUP), lambda bi, i: (0, 0, 0)),
            pl.BlockSpec((1, qb, WA), lambda bi, i: (bi, i, 0)),
            pl.BlockSpec((1, qb, WQI), lambda bi, i: (bi, i, 0)),
            pl.BlockSpec((1, qb, LANES), lambda bi, i: (bi, i, 0)),
            past_spec(WA), past_spec(WA), past_spec(IDX_DIM),
            tail_spec(WA), tail_spec(WA), tail_spec(LANES),
        ],
        out_specs=pl.BlockSpec((1, qb, WA), lambda bi, i: (bi, i, 0)),
        out_shape=jax.ShapeDtypeStruct((b, t, WA), BF16),
        scratch_shapes=[
            pltpu.VMEM((qb, TAIL), I32),
            pltpu.VMEM((n_far_max, qb, SUP), I32),
            pltpu.VMEM((IDX_HEADS * qb, LANES), F32),
            pltpu.VMEM((qb, 1), I32),
            pltpu.VMEM((qb, 1), F32),
            pltpu.VMEM((qb, 1), I32),
            pltpu.VMEM((npair, 2 * qb, 1), F32),
            pltpu.VMEM((npair, 2 * qb, LANES), F32),
            pltpu.VMEM((npair, 2 * qb, LANES), F32),
        ],
        compiler_params=_cparams(("parallel", "arbitrary")),
        name="dsa_attn",
    )(btail, ccol, _front_mask_rows(), qa, qi, misc_f32, kap, vap, kip, kat, vat, kit)


def _dsa_kernel_t(btail_ref, crow_ref, qat_ref, qit_ref, wit_ref, ka_ref, vat_ref, ki_ref, o_ref,
                  kxt_ref, kxf_ref, m_ref, l_ref, acc_ref, *, qb, qb0, topk, idx_bits):
    npair = N_HEADS_A // 2
    qbi = qb0 + pl.program_id(1)
    tail_start, n_far, n_front = _windows(qbi)
    row = lax.broadcasted_iota(I32, (LANES, qb), 0)

    qit = qit_ref[0]
    zpad = jnp.zeros((LANES - IDX_DIM, qb), BF16)
    qist = jnp.concatenate(
        [jnp.concatenate([qit[h * IDX_DIM:(h + 1) * IDX_DIM].astype(BF16), zpad], axis=0)
         for h in range(IDX_HEADS)], axis=1)
    wit = wit_ref[0] * (IDX_HEADS ** -0.5 * IDX_DIM ** -0.5)
    qat = qat_ref[0]
    qast = []
    for p in range(npair):
        part = qat[p * LANES:(p + 1) * LANES]
        qast.append(jnp.concatenate(
            [jnp.where(row < HEAD_DIM, part, jnp.zeros_like(part)),
             jnp.where(row < HEAD_DIM, jnp.zeros_like(part), part)], axis=1))

    def ki_window(start, width):
        return ki_ref[0, pl.ds(pl.multiple_of(start, KB), width), :]

    def keys_of(start, width, hidden):
        d = _dot(ki_window(start, width), qist)
        sc = functools.reduce(jnp.add, [
            jnp.maximum(d[:, h * qb:(h + 1) * qb], 0.0) * wit[h:h + 1] for h in range(IDX_HEADS)])
        sc = jnp.where(sc == 0.0, 0.0, sc)
        bits = lax.bitcast_convert_type(sc, I32)
        key = bits ^ ((bits >> 31) & 0x7FFFFFFF)
        return key if hidden is None else jnp.where(hidden, KEY_NEG_INF, key)

    variant = jnp.where(qbi == 0, 1, 0)
    kxt_ref[...] = keys_of(tail_start, TAIL, btail_ref[variant, 0, :, 0:qb] < 0.5 * NEG)

    def score_body(j, carry):
        kxf_ref[j] = keys_of(_far_start(tail_start, j), SUP, None)
        return carry

    lax.fori_loop(0, n_far - 1, score_body, 0)

    @pl.when(n_far >= 1)
    def _():
        kxf_ref[n_far - 1] = keys_of(_far_start(tail_start, n_far - 1), SUP,
                                     lax.broadcasted_iota(I32, (SUP, qb), 0) >= SUP - n_front * KB)

    row_t = lax.broadcasted_iota(I32, (TAIL, qb), 0)
    row_f = lax.broadcasted_iota(I32, (SUP, qb), 0)

    def hits(pred, kx, idx):
        return _fold_rows(jnp.add, jnp.where(pred(kx, idx), 1.0, 0.0))

    def count(pred):
        def body(j, acc):
            return acc + hits(pred, kxf_ref[j], row_f + (_far_start(tail_start, j)))
        acc = lax.fori_loop(0, n_far, body, hits(pred, kxt_ref[...], row_t + tail_start))
        return jnp.sum(acc, axis=0, keepdims=True)

    def bit_body(t, r):
        cand = r + lax.shift_left(jnp.int32(1), 31 - t)
        return jnp.where(count(lambda kx, idx: kx >= cand) >= topk, cand, r)

    r1 = lax.fori_loop(0, 32, bit_body, jnp.full((1, qb), INT_MIN, I32))

    n_ge = count(lambda kx, idx: kx >= r1)
    overflow = jnp.where((n_ge > topk) & (r1 > KEY_NEG_INF), 1.0, 0.0)

    def tie_search(_):
        need = topk - count(lambda kx, idx: kx > r1)

        def tie_body(t, c):
            cand = c + lax.shift_left(jnp.int32(1), idx_bits - 1 - t)
            cnt = count(lambda kx, idx: (kx == r1) & (idx < cand))
            return jnp.where(cnt < need, cand, c)
        return lax.fori_loop(0, idx_bits, tie_body, jnp.zeros((1, qb), I32))

    cs1 = lax.cond(jnp.max(overflow) > 0.0, tie_search,
                   lambda _: jnp.full((1, qb), 2 ** idx_bits - 1, I32), 0)

    m_ref[...] = jnp.full(m_ref.shape, NEG, F32)
    l_ref[...] = jnp.zeros(l_ref.shape, F32)
    acc_ref[...] = jnp.zeros(acc_ref.shape, F32)

    def sel_mask(kx, idx):
        sel = (kx > r1) | ((kx == r1) & (idx <= cs1))
        madd = jnp.where(sel & (kx > KEY_NEG_INF), 0.0, NEG)
        return jnp.concatenate([madd, madd], axis=1)

    def attend(starts, nblk, madds, bias, crows):
        wins = [_window_t(ka_ref, vat_ref, start, nblk) for start in starts]
        raws = [[_dot(k[:, p * LANES:(p + 1) * LANES], qast[p]) for p in range(npair)] for k, _ in wins]
        for (_, vt), raw, madd2 in zip(wins, raws, madds):
            for p in range(npair):
                x = raw[p] * LOG2E + madd2
                if bias is not None:
                    x = x + bias[:, 2 * p * qb:2 * (p + 1) * qb]
                _attn_step_t(x, crows[p], vt[p * LANES:(p + 1) * LANES],
                             m_ref.at[p], l_ref.at[p], acc_ref.at[p])

    attend([tail_start], TAIL // KB, [sel_mask(kxt_ref[...], row_t + tail_start)],
           btail_ref[variant, 0], [0.0] * npair)

    crows = [crow_ref[p] for p in range(npair)]
    far_start = lambda j: _far_start(tail_start, j)

    def attend_far(js):
        attend([far_start(j) for j in js], SUP // KB,
               [sel_mask(kxf_ref[j], row_f + far_start(j)) for j in js], None, crows)

    def far_body(jj, carry):
        attend_far([2 * jj, 2 * jj + 1])
        return carry

    lax.fori_loop(0, n_far // 2, far_body, 0)

    @pl.when(n_far % 2 == 1)
    def _():
        attend_far([n_far - 1])

    for p in range(npair):
        o2 = acc_ref[p] / jnp.sum(l_ref[p], axis=0, keepdims=True)
        o = jnp.where(row < HEAD_DIM, o2[:, 0:qb], o2[:, qb:2 * qb])
        o_ref[0, p * LANES:(p + 1) * LANES, :] = o.astype(o_ref.dtype)


def _dsa_attn_t(qa, qi, misc_f32, ka, va, ki, btail, cfar, *, qb, qb0, topk):
    b, t, _ = qa.shape
    s_tot = ka.shape[1]
    nq = t // qb
    npair = N_HEADS_A // 2
    n_far_max = max(1, -(-(s_tot - TAIL) // SUP))
    idx_bits = int(math.ceil(math.log2(s_tot))) + 1
    qat = qa.transpose(0, 2, 1)
    qit = qi.transpose(0, 2, 1)
    wit = misc_f32[:, :, IDX_DIM:IDX_DIM + IDX_HEADS].transpose(0, 2, 1)
    vat = _blocked_t(va)
    crow = jnp.repeat(cfar.reshape(npair, 2), qb, axis=1).reshape(npair, 1, 2 * qb)
    bt = btail.transpose(0, 3, 1, 2).reshape(2, 1, TAIL, N_HEADS_A * qb)
    kern = functools.partial(_dsa_kernel_t, qb=qb, qb0=qb0, topk=float(topk), idx_bits=idx_bits)
    out_t = pl.pallas_call(
        kern,
        grid=(b, nq),
        in_specs=[
            pl.BlockSpec((2, 1, TAIL, N_HEADS_A * qb), lambda bi, i: (0, 0, 0, 0)),
            pl.BlockSpec((npair, 1, 2 * qb), lambda bi, i: (0, 0, 0)),
            pl.BlockSpec((1, WA, qb), lambda bi, i: (bi, 0, i)),
            pl.BlockSpec((1, WQI, qb), lambda bi, i: (bi, 0, i)),
            pl.BlockSpec((1, IDX_HEADS, qb), lambda bi, i: (bi, 0, i)),
            _key_spec(s_tot, WA, lambda bi, i: (bi, 0, 0)),
            pl.BlockSpec((1, s_tot // KB, WA, KB), lambda bi, i: (bi, 0, 0, 0),
                         pipeline_mode=pl.Buffered(1)),
            _key_spec(s_tot, LANES, lambda bi, i: (bi, 0, 0)),
        ],
        out_specs=pl.BlockSpec((1, WA, qb), lambda bi, i: (bi, 0, i)),
        out_shape=jax.ShapeDtypeStruct((b, WA, t), BF16),
        scratch_shapes=[
            pltpu.VMEM((TAIL, qb), I32),
            pltpu.VMEM((n_far_max, SUP, qb), I32),
            pltpu.VMEM((npair, 1, 2 * qb), F32),
            pltpu.VMEM((npair, 8, 2 * qb), F32),
            pltpu.VMEM((npair, LANES, 2 * qb), F32),
        ],
        compiler_params=_cparams(("parallel", "arbitrary")),
        name="dsa_attn_t",
    )(bt, crow, qat, qit, wit, ka, vat, ki)
    return out_t.transpose(0, 2, 1)


def _outproj_kernel(oa_ref, ob_ref, oc_ref, wa_ref, wb_ref, wc_ref, g_ref, x_ref, y_ref):
    mix = _dot(oa_ref[...], wa_ref[...]) + _dot(ob_ref[...], wb_ref[...]) + _dot(oc_ref[...], wc_ref[...])
    ms = jnp.mean(mix * mix, axis=-1, keepdims=True)
    y_ref[...] = x_ref[...] + mix * lax.rsqrt(ms + EPS) * g_ref[...]


def _outproj(oa, ob, oc, w_out, g, x2d):
    m, d = x2d.shape
    tm = min(512, m)
    w = w_out.astype(BF16)
    row = lambda i: (i, 0)
    const = lambda i: (0, 0)
    return pl.pallas_call(
        _outproj_kernel,
        grid=(m // tm,),
        in_specs=[pl.BlockSpec((tm, WA), row), pl.BlockSpec((tm, WB), row), pl.BlockSpec((tm, WC), row),
                  pl.BlockSpec((WA, d), const), pl.BlockSpec((WB, d), const), pl.BlockSpec((WC, d), const),
                  pl.BlockSpec((1, d), const), pl.BlockSpec((tm, d), row)],
        out_specs=pl.BlockSpec((tm, d), row),
        out_shape=jax.ShapeDtypeStruct((m, d), F32),
        compiler_params=_cparams(("parallel",)),
        name="outproj",
    )(oa, ob, oc, w[:WA], w[WA:WA + WB], w[WA + WB:], g.reshape(1, d), x2d)


HALO = 16


def _ffn_kernel(x_ref, xh_ref, stg_ref, stv_ref, gpre_ref, gpost_ref, wg_ref, wv_ref, cwg_ref,
                cwv_ref, cbg_ref, cbv_ref, wd_ref, y_ref, convg_ref, convv_ref,
                h_ref, ug_ref, uv_ref, f_ref, *, tm, tiles_per_seq):
    i = pl.program_id(0)
    f = pl.program_id(1)
    first_of_seq = (i % tiles_per_seq) == 0

    def norm(x):
        ms = jnp.mean(x * x, axis=-1, keepdims=True)
        return (x * lax.rsqrt(ms + EPS) * gpre_ref[...]).astype(BF16)

    @pl.when(f == 0)
    def _():
        h_ref[0:HALO, :] = norm(xh_ref[...])
        h_ref[HALO:, :] = norm(x_ref[...])
        f_ref[...] = jnp.zeros(f_ref.shape, F32)

    h = h_ref[...]
    ug_ref[...] = _dot(h, wg_ref[...])
    uv_ref[...] = _dot(h, wv_ref[...])

    @pl.when(first_of_seq)
    def _():
        ug_ref[0:HALO, :] = stg_ref[0]
        uv_ref[0:HALO, :] = stv_ref[0]

    def conv(u_ref, cw_ref, cb_ref):
        u = u_ref[...]
        u1 = pltpu.roll(u, 1, 0)[HALO:, :]
        u2 = pltpu.roll(u, 2, 0)[HALO:, :]
        cw = cw_ref[...]
        return cb_ref[...] + (cw[0:1] * u2 + cw[1:2] * u1 + cw[2:3] * u[HALO:, :])

    gate = conv(ug_ref, cwg_ref, cbg_ref)
    val = conv(uv_ref, cwv_ref, cbv_ref)
    c0 = math.sqrt(2.0 / math.pi)
    gelu = 0.5 * gate * (1.0 + jnp.tanh(c0 * (gate + 0.044715 * (gate * gate * gate))))
    f_ref[...] += _dot((gelu * val).astype(BF16), wd_ref[...])

    convg_ref[0] = ug_ref[tm:, :]
    convv_ref[0] = uv_ref[tm:, :]

    @pl.when(f == pl.num_programs(1) - 1)
    def _():
        ff = f_ref[...]
        ms = jnp.mean(ff * ff, axis=-1, keepdims=True)
        y_ref[...] = x_ref[...] + ff * lax.rsqrt(ms + EPS) * gpost_ref[...]


def _ffn(x2d, state, g_pre, g_post, w_up, conv_w, conv_b, w_down, *, seq):
    m, d = x2d.shape
    b = m // seq
    d_ff = w_down.shape[0]
    tn = 1408
    assert d_ff % tn == 0 and tn % LANES == 0
    nf = d_ff // tn
    tm = min(512, seq)
    assert seq % tm == 0 and tm % HALO == 0
    tiles_per_seq = seq // tm
    n_tiles = m // tm
    st = jnp.pad(state.astype(F32), ((0, 0), (HALO - (CONV_W - 1), 0), (0, 0)))
    wu = w_up.astype(BF16)
    wd = w_down.astype(BF16)
    kern = functools.partial(_ffn_kernel, tm=tm, tiles_per_seq=tiles_per_seq)
    halo_blocks = tm // HALO
    y, conv_g, conv_v = pl.pallas_call(
        kern,
        grid=(n_tiles, nf),
        in_specs=[
            pl.BlockSpec((tm, d), lambda i, f: (i, 0)),
            pl.BlockSpec((HALO, d), lambda i, f: (jnp.maximum(i * halo_blocks - 1, 0), 0)),
            pl.BlockSpec((1, HALO, tn), lambda i, f: (i // tiles_per_seq, 0, f)),
            pl.BlockSpec((1, HALO, tn), lambda i, f: (i // tiles_per_seq, 0, nf + f)),
            pl.BlockSpec((1, d), lambda i, f: (0, 0)),
            pl.BlockSpec((1, d), lambda i, f: (0, 0)),
            pl.BlockSpec((d, tn), lambda i, f: (0, f)),
            pl.BlockSpec((d, tn), lambda i, f: (0, nf + f)),
            pl.BlockSpec((CONV_W, tn), lambda i, f: (0, f)),
            pl.BlockSpec((CONV_W, tn), lambda i, f: (0, nf + f)),
            pl.BlockSpec((1, tn), lambda i, f: (0, f)),
            pl.BlockSpec((1, tn), lambda i, f: (0, nf + f)),
            pl.BlockSpec((tn, d), lambda i, f: (f, 0)),
        ],
        out_specs=[pl.BlockSpec((tm, d), lambda i, f: (i, 0)),
                   pl.BlockSpec((1, HALO, tn), lambda i, f: (i, 0, f)),
                   pl.BlockSpec((1, HALO, tn), lambda i, f: (i, 0, f))],
        out_shape=[jax.ShapeDtypeStruct((m, d), F32),
                   jax.ShapeDtypeStruct((n_tiles, HALO, d_ff), F32),
                   jax.ShapeDtypeStruct((n_tiles, HALO, d_ff), F32)],
        scratch_shapes=[pltpu.VMEM((HALO + tm, d), BF16),
                        pltpu.VMEM((HALO + tm, tn), F32),
                        pltpu.VMEM((HALO + tm, tn), F32),
                        pltpu.VMEM((tm, d), F32)],
        compiler_params=_cparams(("parallel", "arbitrary")),
        name="ffn",
    )(x2d, x2d, st, st, g_pre.reshape(1, d), g_post.reshape(1, d), wu, wu,
      conv_w, conv_w, conv_b.reshape(1, -1), conv_b.reshape(1, -1), wd)
    last = lambda c: c.reshape(b, tiles_per_seq, HALO, d_ff)[:, -1, HALO - (CONV_W - 1):]
    return y, jnp.concatenate([last(conv_g), last(conv_v)], axis=-1)


def _lambda_init(l):
    return 0.8 - 0.6 * math.exp(-0.3 * l)


def _past_and_tail(past, new):
    b, t, w = new.shape
    p = past.shape[1]
    past = past.reshape(b, p, -1)
    last = past[:, p - (TAIL - KB):].astype(BF16)
    if last.shape[2] < w:
        last = jnp.pad(last, ((0, 0), (0, 0), (0, w - last.shape[2])))
    return past, jnp.concatenate([last, new, jnp.zeros((b, KB - t, w), BF16)], axis=1)


def _layer(x, past, l, prm):
    b, t, d = x.shape
    p_len = 0 if past is None else past["k_a"].shape[1]
    s = p_len + t
    qb = min(KB, t)
    assert p_len % KB == 0 and t % qb == 0 and (qb == KB or t == qb)
    qb0 = p_len // KB
    s_pad = -(-s // KB) * KB
    assert s_pad >= max(TAIL, SUP)
    n_valid = s - (s_pad - KB)
    topk = min(TOPK_MAX, s // 4)
    m = b * t

    pr = _proj(x.reshape(m, d), prm["g_pre_mix"][l], _pad_w_in(prm["w_in"][l]))
    r3 = lambda a: a.reshape(b, t, a.shape[-1])

    btail_a, cfar_a = _bias_tiles(prm["rel_bias"][:, :N_HEADS_A], qb, n_valid)
    btail_c, cfar_c = _bias_tiles(prm["rel_bias"][:, N_HEADS_A:], qb, n_valid)

    lamv = jnp.pad(jnp.stack([prm["lambda_q1"][l], prm["lambda_k1"][l],
                              prm["lambda_q2"][l], prm["lambda_k2"][l]]).astype(F32),
                   ((0, 0), (0, LANES - DC_HALF)))
    gain2 = jnp.tile(prm["subln_gain"][l].astype(F32), 2).reshape(1, LANES)
    if past is None:
        assert qb == KB
        o_a = _dsa_attn_t(r3(pr["qa_bf16"]), r3(pr["qi_f32"]), r3(pr["misc_f32"]),
                          r3(pr["ka_bf16"]), r3(pr["va_bf16"]), r3(pr["misc_bf16"]),
                          btail_a, cfar_a, qb=qb, qb0=qb0, topk=topk)
        o_b = _sb_attn(r3(pr["qb_bf16"]), r3(pr["kb_bf16"]), r3(pr["vb_bf16"]), qb=qb, qb0=qb0)
        o_c = _diff_attn_t(r3(pr["qc_bf16"]), r3(pr["kc_bf16"]), r3(pr["vc_bf16"]),
                           btail_c, cfar_c, lamv, gain2, qb=qb, qb0=qb0, lam0=_lambda_init(l))
    else:
        assert t == qb and p_len >= max(TAIL - KB, SUP)
        o_a = _dsa_attn(r3(pr["qa_bf16"]), r3(pr["qi_f32"]), r3(pr["misc_f32"]),
                        _past_and_tail(past["k_a"], r3(pr["ka_bf16"])),
                        _past_and_tail(past["v_a"], r3(pr["va_bf16"])),
                        _past_and_tail(past["k_i"], r3(pr["misc_bf16"])),
                        btail_a, cfar_a, qb=qb, qb0=qb0, topk=topk)
        diag = lambda new: jnp.concatenate([new, jnp.zeros((b, KB - t, new.shape[2]), BF16)], axis=1)
        o_b = _sb_attn(r3(pr["qb_bf16"]), past["k_b"].reshape(b, p_len, WB),
                       past["v_b"].reshape(b, p_len, WB),
                       (diag(r3(pr["kb_bf16"])), diag(r3(pr["vb_bf16"]))), qb=qb, qb0=qb0)
        o_c = _diff_attn(r3(pr["qc_bf16"]),
                         _past_and_tail(past["k_c"], r3(pr["kc_bf16"])),
                         _past_and_tail(past["v_c"], r3(pr["vc_bf16"])),
                         btail_c, cfar_c, lamv, gain2, qb=qb, qb0=qb0, lam0=_lambda_init(l))

    x1 = _outproj(o_a.reshape(m, WA), o_b.reshape(m, WB), o_c.reshape(m, WC),
                  prm["w_out"][l], prm["g_post_mix"][l], x.reshape(m, d))
    state = (jnp.zeros((b, CONV_W - 1, prm["w_up"].shape[2]), F32) if past is None
             else past["conv"])
    x2, new_conv = _ffn(x1, state, prm["g_pre_ffn"][l], prm["g_post_ffn"][l], prm["w_up"][l],
                        prm["conv_w"][l], prm["conv_b"][l], prm["w_down"][l], seq=t)
    new = (pr["ka_f32"].reshape(b, t, N_HEADS_A, HEAD_DIM),
           pr["va_f32"].reshape(b, t, N_HEADS_A, HEAD_DIM),
           pr["misc_f32"][:, :IDX_DIM].reshape(b, t, IDX_DIM),
           pr["kb_f32"].reshape(b, t, N_HEADS_B, HEAD_DIM),
           pr["vb_f32"].reshape(b, t, N_HEADS_B, HEAD_DIM),
           pr["kc_f32"].reshape(b, t, N_HEADS_C, HEAD_DIM),
           pr["vc_f32"].reshape(b, t, N_HEADS_C, HEAD_DIM),
           new_conv)
    return x2.reshape(b, t, d), new


def _run_trunk(x, caches, prm):
    depth = prm["w_in"].shape[0]
    news = []
    for l in range(depth):
        past = None if caches is None else {k: v[l] for k, v in caches.items()}
        x, new = _layer(x, past, l, prm)
        news.append(new)
    return x, [jnp.stack([n[i] for n in news]) for i in range(len(news[0]))]


def kernel(x_prompt, x_sample, cache_k_a, cache_v_a, cache_idx_k, cache_k_b, cache_v_b, cache_k_c,
           cache_v_c, state_ffn_conv, w_in, w_out, rel_bias, lambda_q1, lambda_k1, lambda_q2,
           lambda_k2, subln_gain, g_pre_mix, g_post_mix, g_pre_ffn, g_post_ffn, w_up, conv_w,
           conv_b, w_down):
    prm = dict(w_in=w_in, w_out=w_out, rel_bias=rel_bias, lambda_q1=lambda_q1, lambda_k1=lambda_k1,
               lambda_q2=lambda_q2, lambda_k2=lambda_k2, subln_gain=subln_gain, g_pre_mix=g_pre_mix,
               g_post_mix=g_post_mix, g_pre_ffn=g_pre_ffn, g_post_ffn=g_post_ffn, w_up=w_up,
               conv_w=conv_w, conv_b=conv_b, w_down=w_down)
    y_prompt, p_new = _run_trunk(x_prompt, None, prm)
    caches = dict(k_a=cache_k_a, v_a=cache_v_a, k_i=cache_idx_k, k_b=cache_k_b, v_b=cache_v_b,
                  k_c=cache_k_c, v_c=cache_v_c, conv=state_ffn_conv)
    y_sample, s_new = _run_trunk(x_sample, caches, prm)
    return (y_prompt, y_sample, *p_new, *s_new)
```

```python
import functools
import math

import numpy as np
import jax
import jax.numpy as jnp
from jax import lax
from jax.experimental import pallas as pl
from jax.experimental.pallas import tpu as pltpu

F32 = jnp.float32
BF16 = jnp.bfloat16
I32 = jnp.int32

LANES = 128
VMEM_LIMIT = 56 * 1024 * 1024

CHUNK = 64
HEAD_DIM = 64
N_HEADS_A = 6
N_HEADS_B = 6
N_HEADS_C = 4
DC_HALF = HEAD_DIM // 2
IDX_HEADS = 8
IDX_DIM = 32
TOPK_MAX = 256
NUM_BUCKETS = 32
MAX_DISTANCE = 128
CONV_W = 3
EPS = 1e-6

WA = N_HEADS_A * HEAD_DIM
WB = N_HEADS_B * HEAD_DIM
WC = N_HEADS_C * HEAD_DIM
WQI = IDX_HEADS * IDX_DIM

NEG = -1e30
INT_MIN = -2 ** 31
KEY_NEG_INF = int(np.int32(np.float32(-np.inf).view(np.int32)) ^ np.int32(0x7FFFFFFF))
EXP_ZERO = -104.0
LOG2E = math.log2(math.e)

KB = 128
TAIL = 2 * KB
SUP = 4 * KB
assert TAIL - KB >= MAX_DISTANCE
NT_DIMS = (((1,), (1,)), ((), ()))


def _cparams(sem):
    return pltpu.CompilerParams(dimension_semantics=sem, vmem_limit_bytes=VMEM_LIMIT)


def _dot_nt(a, b):
    return lax.dot_general(a, b, NT_DIMS, preferred_element_type=F32)


def _dot(a, b):
    return jnp.dot(a, b, preferred_element_type=F32)


_PROJ_GROUPS = (
    ("qa", 0, WA, HEAD_DIM ** -0.5, False, True),
    ("ka", 384, WA, 1.0, True, True),
    ("va", 768, WA, 1.0, True, True),
    ("qi", 1152, WQI, 1.0, True, False),
    ("misc", 1408, LANES, 1.0, True, True),
    ("qb", 1536, WB, HEAD_DIM ** -0.5, False, True),
    ("kb", 1920, WB, 1.0, True, True),
    ("vb", 2304, WB, 1.0, True, True),
    ("qc", 2688, WC, 1.0, False, True),
    ("kc", 2944, WC, 1.0, True, True),
    ("vc", 3200, WC, 1.0, True, True),
)
_PROJ_NPAD = 3456


def _proj_out_names():
    names = []
    for name, _, _, _, f32o, bf16o in _PROJ_GROUPS:
        if f32o:
            names.append(name + "_f32")
        if bf16o:
            names.append(name + "_bf16")
    return names


def _proj_kernel(x_ref, g_ref, w_ref, *out_refs):
    x = x_ref[...]
    ms = jnp.mean(x * x, axis=-1, keepdims=True)
    h = (x * lax.rsqrt(ms + EPS) * g_ref[...]).astype(BF16)
    k = 0
    for _, off, width, scale, f32o, bf16o in _PROJ_GROUPS:
        r = _dot(h, w_ref[:, off:off + width])
        if f32o:
            out_refs[k][...] = r
            k += 1
        if bf16o:
            out_refs[k][...] = (r * scale).astype(BF16) if scale != 1.0 else r.astype(BF16)
            k += 1


def _pad_w_in(w):
    d = w.shape[0]
    misc = jnp.concatenate([w[:, 1416:1448], w[:, 1408:1416],
                            jnp.zeros((d, LANES - IDX_DIM - IDX_HEADS), w.dtype)], axis=1)
    return jnp.concatenate([w[:, 0:1408], misc, w[:, 1448:]], axis=1).astype(BF16)


def _proj(x2d, g, w_pad):
    m, d = x2d.shape
    tm = min(512, m)
    assert m % tm == 0
    out_shapes, out_specs = [], []
    for _, _, width, _, f32o, bf16o in _PROJ_GROUPS:
        for want, dt in ((f32o, F32), (bf16o, BF16)):
            if want:
                out_shapes.append(jax.ShapeDtypeStruct((m, width), dt))
                out_specs.append(pl.BlockSpec((tm, width), lambda i: (i, 0)))
    outs = pl.pallas_call(
        _proj_kernel,
        grid=(m // tm,),
        in_specs=[pl.BlockSpec((tm, d), lambda i: (i, 0)),
                  pl.BlockSpec((1, d), lambda i: (0, 0)),
                  pl.BlockSpec((d, _PROJ_NPAD), lambda i: (0, 0))],
        out_specs=out_specs,
        out_shape=out_shapes,
        compiler_params=_cparams(("parallel",)),
        name="proj",
    )(x2d, g.reshape(1, d), w_pad)
    return dict(zip(_proj_out_names(), outs))


def _t5_bucket(rel):
    half = NUM_BUCKETS // 2
    max_exact = half // 2
    ret = jnp.where(rel > 0, half, 0)
    n = jnp.abs(rel)
    nf = jnp.maximum(n, 1).astype(F32)
    large = max_exact + (jnp.log(nf / max_exact) / math.log(MAX_DISTANCE / max_exact)
                         * (half - max_exact)).astype(I32)
    large = jnp.minimum(large, half - 1)
    return ret + jnp.where(n < max_exact, n, large)


def _bias_tiles(tab, qb, n_valid):
    qq = jnp.arange(qb, dtype=I32)[:, None]
    kk = jnp.arange(TAIL, dtype=I32)[None, :] - (TAIL - KB)
    tile = tab[_t5_bucket(kk - qq)].astype(F32).transpose(2, 0, 1) * LOG2E
    vis = ((kk // CHUNK) <= (qq // CHUNK)) & (kk < n_valid)
    tile = jnp.where(vis[None], tile, NEG)
    tile1 = jnp.concatenate([tile[..., TAIL - KB:], jnp.full(tile.shape[:2] + (TAIL - KB,), NEG, F32)],
                            axis=-1)
    far = tab[_t5_bucket(jnp.array(-MAX_DISTANCE, I32))].astype(F32) * LOG2E
    return jnp.stack([tile, tile1]), far


def _front_mask_rows():
    col = jnp.arange(SUP, dtype=I32)[None, None, :]
    v = jnp.arange(SUP // KB, dtype=I32)[:, None, None]
    return jnp.where(col >= SUP - v * KB, NEG, 0.0).astype(F32)


def _windows(qbi):
    tail_start = jnp.maximum((qbi + 1) * KB - TAIL, 0)
    n_far = (tail_start + SUP - 1) // SUP
    n_front = (n_far * SUP - tail_start) // KB
    return tail_start, n_far, n_front


def _far_start(tail_start, j):
    return jnp.maximum(tail_start - SUP * (j + 1), 0)


def _lane_chunks(x):
    return [x[:, c * LANES:(c + 1) * LANES] for c in range(x.shape[1] // LANES)]


def _attn_step(x, ccol, v, m_ref, l_ref, acc_ref):
    chunks = _lane_chunks(x)
    mx = functools.reduce(jnp.maximum, chunks)
    m_old = m_ref[...]
    m_new = jnp.maximum(m_old, jnp.max(mx, axis=1, keepdims=True) + ccol)
    p = jnp.exp2(x + (ccol - m_new))
    alpha = jnp.exp2(m_old - m_new)
    l_ref[...] = alpha * l_ref[...] + functools.reduce(jnp.add, _lane_chunks(p))
    acc_ref[...] = alpha * acc_ref[...] + _dot(p.astype(BF16), v)
    m_ref[...] = m_new


FOLD_CHAINS = 4
DIFF_GROUPS = 1


def _fold_rows(op, x):
    w, r = x.shape
    x3 = x.reshape(w // 8, 8, r)
    chains = [x3[i] for i in range(FOLD_CHAINS)]
    for i in range(FOLD_CHAINS, w // 8):
        chains[i % FOLD_CHAINS] = op(chains[i % FOLD_CHAINS], x3[i])
    return functools.reduce(op, chains)


def _attn_step_t(x, crow, vt, m_ref, l_ref, acc_ref):
    m_old = m_ref[...]
    m_new = jnp.maximum(m_old, jnp.max(_fold_rows(jnp.maximum, x), axis=0, keepdims=True) + crow)
    p = jnp.exp2(x + (crow - m_new))
    alpha = jnp.exp2(m_old - m_new)
    l_ref[...] = alpha * l_ref[...] + _fold_rows(jnp.add, p)
    acc_ref[...] = alpha * acc_ref[...] + _dot(vt, p.astype(BF16))
    m_ref[...] = m_new


def _front_mask_t(n_front, shape):
    return jnp.where(lax.broadcasted_iota(I32, shape, 0) >= SUP - n_front * KB, NEG, 0.0)


def _blocked_t(v):
    b, s, w = v.shape
    return v.reshape(b, s // KB, KB, w).transpose(0, 1, 3, 2)


def _window_t(k_ref, vt_ref, start, nblk):
    off = pl.multiple_of(start, KB)
    k = k_ref[0, pl.ds(off, nblk * KB), :]
    v4 = vt_ref[0, pl.ds((start) // KB, nblk)]
    vt = jnp.concatenate([v4[n] for n in range(nblk)], axis=1)
    return k, vt


def _key_spec(s_tot, width, index_map):
    return pl.BlockSpec((1, s_tot, width), index_map, pipeline_mode=pl.Buffered(1))


def _diff_kernel(lam_ref, gain_ref, btail_ref, cfar_ref, fmask_ref, q_ref, kp_ref, vp_ref,
                 kt_ref, vt_ref, o_ref, m_ref, l_ref, acc_ref, *, qb, qb0, lam0):
    qbi = qb0 + pl.program_id(2)
    tail_start, n_far, n_front = _windows(qbi)
    q = q_ref[0]
    lane = lax.broadcasted_iota(I32, (qb, LANES), 1)
    qs = jnp.concatenate(
        [jnp.where((lane >= DC_HALF * c) & (lane < DC_HALF * (c + 1)), q, jnp.zeros_like(q))
         for c in range(4)], axis=0)
    m_ref[...] = jnp.full(m_ref.shape, NEG, F32)
    l_ref[...] = jnp.zeros(l_ref.shape, F32)
    acc_ref[...] = jnp.zeros(acc_ref.shape, F32)
    a = DC_HALF ** -0.5 * LOG2E

    def window(start, width):
        off = pl.multiple_of(start, KB)
        return (kp_ref[0, pl.ds(off, width), :].astype(BF16),
                vp_ref[0, pl.ds(off, width), :].astype(BF16))

    k, v = kt_ref[0], vt_ref[0]
    bt = btail_ref[jnp.where(qbi == 0, 1, 0)]
    x = (_dot_nt(qs, k) * a).reshape(2, 2, qb, TAIL) + bt[:, None]
    _attn_step(x.reshape(4 * qb, TAIL), 0.0, v, m_ref, l_ref, acc_ref)

    ccol = cfar_ref[0]

    def far_body(j, carry):
        k, v = window(_far_start(tail_start, j), SUP)
        _attn_step(_dot_nt(qs, k) * a, ccol, v, m_ref, l_ref, acc_ref)
        return carry

    lax.fori_loop(0, n_far - 1, far_body, 0)

    @pl.when(n_far >= 1)
    def _():
        k, v = window(_far_start(tail_start, n_far - 1), SUP)
        _attn_step(_dot_nt(qs, k) * a + fmask_ref[n_front], ccol, v, m_ref, l_ref, acc_ref)

    lam_v = lam_ref[...]
    s1 = jnp.sum(lam_v[0:1] * lam_v[1:2], axis=1, keepdims=True)
    s2 = jnp.sum(lam_v[2:3] * lam_v[3:4], axis=1, keepdims=True)
    lam = jnp.exp(s1) - jnp.exp(s2) + lam0
    o4 = acc_ref[...] / jnp.sum(l_ref[...], axis=1, keepdims=True)
    o0 = o4[0:qb] - lam * o4[qb:2 * qb]
    o1 = o4[2 * qb:3 * qb] - lam * o4[3 * qb:4 * qb]
    first = lane < HEAD_DIM
    o = jnp.where(first, o0, o1)
    sq = o * o
    ms0 = jnp.sum(jnp.where(first, sq, 0.0), axis=1, keepdims=True)
    ms1 = jnp.sum(jnp.where(first, 0.0, sq), axis=1, keepdims=True)
    ms = jnp.where(first, ms0, ms1) * (1.0 / HEAD_DIM)
    y = o * lax.rsqrt(ms + EPS) * gain_ref[...]
    o_ref[0] = (y * (1.0 - lam0)).astype(o_ref.dtype)


def _diff_attn(q, k, v, btail, cfar, lamv, gain2, *, qb, qb0, lam0):
    (kp, kt), (vp, vt) = k, v
    b, t, _ = q.shape
    p_len = kp.shape[1]
    nq = t // qb
    assert nq == 1
    npair = N_HEADS_C // 2
    past_spec = pl.BlockSpec((1, p_len, LANES), lambda bi, p, i: (bi, 0, p), pipeline_mode=pl.Buffered(1))
    tail_spec = pl.BlockSpec((1, TAIL, LANES), lambda bi, p, i: (bi, 0, p))
    ccol = jnp.repeat(cfar.reshape(npair, 2), 2 * qb, axis=1).reshape(npair, 4 * qb, 1)
    btail = btail.reshape(2, npair, 2, qb, TAIL)
    kern = functools.partial(_diff_kernel, qb=qb, qb0=qb0, lam0=lam0)
    return pl.pallas_call(
        kern,
        grid=(b, npair, nq),
        in_specs=[
            pl.BlockSpec((4, LANES), lambda bi, p, i: (0, 0)),
            pl.BlockSpec((1, LANES), lambda bi, p, i: (0, 0)),
            pl.BlockSpec((2, None, 2, qb, TAIL), lambda bi, p, i: (0, p, 0, 0, 0)),
            pl.BlockSpec((1, 4 * qb, 1), lambda bi, p, i: (p, 0, 0)),
            pl.BlockSpec((SUP // KB, 1, SUP), lambda bi, p, i: (0, 0, 0)),
            pl.BlockSpec((1, qb, LANES), lambda bi, p, i: (bi, i, p)),
            past_spec, past_spec, tail_spec, tail_spec,
        ],
        out_specs=pl.BlockSpec((1, qb, LANES), lambda bi, p, i: (bi, i, p)),
        out_shape=jax.ShapeDtypeStruct((b, t, WC), BF16),
        scratch_shapes=[pltpu.VMEM((4 * qb, 1), F32), pltpu.VMEM((4 * qb, LANES), F32),
                        pltpu.VMEM((4 * qb, LANES), F32)],
        compiler_params=_cparams(("parallel", "parallel", "arbitrary")),
        name="diff_attn",
    )(lamv, gain2, btail, ccol, _front_mask_rows(), q, kp, vp, kt, vt)


def _diff_kernel_t(lam_ref, gain_ref, btail_ref, crow_ref, qt_ref, k_ref, vt_ref, o_ref,
                   m_ref, l_ref, acc_ref, *, qb, qb0, lam0):
    qbi = qb0 + pl.program_id(2)
    tail_start, n_far, n_front = _windows(qbi)
    qt = qt_ref[0]
    row = lax.broadcasted_iota(I32, (LANES, qb), 0)
    qst = jnp.concatenate(
        [jnp.where((row >= DC_HALF * c) & (row < DC_HALF * (c + 1)), qt, jnp.zeros_like(qt))
         for c in range(4)], axis=1)
    m_ref[...] = jnp.full(m_ref.shape, NEG, F32)
    l_ref[...] = jnp.zeros(l_ref.shape, F32)
    acc_ref[...] = jnp.zeros(acc_ref.shape, F32)
    a = DC_HALF ** -0.5 * LOG2E

    gw = 4 * qb // DIFF_GROUPS

    def steps(starts, nblk, add, far):
        wins = [_window_t(k_ref, vt_ref, start, nblk) for start in starts]
        raws = [_dot(k, qst) for k, _ in wins]
        for (_, vt), raw in zip(wins, raws):
            for g in range(DIFF_GROUPS):
                cols = slice(g * gw, (g + 1) * gw)
                x = raw[:, cols] * a
                if add is not None:
                    x = x + add(cols)
                _attn_step_t(x, crow_ref[0, :, cols] if far else 0.0, vt,
                             m_ref.at[g], l_ref.at[g], acc_ref.at[g])

    variant = jnp.where(qbi == 0, 1, 0)
    steps([tail_start], TAIL // KB, lambda cols: btail_ref[variant, :, cols], False)

    far_start = lambda j: _far_start(tail_start, j)
    n_full = n_far - 1

    def far_body(jj, carry):
        steps([far_start(2 * jj), far_start(2 * jj + 1)], SUP // KB, None, True)
        return carry

    lax.fori_loop(0, n_full // 2, far_body, 0)

    @pl.when(jnp.logical_and(n_full > 0, n_full % 2 == 1))
    def _():
        steps([far_start(n_full - 1)], SUP // KB, None, True)

    @pl.when(n_far >= 1)
    def _():
        fm = _front_mask_t(n_front, (SUP, gw))
        steps([far_start(n_far - 1)], SUP // KB, lambda cols: fm, True)

    acc = jnp.concatenate([acc_ref[g] for g in range(DIFF_GROUPS)], axis=1)
    l_sum = jnp.concatenate([jnp.sum(l_ref[g], axis=0, keepdims=True) for g in range(DIFF_GROUPS)], axis=1)

    lam_v = lam_ref[...]
    s1 = jnp.sum(lam_v[0:1] * lam_v[1:2], axis=1, keepdims=True)
    s2 = jnp.sum(lam_v[2:3] * lam_v[3:4], axis=1, keepdims=True)
    lam = jnp.exp(s1) - jnp.exp(s2) + lam0
    o4 = acc / l_sum
    o0 = o4[:, 0:qb] - lam * o4[:, qb:2 * qb]
    o1 = o4[:, 2 * qb:3 * qb] - lam * o4[:, 3 * qb:4 * qb]
    first = row < HEAD_DIM
    o = jnp.where(first, o0, o1)
    sq = o * o
    ms0 = jnp.sum(sq[0:HEAD_DIM], axis=0, keepdims=True)
    ms1 = jnp.sum(sq[HEAD_DIM:], axis=0, keepdims=True)
    ms = jnp.where(first, ms0, ms1) * (1.0 / HEAD_DIM)
    y = o * lax.rsqrt(ms + EPS) * gain_ref[...]
    o_ref[0] = (y * (1.0 - lam0)).astype(o_ref.dtype)


def _diff_attn_t(q, k, v, btail, cfar, lamv, gain2, *, qb, qb0, lam0):
    b, t, _ = q.shape
    s_tot = k.shape[1]
    nq = t // qb
    npair = N_HEADS_C // 2
    qt = q.transpose(0, 2, 1)
    vt = _blocked_t(v)
    crow = jnp.repeat(cfar.reshape(npair, 2), 2 * qb, axis=1).reshape(npair, 1, 4 * qb)
    bt = btail.reshape(2, npair, 2, qb, TAIL).transpose(0, 1, 4, 2, 3)
    bt = jnp.repeat(bt[:, :, :, :, None, :], 2, axis=4).reshape(2, npair, TAIL, 4 * qb)
    gain_t = jnp.broadcast_to(gain2.reshape(LANES, 1), (LANES, qb))
    kern = functools.partial(_diff_kernel_t, qb=qb, qb0=qb0, lam0=lam0)
    out_t = pl.pallas_call(
        kern,
        grid=(b, npair, nq),
        in_specs=[
            pl.BlockSpec((4, LANES), lambda bi, p, i: (0, 0)),
            pl.BlockSpec((LANES, qb), lambda bi, p, i: (0, 0)),
            pl.BlockSpec((2, None, TAIL, 4 * qb), lambda bi, p, i: (0, p, 0, 0)),
            pl.BlockSpec((1, 1, 4 * qb), lambda bi, p, i: (p, 0, 0)),
            pl.BlockSpec((1, LANES, qb), lambda bi, p, i: (bi, p, i)),
            _key_spec(s_tot, LANES, lambda bi, p, i: (bi, 0, p)),
            pl.BlockSpec((1, s_tot // KB, LANES, KB), lambda bi, p, i: (bi, 0, p, 0),
                         pipeline_mode=pl.Buffered(1)),
        ],
        out_specs=pl.BlockSpec((1, LANES, qb), lambda bi, p, i: (bi, p, i)),
        out_shape=jax.ShapeDtypeStruct((b, WC, t), BF16),
        scratch_shapes=[pltpu.VMEM((DIFF_GROUPS, 1, 4 * qb // DIFF_GROUPS), F32),
                        pltpu.VMEM((DIFF_GROUPS, 8, 4 * qb // DIFF_GROUPS), F32),
                        pltpu.VMEM((DIFF_GROUPS, LANES, 4 * qb // DIFF_GROUPS), F32)],
        compiler_params=_cparams(("parallel", "parallel", "arbitrary")),
        name="diff_attn_t",
    )(lamv, gain_t, bt, crow, qt, k, vt)
    return out_t.transpose(0, 2, 1)


def _sb_kernel(q_ref, k_ref, v_ref, *rest, qb, qb0, has_past):
    if has_past:
        kn_ref, vn_ref, o_ref, carry_ref, acc_ref = rest
    else:
        o_ref, carry_ref, acc_ref = rest
    npair = N_HEADS_B // 2
    qbi = qb0 + pl.program_id(1)
    lane = lax.broadcasted_iota(I32, (qb, LANES), 1)
    rows = lax.broadcasted_iota(I32, (2 * qb, KB), 0)
    causal = lax.broadcasted_iota(I32, (2 * qb, KB), 1) < jnp.where(rows >= qb, rows - qb, rows)
    tri = jnp.where(lax.broadcasted_iota(I32, (KB, KB), 0) > lax.broadcasted_iota(I32, (KB, KB), 1),
                    1.0, 0.0).astype(BF16)
    q = q_ref[0]
    qs = []
    for p in range(npair):
        part = q[:, p * LANES:(p + 1) * LANES]
        qs.append(jnp.concatenate([jnp.where(lane < HEAD_DIM, part, jnp.zeros_like(part)),
                                   jnp.where(lane < HEAD_DIM, jnp.zeros_like(part), part)], axis=0))
    carry_ref[...] = jnp.zeros(carry_ref.shape, F32)
    acc_ref[...] = jnp.zeros(acc_ref.shape, F32)

    def block(j, masked):
        if masked and has_past:
            k, v = kn_ref[0], vn_ref[0]
        else:
            off = pl.multiple_of(j * KB, KB)
            k = k_ref[0, pl.ds(off, KB), :].astype(BF16)
            v = v_ref[0, pl.ds(off, KB), :].astype(BF16)
        pairs = range(npair)
        zs = [_dot_nt(qs[p], k[:, p * LANES:(p + 1) * LANES]) for p in pairs]
        sps = [jnp.maximum(z, 0.0) + jnp.log1p(jnp.exp(-jnp.abs(z))) for z in zs]
        lgs = [-sp for sp in sps]
        if masked:
            lgs = [jnp.where(causal, lg, 0.0) for lg in lgs]
        tails = []
        for lg in lgs:
            l1 = lg.astype(BF16)
            r1 = lg - l1.astype(F32)
            l2 = r1.astype(BF16)
            l3 = (r1 - l2.astype(F32)).astype(BF16)
            tails.append(_dot(l1, tri) + _dot(l2, tri) + _dot(l3, tri))
        carries = [carry_ref[p] for p in pairs]
        ws = [jnp.exp(zs[p] - sps[p] + tails[p] + carries[p]) for p in pairs]
        if masked:
            ws = [jnp.where(causal, w, 0.0) for w in ws]
        pvs = [_dot(ws[p].astype(BF16), v[:, p * LANES:(p + 1) * LANES]) for p in pairs]
        worst = None
        for p in pairs:
            acc_ref[p] += pvs[p]
            carry_new = carries[p] + jnp.sum(lgs[p], axis=1, keepdims=True)
            carry_ref[p] = carry_new
            worst = carry_new if worst is None else jnp.maximum(worst, carry_new)
        return jnp.max(worst)

    mx0 = block(qbi, True)

    def cond(st):
        return jnp.logical_and(st[0] >= 0, st[1] > EXP_ZERO)

    def body(st):
        return st[0] - 1, block(st[0], False)

    lax.while_loop(cond, body, (qbi - 1, mx0))
    for p in range(npair):
        acc = acc_ref[p]
        o_ref[0, :, p * LANES:(p + 1) * LANES] = jnp.where(
            lane < HEAD_DIM, acc[0:qb], acc[qb:2 * qb]).astype(o_ref.dtype)


def _sb_attn(q, k, v, new=None, *, qb, qb0):
    b, t, _ = q.shape
    s_pad = k.shape[1]
    nq = t // qb
    npair = N_HEADS_B // 2
    kern = functools.partial(_sb_kernel, qb=qb, qb0=qb0, has_past=new is not None)
    new_specs = [] if new is None else [pl.BlockSpec((1, KB, WB), lambda bi, i: (bi, 0, 0))] * 2
    return pl.pallas_call(
        kern,
        grid=(b, nq),
        in_specs=[
            pl.BlockSpec((1, qb, WB), lambda bi, i: (bi, i, 0)),
            _key_spec(s_pad, WB, lambda bi, i: (bi, 0, 0)),
            _key_spec(s_pad, WB, lambda bi, i: (bi, 0, 0)),
        ] + new_specs,
        out_specs=pl.BlockSpec((1, qb, WB), lambda bi, i: (bi, i, 0)),
        out_shape=jax.ShapeDtypeStruct((b, t, WB), BF16),
        scratch_shapes=[pltpu.VMEM((npair, 2 * qb, 1), F32), pltpu.VMEM((npair, 2 * qb, LANES), F32)],
        compiler_params=_cparams(("parallel", "arbitrary")),
        name="sb_attn",
    )(q, k, v, *(() if new is None else new))


def _dsa_kernel(btail_ref, cfar_ref, fmask_ref, qa_ref, qi_ref, wi_ref, kap_ref, vap_ref, kip_ref,
                kat_ref, vat_ref, kit_ref, o_ref, kxt_ref, kxf_ref, wb_ref, r_ref, cnt_ref, cs_ref,
                m_ref, l_ref, acc_ref, *, qb, qb0, topk, idx_bits):
    npair = N_HEADS_A // 2
    qbi = qb0 + pl.program_id(1)
    tail_start, n_far, n_front = _windows(qbi)
    lane = lax.broadcasted_iota(I32, (qb, LANES), 1)

    def past(ref, start, width):
        off = pl.multiple_of(start, KB)
        return ref[0, pl.ds(off, width), :].astype(BF16)

    qi = qi_ref[0]
    parts = []
    for h in range(IDX_HEADS):
        g, sh = divmod(h * IDX_DIM, LANES)
        part = qi[:, g * LANES:(g + 1) * LANES]
        if sh:
            part = pltpu.roll(part, LANES - sh, 1)
        parts.append(jnp.where(lane < IDX_DIM, part, 0.0).astype(BF16))
    qis = jnp.concatenate(parts, axis=0)
    wi = wi_ref[0] * (IDX_HEADS ** -0.5 * IDX_DIM ** -0.5)
    for h in range(IDX_HEADS):
        wb_ref[h * qb:(h + 1) * qb, :] = jnp.broadcast_to(wi[:, IDX_DIM + h:IDX_DIM + h + 1],
                                                           (qb, LANES))
    qa = qa_ref[0]
    qas = []
    for p in range(npair):
        part = qa[:, p * LANES:(p + 1) * LANES]
        qas.append(jnp.concatenate(
            [jnp.where(lane < HEAD_DIM, part, jnp.zeros_like(part)),
             jnp.where(lane < HEAD_DIM, jnp.zeros_like(part), part)], axis=0))

    qis_past = qis[:, 0:IDX_DIM]

    def keys_of(start, width, hidden):
        if width == TAIL:
            d = _dot_nt(qis, kit_ref[0])
        else:
            d = _dot_nt(qis_past, past(kip_ref, start, width))
        wb = wb_ref[...]
        cols = []
        for dc in _lane_chunks(d):
            r = jnp.maximum(dc, 0.0) * wb
            cols.append(functools.reduce(jnp.add, [r[h * qb:(h + 1) * qb] for h in range(IDX_HEADS)]))
        sc = jnp.concatenate(cols, axis=1)
        sc = jnp.where(sc == 0.0, 0.0, sc)
        bits = lax.bitcast_convert_type(sc, I32)
        key = bits ^ ((bits >> 31) & 0x7FFFFFFF)
        return jnp.where(hidden, KEY_NEG_INF, key)

    bt0 = btail_ref[jnp.where(qbi == 0, 1, 0), 0]
    kxt_ref[...] = keys_of(tail_start, TAIL, bt0 < 0.5 * NEG)

    def score_body(j, carry):
        kxf_ref[j] = keys_of(_far_start(tail_start, j), SUP, False)
        return carry

    lax.fori_loop(0, n_far - 1, score_body, 0)

    @pl.when(n_far >= 1)
    def _():
        kxf_ref[n_far - 1] = keys_of(_far_start(tail_start, n_far - 1), SUP, fmask_ref[n_front] < 0.5 * NEG)

    col_t = lax.broadcasted_iota(I32, (qb, TAIL), 1)
    col_f = lax.broadcasted_iota(I32, (qb, SUP), 1)

    def count(pred_t, pred_f):
        def body(j, acc):
            hit = jnp.where(pred_f(kxf_ref[j], col_f + (_far_start(tail_start, j))), 1.0, 0.0)
            return acc + functools.reduce(jnp.add, _lane_chunks(hit))
        acc = functools.reduce(jnp.add, _lane_chunks(
            jnp.where(pred_t(kxt_ref[...], col_t + tail_start), 1.0, 0.0)))
        acc = lax.fori_loop(0, n_far, body, acc)
        return jnp.sum(acc, axis=1, keepdims=True)

    def count1(make_pred):
        return count(make_pred(TAIL), make_pred(SUP))

    wide = lambda a, w: jnp.broadcast_to(a, (qb, w))

    def ge(a):
        return lambda w: (lambda kx, idx, aw=wide(a, w): kx >= aw)

    def gt(a):
        return lambda w: (lambda kx, idx, aw=wide(a, w): kx > aw)

    def eq(a):
        return lambda w: (lambda kx, idx, aw=wide(a, w): kx == aw)

    def eq_before(a, c):
        return lambda w: (lambda kx, idx, aw=wide(a, w), cw=wide(c, w): (kx == aw) & (idx < cw))

    r_ref[...] = jnp.full((qb, 1), INT_MIN, I32)
    cnt_ref[...] = jnp.full((qb, 1), float(TAIL + SUP * 64), F32)

    def bit_cond(st):
        return jnp.logical_and(st[0] < 32, st[1] > 0.0)

    def bit_body(st):
        t = st[0]
        r = r_ref[...]
        cand = r + lax.shift_left(jnp.int32(1), 31 - t)
        cnt = count1(ge(cand))
        take = cnt >= topk
        r_ref[...] = jnp.where(take, cand, r)
        cnt_new = jnp.where(take, cnt, cnt_ref[...])
        cnt_ref[...] = cnt_new
        return t + 1, jnp.max(jnp.where(cnt_new == topk, 0.0, 1.0))

    lax.while_loop(bit_cond, bit_body, (jnp.int32(0), jnp.float32(1.0)))
    r1 = r_ref[...]

    need = topk - count1(gt(r1))
    n_eq = count1(eq(r1))
    cs_ref[...] = jnp.full((qb, 1), 2 ** idx_bits - 1, I32)
    overflow = jnp.where((n_eq > need) & (r1 > KEY_NEG_INF), 1.0, 0.0)

    @pl.when(jnp.max(overflow) > 0.0)
    def _():
        def tie_body(t, c):
            cand = c + lax.shift_left(jnp.int32(1), idx_bits - 1 - t)
            cnt = count1(eq_before(r1, cand))
            return jnp.where(cnt < need, cand, c)
        cs_ref[...] = lax.fori_loop(0, idx_bits, tie_body, jnp.zeros((qb, 1), I32))

    cs1 = cs_ref[...]

    m_ref[...] = jnp.full(m_ref.shape, NEG, F32)
    l_ref[...] = jnp.zeros(l_ref.shape, F32)
    acc_ref[...] = jnp.zeros(acc_ref.shape, F32)

    def sel_mask(kx, idx, w):
        sel = (kx > wide(r1, w)) | ((kx == wide(r1, w)) & (idx <= wide(cs1, w)))
        return jnp.where(sel & (kx > KEY_NEG_INF), 0.0, NEG)

    def attend(start, width, madd, bias, ccols):
        if width == TAIL:
            k, v = kat_ref[0], vat_ref[0]
        else:
            k, v = past(kap_ref, start, width), past(vap_ref, start, width)
        for p in range(npair):
            x = (_dot_nt(qas[p], k[:, p * LANES:(p + 1) * LANES]) * LOG2E).reshape(2, qb, width) + madd[None]
            if bias is not None:
                x = x + bias[2 * p:2 * p + 2]
            _attn_step(x.reshape(2 * qb, width), ccols[p], v[:, p * LANES:(p + 1) * LANES],
                       m_ref.at[p], l_ref.at[p], acc_ref.at[p])

    bt = btail_ref[jnp.where(qbi == 0, 1, 0)]
    attend(tail_start, TAIL, sel_mask(kxt_ref[...], col_t + tail_start, TAIL), bt, [0.0] * npair)

    ccols = [cfar_ref[p] for p in range(npair)]

    def far_body(j, carry):
        start = _far_start(tail_start, j)
        attend(start, SUP, sel_mask(kxf_ref[j], col_f + start, SUP), None, ccols)
        return carry

    lax.fori_loop(0, n_far, far_body, 0)

    for p in range(npair):
        o2 = acc_ref[p] / jnp.sum(l_ref[p], axis=1, keepdims=True)
        o = jnp.where(lane < HEAD_DIM, o2[0:qb], o2[qb:2 * qb])
        o_ref[0, :, p * LANES:(p + 1) * LANES] = o.astype(o_ref.dtype)


def _dsa_attn(qa, qi, misc_f32, ka, va, ki, btail, cfar, *, qb, qb0, topk):
    (kap, kat), (vap, vat), (kip, kit) = ka, va, ki
    b, t, _ = qa.shape
    p_len = kap.shape[1]
    s_tot = p_len + KB
    nq = t // qb
    assert nq == 1
    npair = N_HEADS_A // 2
    n_far_max = max(1, -(-(s_tot - TAIL) // SUP))
    assert n_far_max <= 64
    idx_bits = int(math.ceil(math.log2(s_tot))) + 1
    past_spec = lambda w: pl.BlockSpec((1, p_len, w), lambda bi, i: (bi, 0, 0),
                                       pipeline_mode=pl.Buffered(1))
    tail_spec = lambda w: pl.BlockSpec((1, TAIL, w), lambda bi, i: (bi, 0, 0))
    ccol = jnp.repeat(cfar.reshape(npair, 2), qb, axis=1).reshape(npair, 2 * qb, 1)
    kern = functools.partial(_dsa_kernel, qb=qb, qb0=qb0, topk=float(topk), idx_bits=idx_bits)
    return pl.pallas_call(
        kern,
        grid=(b, nq),
        in_specs=[
            pl.BlockSpec((2, N_HEADS_A, qb, TAIL), lambda bi, i: (0, 0, 0, 0)),
            pl.BlockSpec((npair, 2 * qb, 1), lambda bi, i: (0, 0, 0)),
            pl.BlockSpec((SUP // KB, 1, SUP), lambda bi, i: (0, 0, 0)),
            pl.BlockSpec((1, qb, WA), lambda bi, i: (bi, i, 0)),
            pl.BlockSpec((1, qb, WQI), lambda bi, i: (bi, i, 0)),
            pl.BlockSpec((1, qb, LANES), lambda bi, i: (bi, i, 0)),
            past_spec(WA), past_spec(WA), past_spec(IDX_DIM),
            tail_spec(WA), tail_spec(WA), tail_spec(LANES),
        ],
        out_specs=pl.BlockSpec((1, qb, WA), lambda bi, i: (bi, i, 0)),
        out_shape=jax.ShapeDtypeStruct((b, t, WA), BF16),
        scratch_shapes=[
            pltpu.VMEM((qb, TAIL), I32),
            pltpu.VMEM((n_far_max, qb, SUP), I32),
            pltpu.VMEM((IDX_HEADS * qb, LANES), F32),
            pltpu.VMEM((qb, 1), I32),
            pltpu.VMEM((qb, 1), F32),
            pltpu.VMEM((qb, 1), I32),
            pltpu.VMEM((npair, 2 * qb, 1), F32),
            pltpu.VMEM((npair, 2 * qb, LANES), F32),
            pltpu.VMEM((npair, 2 * qb, LANES), F32),
        ],
        compiler_params=_cparams(("parallel", "arbitrary")),
        name="dsa_attn",
    )(btail, ccol, _front_mask_rows(), qa, qi, misc_f32, kap, vap, kip, kat, vat, kit)


def _dsa_kernel_t(btail_ref, crow_ref, qat_ref, qit_ref, wit_ref, ka_ref, vat_ref, ki_ref, o_ref,
                  kxt_ref, kxf_ref, m_ref, l_ref, acc_ref, *, qb, qb0, topk, idx_bits):
    npair = N_HEADS_A // 2
    qbi = qb0 + pl.program_id(1)
    tail_start, n_far, n_front = _windows(qbi)
    row = lax.broadcasted_iota(I32, (LANES, qb), 0)

    qit = qit_ref[0]
    zpad = jnp.zeros((LANES - IDX_DIM, qb), BF16)
    qist = jnp.concatenate(
        [jnp.concatenate([qit[h * IDX_DIM:(h + 1) * IDX_DIM].astype(BF16), zpad], axis=0)
         for h in range(IDX_HEADS)], axis=1)
    wit = wit_ref[0] * (IDX_HEADS ** -0.5 * IDX_DIM ** -0.5)
    qat = qat_ref[0]
    qast = []
    for p in range(npair):
        part = qat[p * LANES:(p + 1) * LANES]
        qast.append(jnp.concatenate(
            [jnp.where(row < HEAD_DIM, part, jnp.zeros_like(part)),
             jnp.where(row < HEAD_DIM, jnp.zeros_like(part), part)], axis=1))

    def ki_window(start, width):
        return ki_ref[0, pl.ds(pl.multiple_of(start, KB), width), :]

    def keys_of(start, width, hidden):
        d = _dot(ki_window(start, width), qist)
        sc = functools.reduce(jnp.add, [
            jnp.maximum(d[:, h * qb:(h + 1) * qb], 0.0) * wit[h:h + 1] for h in range(IDX_HEADS)])
        sc = jnp.where(sc == 0.0, 0.0, sc)
        bits = lax.bitcast_convert_type(sc, I32)
        key = bits ^ ((bits >> 31) & 0x7FFFFFFF)
        return key if hidden is None else jnp.where(hidden, KEY_NEG_INF, key)

    variant = jnp.where(qbi == 0, 1, 0)
    kxt_ref[...] = keys_of(tail_start, TAIL, btail_ref[variant, 0, :, 0:qb] < 0.5 * NEG)

    def score_body(j, carry):
        kxf_ref[j] = keys_of(_far_start(tail_start, j), SUP, None)
        return carry

    lax.fori_loop(0, n_far - 1, score_body, 0)

    @pl.when(n_far >= 1)
    def _():
        kxf_ref[n_far - 1] = keys_of(_far_start(tail_start, n_far - 1), SUP,
                                     lax.broadcasted_iota(I32, (SUP, qb), 0) >= SUP - n_front * KB)

    row_t = lax.broadcasted_iota(I32, (TAIL, qb), 0)
    row_f = lax.broadcasted_iota(I32, (SUP, qb), 0)

    def hits(pred, kx, idx):
        return _fold_rows(jnp.add, jnp.where(pred(kx, idx), 1.0, 0.0))

    def count(pred):
        def body(j, acc):
            return acc + hits(pred, kxf_ref[j], row_f + (_far_start(tail_start, j)))
        acc = lax.fori_loop(0, n_far, body, hits(pred, kxt_ref[...], row_t + tail_start))
        return jnp.sum(acc, axis=0, keepdims=True)

    def bit_body(t, r):
        cand = r + lax.shift_left(jnp.int32(1), 31 - t)
        return jnp.where(count(lambda kx, idx: kx >= cand) >= topk, cand, r)

    r1 = lax.fori_loop(0, 32, bit_body, jnp.full((1, qb), INT_MIN, I32))

    n_ge = count(lambda kx, idx: kx >= r1)
    overflow = jnp.where((n_ge > topk) & (r1 > KEY_NEG_INF), 1.0, 0.0)

    def tie_search(_):
        need = topk - count(lambda kx, idx: kx > r1)

        def tie_body(t, c):
            cand = c + lax.shift_left(jnp.int32(1), idx_bits - 1 - t)
            cnt = count(lambda kx, idx: (kx == r1) & (idx < cand))
            return jnp.where(cnt < need, cand, c)
        return lax.fori_loop(0, idx_bits, tie_body, jnp.zeros((1, qb), I32))

    cs1 = lax.cond(jnp.max(overflow) > 0.0, tie_search,
                   lambda _: jnp.full((1, qb), 2 ** idx_bits - 1, I32), 0)

    m_ref[...] = jnp.full(m_ref.shape, NEG, F32)
    l_ref[...] = jnp.zeros(l_ref.shape, F32)
    acc_ref[...] = jnp.zeros(acc_ref.shape, F32)

    def sel_mask(kx, idx):
        sel = (kx > r1) | ((kx == r1) & (idx <= cs1))
        madd = jnp.where(sel & (kx > KEY_NEG_INF), 0.0, NEG)
        return jnp.concatenate([madd, madd], axis=1)

    def attend(starts, nblk, madds, bias, crows):
        wins = [_window_t(ka_ref, vat_ref, start, nblk) for start in starts]
        raws = [[_dot(k[:, p * LANES:(p + 1) * LANES], qast[p]) for p in range(npair)] for k, _ in wins]
        for (_, vt), raw, madd2 in zip(wins, raws, madds):
            for p in range(npair):
                x = raw[p] * LOG2E + madd2
                if bias is not None:
                    x = x + bias[:, 2 * p * qb:2 * (p + 1) * qb]
                _attn_step_t(x, crows[p], vt[p * LANES:(p + 1) * LANES],
                             m_ref.at[p], l_ref.at[p], acc_ref.at[p])

    attend([tail_start], TAIL // KB, [sel_mask(kxt_ref[...], row_t + tail_start)],
           btail_ref[variant, 0], [0.0] * npair)

    crows = [crow_ref[p] for p in range(npair)]
    far_start = lambda j: _far_start(tail_start, j)

    def attend_far(js):
        attend([far_start(j) for j in js], SUP // KB,
               [sel_mask(kxf_ref[j], row_f + far_start(j)) for j in js], None, crows)

    def far_body(jj, carry):
        attend_far([2 * jj, 2 * jj + 1])
        return carry

    lax.fori_loop(0, n_far // 2, far_body, 0)

    @pl.when(n_far % 2 == 1)
    def _():
        attend_far([n_far - 1])

    for p in range(npair):
        o2 = acc_ref[p] / jnp.sum(l_ref[p], axis=0, keepdims=True)
        o = jnp.where(row < HEAD_DIM, o2[:, 0:qb], o2[:, qb:2 * qb])
        o_ref[0, p * LANES:(p + 1) * LANES, :] = o.astype(o_ref.dtype)


def _dsa_attn_t(qa, qi, misc_f32, ka, va, ki, btail, cfar, *, qb, qb0, topk):
    b, t, _ = qa.shape
    s_tot = ka.shape[1]
    nq = t // qb
    npair = N_HEADS_A // 2
    n_far_max = max(1, -(-(s_tot - TAIL) // SUP))
    idx_bits = int(math.ceil(math.log2(s_tot))) + 1
    qat = qa.transpose(0, 2, 1)
    qit = qi.transpose(0, 2, 1)
    wit = misc_f32[:, :, IDX_DIM:IDX_DIM + IDX_HEADS].transpose(0, 2, 1)
    vat = _blocked_t(va)
    crow = jnp.repeat(cfar.reshape(npair, 2), qb, axis=1).reshape(npair, 1, 2 * qb)
    bt = btail.transpose(0, 3, 1, 2).reshape(2, 1, TAIL, N_HEADS_A * qb)
    kern = functools.partial(_dsa_kernel_t, qb=qb, qb0=qb0, topk=float(topk), idx_bits=idx_bits)
    out_t = pl.pallas_call(
        kern,
        grid=(b, nq),
        in_specs=[
            pl.BlockSpec((2, 1, TAIL, N_HEADS_A * qb), lambda bi, i: (0, 0, 0, 0)),
            pl.BlockSpec((npair, 1, 2 * qb), lambda bi, i: (0, 0, 0)),
            pl.BlockSpec((1, WA, qb), lambda bi, i: (bi, 0, i)),
            pl.BlockSpec((1, WQI, qb), lambda bi, i: (bi, 0, i)),
            pl.BlockSpec((1, IDX_HEADS, qb), lambda bi, i: (bi, 0, i)),
            _key_spec(s_tot, WA, lambda bi, i: (bi, 0, 0)),
            pl.BlockSpec((1, s_tot // KB, WA, KB), lambda bi, i: (bi, 0, 0, 0),
                         pipeline_mode=pl.Buffered(1)),
            _key_spec(s_tot, LANES, lambda bi, i: (bi, 0, 0)),
        ],
        out_specs=pl.BlockSpec((1, WA, qb), lambda bi, i: (bi, 0, i)),
        out_shape=jax.ShapeDtypeStruct((b, WA, t), BF16),
        scratch_shapes=[
            pltpu.VMEM((TAIL, qb), I32),
            pltpu.VMEM((n_far_max, SUP, qb), I32),
            pltpu.VMEM((npair, 1, 2 * qb), F32),
            pltpu.VMEM((npair, 8, 2 * qb), F32),
            pltpu.VMEM((npair, LANES, 2 * qb), F32),
        ],
        compiler_params=_cparams(("parallel", "arbitrary")),
        name="dsa_attn_t",
    )(bt, crow, qat, qit, wit, ka, vat, ki)
    return out_t.transpose(0, 2, 1)


def _outproj_kernel(oa_ref, ob_ref, oc_ref, wa_ref, wb_ref, wc_ref, g_ref, x_ref, y_ref):
    mix = _dot(oa_ref[...], wa_ref[...]) + _dot(ob_ref[...], wb_ref[...]) + _dot(oc_ref[...], wc_ref[...])
    ms = jnp.mean(mix * mix, axis=-1, keepdims=True)
    y_ref[...] = x_ref[...] + mix * lax.rsqrt(ms + EPS) * g_ref[...]


def _outproj(oa, ob, oc, w_out, g, x2d):
    m, d = x2d.shape
    tm = min(512, m)
    w = w_out.astype(BF16)
    row = lambda i: (i, 0)
    const = lambda i: (0, 0)
    return pl.pallas_call(
        _outproj_kernel,
        grid=(m // tm,),
        in_specs=[pl.BlockSpec((tm, WA), row), pl.BlockSpec((tm, WB), row), pl.BlockSpec((tm, WC), row),
                  pl.BlockSpec((WA, d), const), pl.BlockSpec((WB, d), const), pl.BlockSpec((WC, d), const),
                  pl.BlockSpec((1, d), const), pl.BlockSpec((tm, d), row)],
        out_specs=pl.BlockSpec((tm, d), row),
        out_shape=jax.ShapeDtypeStruct((m, d), F32),
        compiler_params=_cparams(("parallel",)),
        name="outproj",
    )(oa, ob, oc, w[:WA], w[WA:WA + WB], w[WA + WB:], g.reshape(1, d), x2d)


HALO = 16


def _ffn_kernel(x_ref, xh_ref, stg_ref, stv_ref, gpre_ref, gpost_ref, wg_ref, wv_ref, cwg_ref,
                cwv_ref, cbg_ref, cbv_ref, wd_ref, y_ref, convg_ref, convv_ref,
                h_ref, ug_ref, uv_ref, f_ref, *, tm, tiles_per_seq):
    i = pl.program_id(0)
    f = pl.program_id(1)
    first_of_seq = (i % tiles_per_seq) == 0

    def norm(x):
        ms = jnp.mean(x * x, axis=-1, keepdims=True)
        return (x * lax.rsqrt(ms + EPS) * gpre_ref[...]).astype(BF16)

    @pl.when(f == 0)
    def _():
        h_ref[0:HALO, :] = norm(xh_ref[...])
        h_ref[HALO:, :] = norm(x_ref[...])
        f_ref[...] = jnp.zeros(f_ref.shape, F32)

    h = h_ref[...]
    ug_ref[...] = _dot(h, wg_ref[...])
    uv_ref[...] = _dot(h, wv_ref[...])

    @pl.when(first_of_seq)
    def _():
        ug_ref[0:HALO, :] = stg_ref[0]
        uv_ref[0:HALO, :] = stv_ref[0]

    def conv(u_ref, cw_ref, cb_ref):
        u = u_ref[...]
        u1 = pltpu.roll(u, 1, 0)[HALO:, :]
        u2 = pltpu.roll(u, 2, 0)[HALO:, :]
        cw = cw_ref[...]
        return cb_ref[...] + (cw[0:1] * u2 + cw[1:2] * u1 + cw[2:3] * u[HALO:, :])

    gate = conv(ug_ref, cwg_ref, cbg_ref)
    val = conv(uv_ref, cwv_ref, cbv_ref)
    c0 = math.sqrt(2.0 / math.pi)
    gelu = 0.5 * gate * (1.0 + jnp.tanh(c0 * (gate + 0.044715 * (gate * gate * gate))))
    f_ref[...] += _dot((gelu * val).astype(BF16), wd_ref[...])

    convg_ref[0] = ug_ref[tm:, :]
    convv_ref[0] = uv_ref[tm:, :]

    @pl.when(f == pl.num_programs(1) - 1)
    def _():
        ff = f_ref[...]
        ms = jnp.mean(ff * ff, axis=-1, keepdims=True)
        y_ref[...] = x_ref[...] + ff * lax.rsqrt(ms + EPS) * gpost_ref[...]


def _ffn(x2d, state, g_pre, g_post, w_up, conv_w, conv_b, w_down, *, seq):
    m, d = x2d.shape
    b = m // seq
    d_ff = w_down.shape[0]
    tn = 1408
    assert d_ff % tn == 0 and tn % LANES == 0
    nf = d_ff // tn
    tm = min(512, seq)
    assert seq % tm == 0 and tm % HALO == 0
    tiles_per_seq = seq // tm
    n_tiles = m // tm
    st = jnp.pad(state.astype(F32), ((0, 0), (HALO - (CONV_W - 1), 0), (0, 0)))
    wu = w_up.astype(BF16)
    wd = w_down.astype(BF16)
    kern = functools.partial(_ffn_kernel, tm=tm, tiles_per_seq=tiles_per_seq)
    halo_blocks = tm // HALO
    y, conv_g, conv_v = pl.pallas_call(
        kern,
        grid=(n_tiles, nf),
        in_specs=[
            pl.BlockSpec((tm, d), lambda i, f: (i, 0)),
            pl.BlockSpec((HALO, d), lambda i, f: (jnp.maximum(i * halo_blocks - 1, 0), 0)),
            pl.BlockSpec((1, HALO, tn), lambda i, f: (i // tiles_per_seq, 0, f)),
            pl.BlockSpec((1, HALO, tn), lambda i, f: (i // tiles_per_seq, 0, nf + f)),
            pl.BlockSpec((1, d), lambda i, f: (0, 0)),
            pl.BlockSpec((1, d), lambda i, f: (0, 0)),
            pl.BlockSpec((d, tn), lambda i, f: (0, f)),
            pl.BlockSpec((d, tn), lambda i, f: (0, nf + f)),
            pl.BlockSpec((CONV_W, tn), lambda i, f: (0, f)),
            pl.BlockSpec((CONV_W, tn), lambda i, f: (0, nf + f)),
            pl.BlockSpec((1, tn), lambda i, f: (0, f)),
            pl.BlockSpec((1, tn), lambda i, f: (0, nf + f)),
            pl.BlockSpec((tn, d), lambda i, f: (f, 0)),
        ],
        out_specs=[pl.BlockSpec((tm, d), lambda i, f: (i, 0)),
                   pl.BlockSpec((1, HALO, tn), lambda i, f: (i, 0, f)),
                   pl.BlockSpec((1, HALO, tn), lambda i, f: (i, 0, f))],
        out_shape=[jax.ShapeDtypeStruct((m, d), F32),
                   jax.ShapeDtypeStruct((n_tiles, HALO, d_ff), F32),
                   jax.ShapeDtypeStruct((n_tiles, HALO, d_ff), F32)],
        scratch_shapes=[pltpu.VMEM((HALO + tm, d), BF16),
                        pltpu.VMEM((HALO + tm, tn), F32),
                        pltpu.VMEM((HALO + tm, tn), F32),
                        pltpu.VMEM((tm, d), F32)],
        compiler_params=_cparams(("parallel", "arbitrary")),
        name="ffn",
    )(x2d, x2d, st, st, g_pre.reshape(1, d), g_post.reshape(1, d), wu, wu,
      conv_w, conv_w, conv_b.reshape(1, -1), conv_b.reshape(1, -1), wd)
    last = lambda c: c.reshape(b, tiles_per_seq, HALO, d_ff)[:, -1, HALO - (CONV_W - 1):]
    return y, jnp.concatenate([last(conv_g), last(conv_v)], axis=-1)


def _lambda_init(l):
    return 0.8 - 0.6 * math.exp(-0.3 * l)


def _past_and_tail(past, new):
    b, t, w = new.shape
    p = past.shape[1]
    past = past.reshape(b, p, -1)
    last = past[:, p - (TAIL - KB):].astype(BF16)
    if last.shape[2] < w:
        last = jnp.pad(last, ((0, 0), (0, 0), (0, w - last.shape[2])))
    return past, jnp.concatenate([last, new, jnp.zeros((b, KB - t, w), BF16)], axis=1)


def _layer(x, past, l, prm):
    b, t, d = x.shape
    p_len = 0 if past is None else past["k_a"].shape[1]
    s = p_len + t
    qb = min(KB, t)
    assert p_len % KB == 0 and t % qb == 0 and (qb == KB or t == qb)
    qb0 = p_len // KB
    s_pad = -(-s // KB) * KB
    assert s_pad >= max(TAIL, SUP)
    n_valid = s - (s_pad - KB)
    topk = min(TOPK_MAX, s // 4)
    m = b * t

    pr = _proj(x.reshape(m, d), prm["g_pre_mix"][l], _pad_w_in(prm["w_in"][l]))
    r3 = lambda a: a.reshape(b, t, a.shape[-1])

    btail_a, cfar_a = _bias_tiles(prm["rel_bias"][:, :N_HEADS_A], qb, n_valid)
    btail_c, cfar_c = _bias_tiles(prm["rel_bias"][:, N_HEADS_A:], qb, n_valid)

    lamv = jnp.pad(jnp.stack([prm["lambda_q1"][l], prm["lambda_k1"][l],
                              prm["lambda_q2"][l], prm["lambda_k2"][l]]).astype(F32),
                   ((0, 0), (0, LANES - DC_HALF)))
    gain2 = jnp.tile(prm["subln_gain"][l].astype(F32), 2).reshape(1, LANES)
    if past is None:
        assert qb == KB
        o_a = _dsa_attn_t(r3(pr["qa_bf16"]), r3(pr["qi_f32"]), r3(pr["misc_f32"]),
                          r3(pr["ka_bf16"]), r3(pr["va_bf16"]), r3(pr["misc_bf16"]),
                          btail_a, cfar_a, qb=qb, qb0=qb0, topk=topk)
        o_b = _sb_attn(r3(pr["qb_bf16"]), r3(pr["kb_bf16"]), r3(pr["vb_bf16"]), qb=qb, qb0=qb0)
        o_c = _diff_attn_t(r3(pr["qc_bf16"]), r3(pr["kc_bf16"]), r3(pr["vc_bf16"]),
                           btail_c, cfar_c, lamv, gain2, qb=qb, qb0=qb0, lam0=_lambda_init(l))
    else:
        assert t == qb and p_len >= max(TAIL - KB, SUP)
        o_a = _dsa_attn(r3(pr["qa_bf16"]), r3(pr["qi_f32"]), r3(pr["misc_f32"]),
                        _past_and_tail(past["k_a"], r3(pr["ka_bf16"])),
                        _past_and_tail(past["v_a"], r3(pr["va_bf16"])),
                        _past_and_tail(past["k_i"], r3(pr["misc_bf16"])),
                        btail_a, cfar_a, qb=qb, qb0=qb0, topk=topk)
        diag = lambda new: jnp.concatenate([new, jnp.zeros((b, KB - t, new.shape[2]), BF16)], axis=1)
        o_b = _sb_attn(r3(pr["qb_bf16"]), past["k_b"].reshape(b, p_len, WB),
                       past["v_b"].reshape(b, p_len, WB),
                       (diag(r3(pr["kb_bf16"])), diag(r3(pr["vb_bf16"]))), qb=qb, qb0=qb0)
        o_c = _diff_attn(r3(pr["qc_bf16"]),
                         _past_and_tail(past["k_c"], r3(pr["kc_bf16"])),
                         _past_and_tail(past["v_c"], r3(pr["vc_bf16"])),
                         btail_c, cfar_c, lamv, gain2, qb=qb, qb0=qb0, lam0=_lambda_init(l))

    x1 = _outproj(o_a.reshape(m, WA), o_b.reshape(m, WB), o_c.reshape(m, WC),
                  prm["w_out"][l], prm["g_post_mix"][l], x.reshape(m, d))
    state = (jnp.zeros((b, CONV_W - 1, prm["w_up"].shape[2]), F32) if past is None
             else past["conv"])
    x2, new_conv = _ffn(x1, state, prm["g_pre_ffn"][l], prm["g_post_ffn"][l], prm["w_up"][l],
                        prm["conv_w"][l], prm["conv_b"][l], prm["w_down"][l], seq=t)
    new = (pr["ka_f32"], pr["va_f32"], pr["misc_f32"][:, :IDX_DIM], pr["kb_f32"], pr["vb_f32"],
           pr["kc_f32"], pr["vc_f32"], new_conv)
    return x2.reshape(b, t, d), new


def _run_trunk(x, caches, prm):
    depth = prm["w_in"].shape[0]
    b, t, _ = x.shape
    news = []
    for l in range(depth):
        past = None if caches is None else {k: v[l] for k, v in caches.items()}
        x, new = _layer(x, past, l, prm)
        news.append(new)
    stacked = [jnp.stack([n[i] for n in news]) for i in range(len(news[0]))]
    heads = (N_HEADS_A, N_HEADS_A, None, N_HEADS_B, N_HEADS_B, N_HEADS_C, N_HEADS_C)
    outs = [s.reshape(depth, b, t, IDX_DIM) if h is None else s.reshape(depth, b, t, h, HEAD_DIM)
            for s, h in zip(stacked[:-1], heads)]
    return x, outs + [stacked[-1]]


def kernel(x_prompt, x_sample, cache_k_a, cache_v_a, cache_idx_k, cache_k_b, cache_v_b, cache_k_c,
           cache_v_c, state_ffn_conv, w_in, w_out, rel_bias, lambda_q1, lambda_k1, lambda_q2,
           lambda_k2, subln_gain, g_pre_mix, g_post_mix, g_pre_ffn, g_post_ffn, w_up, conv_w,
           conv_b, w_down):
    prm = dict(w_in=w_in, w_out=w_out, rel_bias=rel_bias, lambda_q1=lambda_q1, lambda_k1=lambda_k1,
               lambda_q2=lambda_q2, lambda_k2=lambda_k2, subln_gain=subln_gain, g_pre_mix=g_pre_mix,
               g_post_mix=g_post_mix, g_pre_ffn=g_pre_ffn, g_post_ffn=g_post_ffn, w_up=w_up,
               conv_w=conv_w, conv_b=conv_b, w_down=w_down)
    y_prompt, p_new = _run_trunk(x_prompt, None, prm)
    caches = dict(k_a=cache_k_a, v_a=cache_v_a, k_i=cache_idx_k, k_b=cache_k_b, v_b=cache_v_b,
                  k_c=cache_k_c, v_c=cache_v_c, conv=state_ffn_conv)
    y_sample, s_new = _run_trunk(x_sample, caches, prm)
    return (y_prompt, y_sample, *p_new, *s_new)
```

```python
import functools
import math

import numpy as np
import jax
import jax.numpy as jnp
from jax import lax
from jax.experimental import pallas as pl
from jax.experimental.pallas import tpu as pltpu

F32 = jnp.float32
BF16 = jnp.bfloat16
I32 = jnp.int32

LANES = 128
VMEM_LIMIT = 56 * 1024 * 1024

CHUNK = 64
HEAD_DIM = 64
N_HEADS_A = 6
N_HEADS_B = 6
N_HEADS_C = 4
DC_HALF = HEAD_DIM // 2
IDX_HEADS = 8
IDX_DIM = 32
TOPK_MAX = 256
NUM_BUCKETS = 32
MAX_DISTANCE = 128
CONV_W = 3
EPS = 1e-6

WA = N_HEADS_A * HEAD_DIM
WB = N_HEADS_B * HEAD_DIM
WC = N_HEADS_C * HEAD_DIM
WQI = IDX_HEADS * IDX_DIM

NEG = -1e30
INT_MIN = -2 ** 31
KEY_NEG_INF = int(np.int32(np.float32(-np.inf).view(np.int32)) ^ np.int32(0x7FFFFFFF))
EXP_ZERO = -104.0
LOG2E = math.log2(math.e)

KB = 128
TAIL = 2 * KB
SUP = 8 * KB
assert TAIL - KB >= MAX_DISTANCE
NT_DIMS = (((1,), (1,)), ((), ()))


def _cparams(sem):
    return pltpu.CompilerParams(dimension_semantics=sem, vmem_limit_bytes=VMEM_LIMIT)


def _dot_nt(a, b):
    return lax.dot_general(a, b, NT_DIMS, preferred_element_type=F32)


def _dot(a, b):
    return jnp.dot(a, b, preferred_element_type=F32)


_PROJ_GROUPS = (
    ("qa", 0, WA, HEAD_DIM ** -0.5, False, True),
    ("ka", 384, WA, 1.0, True, True),
    ("va", 768, WA, 1.0, True, True),
    ("qi", 1152, WQI, 1.0, True, False),
    ("misc", 1408, LANES, 1.0, True, True),
    ("qb", 1536, WB, HEAD_DIM ** -0.5, False, True),
    ("kb", 1920, WB, 1.0, True, True),
    ("vb", 2304, WB, 1.0, True, True),
    ("qc", 2688, WC, 1.0, False, True),
    ("kc", 2944, WC, 1.0, True, True),
    ("vc", 3200, WC, 1.0, True, True),
)
_PROJ_NPAD = 3456


def _proj_out_names():
    names = []
    for name, _, _, _, f32o, bf16o in _PROJ_GROUPS:
        if f32o:
            names.append(name + "_f32")
        if bf16o:
            names.append(name + "_bf16")
    return names


def _proj_kernel(x_ref, g_ref, w_ref, *out_refs):
    x = x_ref[...]
    ms = jnp.mean(x * x, axis=-1, keepdims=True)
    h = (x * lax.rsqrt(ms + EPS) * g_ref[...]).astype(BF16)
    k = 0
    for _, off, width, scale, f32o, bf16o in _PROJ_GROUPS:
        r = _dot(h, w_ref[:, off:off + width])
        if f32o:
            out_refs[k][...] = r
            k += 1
        if bf16o:
            out_refs[k][...] = (r * scale).astype(BF16) if scale != 1.0 else r.astype(BF16)
            k += 1


def _pad_w_in(w):
    d = w.shape[0]
    misc = jnp.concatenate([w[:, 1416:1448], w[:, 1408:1416],
                            jnp.zeros((d, LANES - IDX_DIM - IDX_HEADS), w.dtype)], axis=1)
    return jnp.concatenate([w[:, 0:1408], misc, w[:, 1448:]], axis=1).astype(BF16)


def _proj(x2d, g, w_pad):
    m, d = x2d.shape
    tm = min(512, m)
    assert m % tm == 0
    out_shapes, out_specs = [], []
    for _, _, width, _, f32o, bf16o in _PROJ_GROUPS:
        for want, dt in ((f32o, F32), (bf16o, BF16)):
            if want:
                out_shapes.append(jax.ShapeDtypeStruct((m, width), dt))
                out_specs.append(pl.BlockSpec((tm, width), lambda i: (i, 0)))
    outs = pl.pallas_call(
        _proj_kernel,
        grid=(m // tm,),
        in_specs=[pl.BlockSpec((tm, d), lambda i: (i, 0)),
                  pl.BlockSpec((1, d), lambda i: (0, 0)),
                  pl.BlockSpec((d, _PROJ_NPAD), lambda i: (0, 0))],
        out_specs=out_specs,
        out_shape=out_shapes,
        compiler_params=_cparams(("parallel",)),
        name="proj",
    )(x2d, g.reshape(1, d), w_pad)
    return dict(zip(_proj_out_names(), outs))


def _t5_bucket(rel):
    half = NUM_BUCKETS // 2
    max_exact = half // 2
    ret = jnp.where(rel > 0, half, 0)
    n = jnp.abs(rel)
    nf = jnp.maximum(n, 1).astype(F32)
    large = max_exact + (jnp.log(nf / max_exact) / math.log(MAX_DISTANCE / max_exact)
                         * (half - max_exact)).astype(I32)
    large = jnp.minimum(large, half - 1)
    return ret + jnp.where(n < max_exact, n, large)


def _bias_tiles(tab, qb, n_valid):
    qq = jnp.arange(qb, dtype=I32)[:, None]
    kk = jnp.arange(TAIL, dtype=I32)[None, :] - (TAIL - KB)
    tile = tab[_t5_bucket(kk - qq)].astype(F32).transpose(2, 0, 1) * LOG2E
    vis = ((kk // CHUNK) <= (qq // CHUNK)) & (kk < n_valid)
    tile = jnp.where(vis[None], tile, NEG)
    tile1 = jnp.concatenate([tile[..., TAIL - KB:], jnp.full(tile.shape[:2] + (TAIL - KB,), NEG, F32)],
                            axis=-1)
    far = tab[_t5_bucket(jnp.array(-MAX_DISTANCE, I32))].astype(F32) * LOG2E
    return jnp.stack([tile, tile1]), far


def _front_mask_rows():
    col = jnp.arange(SUP, dtype=I32)[None, None, :]
    v = jnp.arange(SUP // KB, dtype=I32)[:, None, None]
    return jnp.where(col >= SUP - v * KB, NEG, 0.0).astype(F32)


def _windows(qbi):
    tail_start = jnp.maximum((qbi + 1) * KB - TAIL, 0)
    n_far = (tail_start + SUP - 1) // SUP
    n_front = (n_far * SUP - tail_start) // KB
    return tail_start, n_far, n_front


def _far_start(tail_start, j):
    return jnp.maximum(tail_start - SUP * (j + 1), 0)


def _lane_chunks(x):
    return [x[:, c * LANES:(c + 1) * LANES] for c in range(x.shape[1] // LANES)]


def _attn_step(x, ccol, v, m_ref, l_ref, acc_ref):
    chunks = _lane_chunks(x)
    mx = functools.reduce(jnp.maximum, chunks)
    m_old = m_ref[...]
    m_new = jnp.maximum(m_old, jnp.max(mx, axis=1, keepdims=True) + ccol)
    p = jnp.exp2(x + (ccol - m_new))
    alpha = jnp.exp2(m_old - m_new)
    l_ref[...] = alpha * l_ref[...] + functools.reduce(jnp.add, _lane_chunks(p))
    acc_ref[...] = alpha * acc_ref[...] + _dot(p.astype(BF16), v)
    m_ref[...] = m_new


FOLD_CHAINS = 4
DIFF_GROUPS = 1


def _fold_rows(op, x):
    w, r = x.shape
    x3 = x.reshape(w // 8, 8, r)
    chains = [x3[i] for i in range(FOLD_CHAINS)]
    for i in range(FOLD_CHAINS, w // 8):
        chains[i % FOLD_CHAINS] = op(chains[i % FOLD_CHAINS], x3[i])
    return functools.reduce(op, chains)


def _attn_step_t(x, crow, vt, m_ref, l_ref, acc_ref):
    m_old = m_ref[...]
    m_new = jnp.maximum(m_old, jnp.max(_fold_rows(jnp.maximum, x), axis=0, keepdims=True) + crow)
    p = jnp.exp2(x + (crow - m_new))
    alpha = jnp.exp2(m_old - m_new)
    l_ref[...] = alpha * l_ref[...] + _fold_rows(jnp.add, p)
    acc_ref[...] = alpha * acc_ref[...] + _dot(vt, p.astype(BF16))
    m_ref[...] = m_new


def _front_mask_t(n_front, shape):
    return jnp.where(lax.broadcasted_iota(I32, shape, 0) >= SUP - n_front * KB, NEG, 0.0)


def _blocked_t(v):
    b, s, w = v.shape
    return v.reshape(b, s // KB, KB, w).transpose(0, 1, 3, 2)


def _window_t(k_ref, vt_ref, start, nblk):
    off = pl.multiple_of(start, KB)
    k = k_ref[0, pl.ds(off, nblk * KB), :]
    v4 = vt_ref[0, pl.ds((start) // KB, nblk)]
    vt = jnp.concatenate([v4[n] for n in range(nblk)], axis=1)
    return k, vt


def _key_spec(s_tot, width, index_map):
    return pl.BlockSpec((1, s_tot, width), index_map, pipeline_mode=pl.Buffered(1))


def _diff_kernel(lam_ref, gain_ref, btail_ref, cfar_ref, fmask_ref, q_ref, kp_ref, vp_ref,
                 kt_ref, vt_ref, o_ref, m_ref, l_ref, acc_ref, *, qb, qb0, lam0):
    qbi = qb0 + pl.program_id(2)
    tail_start, n_far, n_front = _windows(qbi)
    q = q_ref[0]
    lane = lax.broadcasted_iota(I32, (qb, LANES), 1)
    qs = jnp.concatenate(
        [jnp.where((lane >= DC_HALF * c) & (lane < DC_HALF * (c + 1)), q, jnp.zeros_like(q))
         for c in range(4)], axis=0)
    m_ref[...] = jnp.full(m_ref.shape, NEG, F32)
    l_ref[...] = jnp.zeros(l_ref.shape, F32)
    acc_ref[...] = jnp.zeros(acc_ref.shape, F32)
    a = DC_HALF ** -0.5 * LOG2E

    def window(start, width):
        off = pl.multiple_of(start, KB)
        return (kp_ref[0, pl.ds(off, width), :].astype(BF16),
                vp_ref[0, pl.ds(off, width), :].astype(BF16))

    k, v = kt_ref[0], vt_ref[0]
    bt = btail_ref[jnp.where(qbi == 0, 1, 0)]
    x = (_dot_nt(qs, k) * a).reshape(2, 2, qb, TAIL) + bt[:, None]
    _attn_step(x.reshape(4 * qb, TAIL), 0.0, v, m_ref, l_ref, acc_ref)

    ccol = cfar_ref[0]

    def far_body(j, carry):
        k, v = window(_far_start(tail_start, j), SUP)
        _attn_step(_dot_nt(qs, k) * a, ccol, v, m_ref, l_ref, acc_ref)
        return carry

    lax.fori_loop(0, n_far - 1, far_body, 0)

    @pl.when(n_far >= 1)
    def _():
        k, v = window(_far_start(tail_start, n_far - 1), SUP)
        _attn_step(_dot_nt(qs, k) * a + fmask_ref[n_front], ccol, v, m_ref, l_ref, acc_ref)

    lam_v = lam_ref[...]
    s1 = jnp.sum(lam_v[0:1] * lam_v[1:2], axis=1, keepdims=True)
    s2 = jnp.sum(lam_v[2:3] * lam_v[3:4], axis=1, keepdims=True)
    lam = jnp.exp(s1) - jnp.exp(s2) + lam0
    o4 = acc_ref[...] / jnp.sum(l_ref[...], axis=1, keepdims=True)
    o0 = o4[0:qb] - lam * o4[qb:2 * qb]
    o1 = o4[2 * qb:3 * qb] - lam * o4[3 * qb:4 * qb]
    first = lane < HEAD_DIM
    o = jnp.where(first, o0, o1)
    sq = o * o
    ms0 = jnp.sum(jnp.where(first, sq, 0.0), axis=1, keepdims=True)
    ms1 = jnp.sum(jnp.where(first, 0.0, sq), axis=1, keepdims=True)
    ms = jnp.where(first, ms0, ms1) * (1.0 / HEAD_DIM)
    y = o * lax.rsqrt(ms + EPS) * gain_ref[...]
    o_ref[0] = (y * (1.0 - lam0)).astype(o_ref.dtype)


def _diff_attn(q, k, v, btail, cfar, lamv, gain2, *, qb, qb0, lam0):
    (kp, kt), (vp, vt) = k, v
    b, t, _ = q.shape
    p_len = kp.shape[1]
    nq = t // qb
    assert nq == 1
    npair = N_HEADS_C // 2
    past_spec = pl.BlockSpec((1, p_len, LANES), lambda bi, p, i: (bi, 0, p), pipeline_mode=pl.Buffered(1))
    tail_spec = pl.BlockSpec((1, TAIL, LANES), lambda bi, p, i: (bi, 0, p))
    ccol = jnp.repeat(cfar.reshape(npair, 2), 2 * qb, axis=1).reshape(npair, 4 * qb, 1)
    btail = btail.reshape(2, npair, 2, qb, TAIL)
    kern = functools.partial(_diff_kernel, qb=qb, qb0=qb0, lam0=lam0)
    return pl.pallas_call(
        kern,
        grid=(b, npair, nq),
        in_specs=[
            pl.BlockSpec((4, LANES), lambda bi, p, i: (0, 0)),
            pl.BlockSpec((1, LANES), lambda bi, p, i: (0, 0)),
            pl.BlockSpec((2, None, 2, qb, TAIL), lambda bi, p, i: (0, p, 0, 0, 0)),
            pl.BlockSpec((1, 4 * qb, 1), lambda bi, p, i: (p, 0, 0)),
            pl.BlockSpec((SUP // KB, 1, SUP), lambda bi, p, i: (0, 0, 0)),
            pl.BlockSpec((1, qb, LANES), lambda bi, p, i: (bi, i, p)),
            past_spec, past_spec, tail_spec, tail_spec,
        ],
        out_specs=pl.BlockSpec((1, qb, LANES), lambda bi, p, i: (bi, i, p)),
        out_shape=jax.ShapeDtypeStruct((b, t, WC), BF16),
        scratch_shapes=[pltpu.VMEM((4 * qb, 1), F32), pltpu.VMEM((4 * qb, LANES), F32),
                        pltpu.VMEM((4 * qb, LANES), F32)],
        compiler_params=_cparams(("parallel", "parallel", "arbitrary")),
        name="diff_attn",
    )(lamv, gain2, btail, ccol, _front_mask_rows(), q, kp, vp, kt, vt)


def _diff_kernel_t(lam_ref, gain_ref, btail_ref, crow_ref, qt_ref, k_ref, vt_ref, o_ref,
                   m_ref, l_ref, acc_ref, *, qb, qb0, lam0):
    qbi = qb0 + pl.program_id(2)
    tail_start, n_far, n_front = _windows(qbi)
    qt = qt_ref[0]
    row = lax.broadcasted_iota(I32, (LANES, qb), 0)
    qst = jnp.concatenate(
        [jnp.where((row >= DC_HALF * c) & (row < DC_HALF * (c + 1)), qt, jnp.zeros_like(qt))
         for c in range(4)], axis=1)
    m_ref[...] = jnp.full(m_ref.shape, NEG, F32)
    l_ref[...] = jnp.zeros(l_ref.shape, F32)
    acc_ref[...] = jnp.zeros(acc_ref.shape, F32)
    a = DC_HALF ** -0.5 * LOG2E

    gw = 4 * qb // DIFF_GROUPS

    def steps(starts, nblk, add, far):
        wins = [_window_t(k_ref, vt_ref, start, nblk) for start in starts]
        raws = [_dot(k, qst) for k, _ in wins]
        for (_, vt), raw in zip(wins, raws):
            for g in range(DIFF_GROUPS):
                cols = slice(g * gw, (g + 1) * gw)
                x = raw[:, cols] * a
                if add is not None:
                    x = x + add(cols)
                _attn_step_t(x, crow_ref[0, :, cols] if far else 0.0, vt,
                             m_ref.at[g], l_ref.at[g], acc_ref.at[g])

    variant = jnp.where(qbi == 0, 1, 0)
    steps([tail_start], TAIL // KB, lambda cols: btail_ref[variant, :, cols], False)

    far_start = lambda j: _far_start(tail_start, j)
    n_full = n_far - 1

    def far_body(jj, carry):
        steps([far_start(2 * jj), far_start(2 * jj + 1)], SUP // KB, None, True)
        return carry

    lax.fori_loop(0, n_full // 2, far_body, 0)

    @pl.when(jnp.logical_and(n_full > 0, n_full % 2 == 1))
    def _():
        steps([far_start(n_full - 1)], SUP // KB, None, True)

    @pl.when(n_far >= 1)
    def _():
        fm = _front_mask_t(n_front, (SUP, gw))
        steps([far_start(n_far - 1)], SUP // KB, lambda cols: fm, True)

    acc = jnp.concatenate([acc_ref[g] for g in range(DIFF_GROUPS)], axis=1)
    l_sum = jnp.concatenate([jnp.sum(l_ref[g], axis=0, keepdims=True) for g in range(DIFF_GROUPS)], axis=1)

    lam_v = lam_ref[...]
    s1 = jnp.sum(lam_v[0:1] * lam_v[1:2], axis=1, keepdims=True)
    s2 = jnp.sum(lam_v[2:3] * lam_v[3:4], axis=1, keepdims=True)
    lam = jnp.exp(s1) - jnp.exp(s2) + lam0
    o4 = acc / l_sum
    o0 = o4[:, 0:qb] - lam * o4[:, qb:2 * qb]
    o1 = o4[:, 2 * qb:3 * qb] - lam * o4[:, 3 * qb:4 * qb]
    first = row < HEAD_DIM
    o = jnp.where(first, o0, o1)
    sq = o * o
    ms0 = jnp.sum(sq[0:HEAD_DIM], axis=0, keepdims=True)
    ms1 = jnp.sum(sq[HEAD_DIM:], axis=0, keepdims=True)
    ms = jnp.where(first, ms0, ms1) * (1.0 / HEAD_DIM)
    y = o * lax.rsqrt(ms + EPS) * gain_ref[...]
    o_ref[0] = (y * (1.0 - lam0)).astype(o_ref.dtype)


def _diff_attn_t(q, k, v, btail, cfar, lamv, gain2, *, qb, qb0, lam0):
    b, t, _ = q.shape
    s_tot = k.shape[1]
    nq = t // qb
    npair = N_HEADS_C // 2
    qt = q.transpose(0, 2, 1)
    vt = _blocked_t(v)
    crow = jnp.repeat(cfar.reshape(npair, 2), 2 * qb, axis=1).reshape(npair, 1, 4 * qb)
    bt = btail.reshape(2, npair, 2, qb, TAIL).transpose(0, 1, 4, 2, 3)
    bt = jnp.repeat(bt[:, :, :, :, None, :], 2, axis=4).reshape(2, npair, TAIL, 4 * qb)
    gain_t = jnp.broadcast_to(gain2.reshape(LANES, 1), (LANES, qb))
    kern = functools.partial(_diff_kernel_t, qb=qb, qb0=qb0, lam0=lam0)
    out_t = pl.pallas_call(
        kern,
        grid=(b, npair, nq),
        in_specs=[
            pl.BlockSpec((4, LANES), lambda bi, p, i: (0, 0)),
            pl.BlockSpec((LANES, qb), lambda bi, p, i: (0, 0)),
            pl.BlockSpec((2, None, TAIL, 4 * qb), lambda bi, p, i: (0, p, 0, 0)),
            pl.BlockSpec((1, 1, 4 * qb), lambda bi, p, i: (p, 0, 0)),
            pl.BlockSpec((1, LANES, qb), lambda bi, p, i: (bi, p, i)),
            _key_spec(s_tot, LANES, lambda bi, p, i: (bi, 0, p)),
            pl.BlockSpec((1, s_tot // KB, LANES, KB), lambda bi, p, i: (bi, 0, p, 0),
                         pipeline_mode=pl.Buffered(1)),
        ],
        out_specs=pl.BlockSpec((1, LANES, qb), lambda bi, p, i: (bi, p, i)),
        out_shape=jax.ShapeDtypeStruct((b, WC, t), BF16),
        scratch_shapes=[pltpu.VMEM((DIFF_GROUPS, 1, 4 * qb // DIFF_GROUPS), F32),
                        pltpu.VMEM((DIFF_GROUPS, 8, 4 * qb // DIFF_GROUPS), F32),
                        pltpu.VMEM((DIFF_GROUPS, LANES, 4 * qb // DIFF_GROUPS), F32)],
        compiler_params=_cparams(("parallel", "parallel", "arbitrary")),
        name="diff_attn_t",
    )(lamv, gain_t, bt, crow, qt, k, vt)
    return out_t.transpose(0, 2, 1)


def _sb_kernel(q_ref, k_ref, v_ref, *rest, qb, qb0, has_past):
    if has_past:
        kn_ref, vn_ref, o_ref, carry_ref, acc_ref = rest
    else:
        o_ref, carry_ref, acc_ref = rest
    npair = N_HEADS_B // 2
    qbi = qb0 + pl.program_id(1)
    lane = lax.broadcasted_iota(I32, (qb, LANES), 1)
    rows = lax.broadcasted_iota(I32, (2 * qb, KB), 0)
    causal = lax.broadcasted_iota(I32, (2 * qb, KB), 1) < jnp.where(rows >= qb, rows - qb, rows)
    tri = jnp.where(lax.broadcasted_iota(I32, (KB, KB), 0) > lax.broadcasted_iota(I32, (KB, KB), 1),
                    1.0, 0.0).astype(BF16)
    q = q_ref[0]
    qs = []
    for p in range(npair):
        part = q[:, p * LANES:(p + 1) * LANES]
        qs.append(jnp.concatenate([jnp.where(lane < HEAD_DIM, part, jnp.zeros_like(part)),
                                   jnp.where(lane < HEAD_DIM, jnp.zeros_like(part), part)], axis=0))
    carry_ref[...] = jnp.zeros(carry_ref.shape, F32)
    acc_ref[...] = jnp.zeros(acc_ref.shape, F32)

    def block(j, masked):
        if masked and has_past:
            k, v = kn_ref[0], vn_ref[0]
        else:
            off = pl.multiple_of(j * KB, KB)
            k = k_ref[0, pl.ds(off, KB), :].astype(BF16)
            v = v_ref[0, pl.ds(off, KB), :].astype(BF16)
        pairs = range(npair)
        zs = [_dot_nt(qs[p], k[:, p * LANES:(p + 1) * LANES]) for p in pairs]
        sps = [jnp.maximum(z, 0.0) + jnp.log1p(jnp.exp(-jnp.abs(z))) for z in zs]
        lgs = [-sp for sp in sps]
        if masked:
            lgs = [jnp.where(causal, lg, 0.0) for lg in lgs]
        tails = []
        for lg in lgs:
            l1 = lg.astype(BF16)
            r1 = lg - l1.astype(F32)
            l2 = r1.astype(BF16)
            l3 = (r1 - l2.astype(F32)).astype(BF16)
            tails.append(_dot(l1, tri) + _dot(l2, tri) + _dot(l3, tri))
        carries = [carry_ref[p] for p in pairs]
        ws = [jnp.exp(zs[p] - sps[p] + tails[p] + carries[p]) for p in pairs]
        if masked:
            ws = [jnp.where(causal, w, 0.0) for w in ws]
        pvs = [_dot(ws[p].astype(BF16), v[:, p * LANES:(p + 1) * LANES]) for p in pairs]
        worst = None
        for p in pairs:
            acc_ref[p] += pvs[p]
            carry_new = carries[p] + jnp.sum(lgs[p], axis=1, keepdims=True)
            carry_ref[p] = carry_new
            worst = carry_new if worst is None else jnp.maximum(worst, carry_new)
        return jnp.max(worst)

    mx0 = block(qbi, True)

    def cond(st):
        return jnp.logical_and(st[0] >= 0, st[1] > EXP_ZERO)

    def body(st):
        return st[0] - 1, block(st[0], False)

    lax.while_loop(cond, body, (qbi - 1, mx0))
    for p in range(npair):
        acc = acc_ref[p]
        o_ref[0, :, p * LANES:(p + 1) * LANES] = jnp.where(
            lane < HEAD_DIM, acc[0:qb], acc[qb:2 * qb]).astype(o_ref.dtype)


def _sb_attn(q, k, v, new=None, *, qb, qb0):
    b, t, _ = q.shape
    s_pad = k.shape[1]
    nq = t // qb
    npair = N_HEADS_B // 2
    kern = functools.partial(_sb_kernel, qb=qb, qb0=qb0, has_past=new is not None)
    new_specs = [] if new is None else [pl.BlockSpec((1, KB, WB), lambda bi, i: (bi, 0, 0))] * 2
    return pl.pallas_call(
        kern,
        grid=(b, nq),
        in_specs=[
            pl.BlockSpec((1, qb, WB), lambda bi, i: (bi, i, 0)),
            _key_spec(s_pad, WB, lambda bi, i: (bi, 0, 0)),
            _key_spec(s_pad, WB, lambda bi, i: (bi, 0, 0)),
        ] + new_specs,
        out_specs=pl.BlockSpec((1, qb, WB), lambda bi, i: (bi, i, 0)),
        out_shape=jax.ShapeDtypeStruct((b, t, WB), BF16),
        scratch_shapes=[pltpu.VMEM((npair, 2 * qb, 1), F32), pltpu.VMEM((npair, 2 * qb, LANES), F32)],
        compiler_params=_cparams(("parallel", "arbitrary")),
        name="sb_attn",
    )(q, k, v, *(() if new is None else new))


def _dsa_kernel(btail_ref, cfar_ref, fmask_ref, qa_ref, qi_ref, wi_ref, kap_ref, vap_ref, kip_ref,
                kat_ref, vat_ref, kit_ref, o_ref, kxt_ref, kxf_ref, wb_ref, r_ref, cnt_ref, cs_ref,
                m_ref, l_ref, acc_ref, *, qb, qb0, topk, idx_bits):
    npair = N_HEADS_A // 2
    qbi = qb0 + pl.program_id(1)
    tail_start, n_far, n_front = _windows(qbi)
    lane = lax.broadcasted_iota(I32, (qb, LANES), 1)

    def past(ref, start, width):
        off = pl.multiple_of(start, KB)
        return ref[0, pl.ds(off, width), :].astype(BF16)

    qi = qi_ref[0]
    parts = []
    for h in range(IDX_HEADS):
        g, sh = divmod(h * IDX_DIM, LANES)
        part = qi[:, g * LANES:(g + 1) * LANES]
        if sh:
            part = pltpu.roll(part, LANES - sh, 1)
        parts.append(jnp.where(lane < IDX_DIM, part, 0.0).astype(BF16))
    qis = jnp.concatenate(parts, axis=0)
    wi = wi_ref[0] * (IDX_HEADS ** -0.5 * IDX_DIM ** -0.5)
    for h in range(IDX_HEADS):
        wb_ref[h * qb:(h + 1) * qb, :] = jnp.broadcast_to(wi[:, IDX_DIM + h:IDX_DIM + h + 1],
                                                           (qb, LANES))
    qa = qa_ref[0]
    qas = []
    for p in range(npair):
        part = qa[:, p * LANES:(p + 1) * LANES]
        qas.append(jnp.concatenate(
            [jnp.where(lane < HEAD_DIM, part, jnp.zeros_like(part)),
             jnp.where(lane < HEAD_DIM, jnp.zeros_like(part), part)], axis=0))

    qis_past = qis[:, 0:IDX_DIM]

    def keys_of(start, width, hidden):
        if width == TAIL:
            d = _dot_nt(qis, kit_ref[0])
        else:
            d = _dot_nt(qis_past, past(kip_ref, start, width))
        wb = wb_ref[...]
        cols = []
        for dc in _lane_chunks(d):
            r = jnp.maximum(dc, 0.0) * wb
            cols.append(functools.reduce(jnp.add, [r[h * qb:(h + 1) * qb] for h in range(IDX_HEADS)]))
        sc = jnp.concatenate(cols, axis=1)
        sc = jnp.where(sc == 0.0, 0.0, sc)
        bits = lax.bitcast_convert_type(sc, I32)
        key = bits ^ ((bits >> 31) & 0x7FFFFFFF)
        return jnp.where(hidden, KEY_NEG_INF, key)

    bt0 = btail_ref[jnp.where(qbi == 0, 1, 0), 0]
    kxt_ref[...] = keys_of(tail_start, TAIL, bt0 < 0.5 * NEG)

    def score_body(j, carry):
        kxf_ref[j] = keys_of(_far_start(tail_start, j), SUP, False)
        return carry

    lax.fori_loop(0, n_far - 1, score_body, 0)

    @pl.when(n_far >= 1)
    def _():
        kxf_ref[n_far - 1] = keys_of(_far_start(tail_start, n_far - 1), SUP, fmask_ref[n_front] < 0.5 * NEG)

    col_t = lax.broadcasted_iota(I32, (qb, TAIL), 1)
    col_f = lax.broadcasted_iota(I32, (qb, SUP), 1)

    def count(pred_t, pred_f):
        def body(j, acc):
            hit = jnp.where(pred_f(kxf_ref[j], col_f + (_far_start(tail_start, j))), 1.0, 0.0)
            return acc + functools.reduce(jnp.add, _lane_chunks(hit))
        acc = functools.reduce(jnp.add, _lane_chunks(
            jnp.where(pred_t(kxt_ref[...], col_t + tail_start), 1.0, 0.0)))
        acc = lax.fori_loop(0, n_far, body, acc)
        return jnp.sum(acc, axis=1, keepdims=True)

    def count1(make_pred):
        return count(make_pred(TAIL), make_pred(SUP))

    wide = lambda a, w: jnp.broadcast_to(a, (qb, w))

    def ge(a):
        return lambda w: (lambda kx, idx, aw=wide(a, w): kx >= aw)

    def gt(a):
        return lambda w: (lambda kx, idx, aw=wide(a, w): kx > aw)

    def eq(a):
        return lambda w: (lambda kx, idx, aw=wide(a, w): kx == aw)

    def eq_before(a, c):
        return lambda w: (lambda kx, idx, aw=wide(a, w), cw=wide(c, w): (kx == aw) & (idx < cw))

    r_ref[...] = jnp.full((qb, 1), INT_MIN, I32)
    cnt_ref[...] = jnp.full((qb, 1), float(TAIL + SUP * 64), F32)

    def bit_cond(st):
        return jnp.logical_and(st[0] < 32, st[1] > 0.0)

    def bit_body(st):
        t = st[0]
        r = r_ref[...]
        cand = r + lax.shift_left(jnp.int32(1), 31 - t)
        cnt = count1(ge(cand))
        take = cnt >= topk
        r_ref[...] = jnp.where(take, cand, r)
        cnt_new = jnp.where(take, cnt, cnt_ref[...])
        cnt_ref[...] = cnt_new
        return t + 1, jnp.max(jnp.where(cnt_new == topk, 0.0, 1.0))

    lax.while_loop(bit_cond, bit_body, (jnp.int32(0), jnp.float32(1.0)))
    r1 = r_ref[...]

    need = topk - count1(gt(r1))
    n_eq = count1(eq(r1))
    cs_ref[...] = jnp.full((qb, 1), 2 ** idx_bits - 1, I32)
    overflow = jnp.where((n_eq > need) & (r1 > KEY_NEG_INF), 1.0, 0.0)

    @pl.when(jnp.max(overflow) > 0.0)
    def _():
        def tie_body(t, c):
            cand = c + lax.shift_left(jnp.int32(1), idx_bits - 1 - t)
            cnt = count1(eq_before(r1, cand))
            return jnp.where(cnt < need, cand, c)
        cs_ref[...] = lax.fori_loop(0, idx_bits, tie_body, jnp.zeros((qb, 1), I32))

    cs1 = cs_ref[...]

    m_ref[...] = jnp.full(m_ref.shape, NEG, F32)
    l_ref[...] = jnp.zeros(l_ref.shape, F32)
    acc_ref[...] = jnp.zeros(acc_ref.shape, F32)

    def sel_mask(kx, idx, w):
        sel = (kx > wide(r1, w)) | ((kx == wide(r1, w)) & (idx <= wide(cs1, w)))
        return jnp.where(sel & (kx > KEY_NEG_INF), 0.0, NEG)

    def attend(start, width, madd, bias, ccols):
        if width == TAIL:
            k, v = kat_ref[0], vat_ref[0]
        else:
            k, v = past(kap_ref, start, width), past(vap_ref, start, width)
        for p in range(npair):
            x = (_dot_nt(qas[p], k[:, p * LANES:(p + 1) * LANES]) * LOG2E).reshape(2, qb, width) + madd[None]
            if bias is not None:
                x = x + bias[2 * p:2 * p + 2]
            _attn_step(x.reshape(2 * qb, width), ccols[p], v[:, p * LANES:(p + 1) * LANES],
                       m_ref.at[p], l_ref.at[p], acc_ref.at[p])

    bt = btail_ref[jnp.where(qbi == 0, 1, 0)]
    attend(tail_start, TAIL, sel_mask(kxt_ref[...], col_t + tail_start, TAIL), bt, [0.0] * npair)

    ccols = [cfar_ref[p] for p in range(npair)]

    def far_body(j, carry):
        start = _far_start(tail_start, j)
        attend(start, SUP, sel_mask(kxf_ref[j], col_f + start, SUP), None, ccols)
        return carry

    lax.fori_loop(0, n_far, far_body, 0)

    for p in range(npair):
        o2 = acc_ref[p] / jnp.sum(l_ref[p], axis=1, keepdims=True)
        o = jnp.where(lane < HEAD_DIM, o2[0:qb], o2[qb:2 * qb])
        o_ref[0, :, p * LANES:(p + 1) * LANES] = o.astype(o_ref.dtype)


def _dsa_attn(qa, qi, misc_f32, ka, va, ki, btail, cfar, *, qb, qb0, topk):
    (kap, kat), (vap, vat), (kip, kit) = ka, va, ki
    b, t, _ = qa.shape
    p_len = kap.shape[1]
    s_tot = p_len + KB
    nq = t // qb
    assert nq == 1
    npair = N_HEADS_A // 2
    n_far_max = max(1, -(-(s_tot - TAIL) // SUP))
    assert n_far_max <= 64
    idx_bits = int(math.ceil(math.log2(s_tot))) + 1
    past_spec = lambda w: pl.BlockSpec((1, p_len, w), lambda bi, i: (bi, 0, 0),
                                       pipeline_mode=pl.Buffered(1))
    tail_spec = lambda w: pl.BlockSpec((1, TAIL, w), lambda bi, i: (bi, 0, 0))
    ccol = jnp.repeat(cfar.reshape(npair, 2), qb, axis=1).reshape(npair, 2 * qb, 1)
    kern = functools.partial(_dsa_kernel, qb=qb, qb0=qb0, topk=float(topk), idx_bits=idx_bits)
    return pl.pallas_call(
        kern,
        grid=(b, nq),
        in_specs=[
            pl.BlockSpec((2, N_HEADS_A, qb, TAIL), lambda bi, i: (0, 0, 0, 0)),
            pl.BlockSpec((npair, 2 * qb, 1), lambda bi, i: (0, 0, 0)),
            pl.BlockSpec((SUP // KB, 1, SUP), lambda bi, i: (0, 0, 0)),
            pl.BlockSpec((1, qb, WA), lambda bi, i: (bi, i, 0)),
            pl.BlockSpec((1, qb, WQI), lambda bi, i: (bi, i, 0)),
            pl.BlockSpec((1, qb, LANES), lambda bi, i: (bi, i, 0)),
            past_spec(WA), past_spec(WA), past_spec(IDX_DIM),
            tail_spec(WA), tail_spec(WA), tail_spec(LANES),
        ],
        out_specs=pl.BlockSpec((1, qb, WA), lambda bi, i: (bi, i, 0)),
        out_shape=jax.ShapeDtypeStruct((b, t, WA), BF16),
        scratch_shapes=[
            pltpu.VMEM((qb, TAIL), I32),
            pltpu.VMEM((n_far_max, qb, SUP), I32),
            pltpu.VMEM((IDX_HEADS * qb, LANES), F32),
            pltpu.VMEM((qb, 1), I32),
            pltpu.VMEM((qb, 1), F32),
            pltpu.VMEM((qb, 1), I32),
            pltpu.VMEM((npair, 2 * qb, 1), F32),
            pltpu.VMEM((npair, 2 * qb, LANES), F32),
            pltpu.VMEM((npair, 2 * qb, LANES), F32),
        ],
        compiler_params=_cparams(("parallel", "arbitrary")),
        name="dsa_attn",
    )(btail, ccol, _front_mask_rows(), qa, qi, misc_f32, kap, vap, kip, kat, vat, kit)


def _dsa_kernel_t(btail_ref, crow_ref, qat_ref, qit_ref, wit_ref, ka_ref, vat_ref, ki_ref, o_ref,
                  kxt_ref, kxf_ref, m_ref, l_ref, acc_ref, *, qb, qb0, topk, idx_bits):
    npair = N_HEADS_A // 2
    qbi = qb0 + pl.program_id(1)
    tail_start, n_far, n_front = _windows(qbi)
    row = lax.broadcasted_iota(I32, (LANES, qb), 0)

    qit = qit_ref[0]
    zpad = jnp.zeros((LANES - IDX_DIM, qb), BF16)
    qist = jnp.concatenate(
        [jnp.concatenate([qit[h * IDX_DIM:(h + 1) * IDX_DIM].astype(BF16), zpad], axis=0)
         for h in range(IDX_HEADS)], axis=1)
    wit = wit_ref[0] * (IDX_HEADS ** -0.5 * IDX_DIM ** -0.5)
    qat = qat_ref[0]
    qast = []
    for p in range(npair):
        part = qat[p * LANES:(p + 1) * LANES]
        qast.append(jnp.concatenate(
            [jnp.where(row < HEAD_DIM, part, jnp.zeros_like(part)),
             jnp.where(row < HEAD_DIM, jnp.zeros_like(part), part)], axis=1))

    def ki_window(start, width):
        return ki_ref[0, pl.ds(pl.multiple_of(start, KB), width), :]

    def keys_of(start, width, hidden):
        d = _dot(ki_window(start, width), qist)
        sc = functools.reduce(jnp.add, [
            jnp.maximum(d[:, h * qb:(h + 1) * qb], 0.0) * wit[h:h + 1] for h in range(IDX_HEADS)])
        sc = jnp.where(sc == 0.0, 0.0, sc)
        bits = lax.bitcast_convert_type(sc, I32)
        key = bits ^ ((bits >> 31) & 0x7FFFFFFF)
        return key if hidden is None else jnp.where(hidden, KEY_NEG_INF, key)

    variant = jnp.where(qbi == 0, 1, 0)
    kxt_ref[...] = keys_of(tail_start, TAIL, btail_ref[variant, 0, :, 0:qb] < 0.5 * NEG)

    def score_body(j, carry):
        kxf_ref[j] = keys_of(_far_start(tail_start, j), SUP, None)
        return carry

    lax.fori_loop(0, n_far - 1, score_body, 0)

    @pl.when(n_far >= 1)
    def _():
        kxf_ref[n_far - 1] = keys_of(_far_start(tail_start, n_far - 1), SUP,
                                     lax.broadcasted_iota(I32, (SUP, qb), 0) >= SUP - n_front * KB)

    row_t = lax.broadcasted_iota(I32, (TAIL, qb), 0)
    row_f = lax.broadcasted_iota(I32, (SUP, qb), 0)

    def hits(pred, kx, idx):
        return _fold_rows(jnp.add, jnp.where(pred(kx, idx), 1.0, 0.0))

    def count(pred):
        def body(j, acc):
            return acc + hits(pred, kxf_ref[j], row_f + (_far_start(tail_start, j)))
        acc = lax.fori_loop(0, n_far, body, hits(pred, kxt_ref[...], row_t + tail_start))
        return jnp.sum(acc, axis=0, keepdims=True)

    def bit_body(t, r):
        cand = r + lax.shift_left(jnp.int32(1), 31 - t)
        return jnp.where(count(lambda kx, idx: kx >= cand) >= topk, cand, r)

    r1 = lax.fori_loop(0, 32, bit_body, jnp.full((1, qb), INT_MIN, I32))

    n_ge = count(lambda kx, idx: kx >= r1)
    overflow = jnp.where((n_ge > topk) & (r1 > KEY_NEG_INF), 1.0, 0.0)

    def tie_search(_):
        need = topk - count(lambda kx, idx: kx > r1)

        def tie_body(t, c):
            cand = c + lax.shift_left(jnp.int32(1), idx_bits - 1 - t)
            cnt = count(lambda kx, idx: (kx == r1) & (idx < cand))
            return jnp.where(cnt < need, cand, c)
        return lax.fori_loop(0, idx_bits, tie_body, jnp.zeros((1, qb), I32))

    cs1 = lax.cond(jnp.max(overflow) > 0.0, tie_search,
                   lambda _: jnp.full((1, qb), 2 ** idx_bits - 1, I32), 0)

    m_ref[...] = jnp.full(m_ref.shape, NEG, F32)
    l_ref[...] = jnp.zeros(l_ref.shape, F32)
    acc_ref[...] = jnp.zeros(acc_ref.shape, F32)

    def sel_mask(kx, idx):
        sel = (kx > r1) | ((kx == r1) & (idx <= cs1))
        madd = jnp.where(sel & (kx > KEY_NEG_INF), 0.0, NEG)
        return jnp.concatenate([madd, madd], axis=1)

    def attend(starts, nblk, madds, bias, crows):
        wins = [_window_t(ka_ref, vat_ref, start, nblk) for start in starts]
        raws = [[_dot(k[:, p * LANES:(p + 1) * LANES], qast[p]) for p in range(npair)] for k, _ in wins]
        for (_, vt), raw, madd2 in zip(wins, raws, madds):
            for p in range(npair):
                x = raw[p] * LOG2E + madd2
                if bias is not None:
                    x = x + bias[:, 2 * p * qb:2 * (p + 1) * qb]
                _attn_step_t(x, crows[p], vt[p * LANES:(p + 1) * LANES],
                             m_ref.at[p], l_ref.at[p], acc_ref.at[p])

    attend([tail_start], TAIL // KB, [sel_mask(kxt_ref[...], row_t + tail_start)],
           btail_ref[variant, 0], [0.0] * npair)

    crows = [crow_ref[p] for p in range(npair)]
    far_start = lambda j: _far_start(tail_start, j)

    def attend_far(js):
        attend([far_start(j) for j in js], SUP // KB,
               [sel_mask(kxf_ref[j], row_f + far_start(j)) for j in js], None, crows)

    def far_body(jj, carry):
        attend_far([2 * jj, 2 * jj + 1])
        return carry

    lax.fori_loop(0, n_far // 2, far_body, 0)

    @pl.when(n_far % 2 == 1)
    def _():
        attend_far([n_far - 1])

    for p in range(npair):
        o2 = acc_ref[p] / jnp.sum(l_ref[p], axis=0, keepdims=True)
        o = jnp.where(row < HEAD_DIM, o2[:, 0:qb], o2[:, qb:2 * qb])
        o_ref[0, p * LANES:(p + 1) * LANES, :] = o.astype(o_ref.dtype)


def _dsa_attn_t(qa, qi, misc_f32, ka, va, ki, btail, cfar, *, qb, qb0, topk):
    b, t, _ = qa.shape
    s_tot = ka.shape[1]
    nq = t // qb
    npair = N_HEADS_A // 2
    n_far_max = max(1, -(-(s_tot - TAIL) // SUP))
    idx_bits = int(math.ceil(math.log2(s_tot))) + 1
    qat = qa.transpose(0, 2, 1)
    qit = qi.transpose(0, 2, 1)
    wit = misc_f32[:, :, IDX_DIM:IDX_DIM + IDX_HEADS].transpose(0, 2, 1)
    vat = _blocked_t(va)
    crow = jnp.repeat(cfar.reshape(npair, 2), qb, axis=1).reshape(npair, 1, 2 * qb)
    bt = btail.transpose(0, 3, 1, 2).reshape(2, 1, TAIL, N_HEADS_A * qb)
    kern = functools.partial(_dsa_kernel_t, qb=qb, qb0=qb0, topk=float(topk), idx_bits=idx_bits)
    out_t = pl.pallas_call(
        kern,
        grid=(b, nq),
        in_specs=[
            pl.BlockSpec((2, 1, TAIL, N_HEADS_A * qb), lambda bi, i: (0, 0, 0, 0)),
            pl.BlockSpec((npair, 1, 2 * qb), lambda bi, i: (0, 0, 0)),
            pl.BlockSpec((1, WA, qb), lambda bi, i: (bi, 0, i)),
            pl.BlockSpec((1, WQI, qb), lambda bi, i: (bi, 0, i)),
            pl.BlockSpec((1, IDX_HEADS, qb), lambda bi, i: (bi, 0, i)),
            _key_spec(s_tot, WA, lambda bi, i: (bi, 0, 0)),
            pl.BlockSpec((1, s_tot // KB, WA, KB), lambda bi, i: (bi, 0, 0, 0),
                         pipeline_mode=pl.Buffered(1)),
            _key_spec(s_tot, LANES, lambda bi, i: (bi, 0, 0)),
        ],
        out_specs=pl.BlockSpec((1, WA, qb), lambda bi, i: (bi, 0, i)),
        out_shape=jax.ShapeDtypeStruct((b, WA, t), BF16),
        scratch_shapes=[
            pltpu.VMEM((TAIL, qb), I32),
            pltpu.VMEM((n_far_max, SUP, qb), I32),
            pltpu.VMEM((npair, 1, 2 * qb), F32),
            pltpu.VMEM((npair, 8, 2 * qb), F32),
            pltpu.VMEM((npair, LANES, 2 * qb), F32),
        ],
        compiler_params=_cparams(("parallel", "arbitrary")),
        name="dsa_attn_t",
    )(bt, crow, qat, qit, wit, ka, vat, ki)
    return out_t.transpose(0, 2, 1)


def _outproj_kernel(oa_ref, ob_ref, oc_ref, wa_ref, wb_ref, wc_ref, g_ref, x_ref, y_ref):
    mix = _dot(oa_ref[...], wa_ref[...]) + _dot(ob_ref[...], wb_ref[...]) + _dot(oc_ref[...], wc_ref[...])
    ms = jnp.mean(mix * mix, axis=-1, keepdims=True)
    y_ref[...] = x_ref[...] + mix * lax.rsqrt(ms + EPS) * g_ref[...]


def _outproj(oa, ob, oc, w_out, g, x2d):
    m, d = x2d.shape
    tm = min(512, m)
    w = w_out.astype(BF16)
    row = lambda i: (i, 0)
    const = lambda i: (0, 0)
    return pl.pallas_call(
        _outproj_kernel,
        grid=(m // tm,),
        in_specs=[pl.BlockSpec((tm, WA), row), pl.BlockSpec((tm, WB), row), pl.BlockSpec((tm, WC), row),
                  pl.BlockSpec((WA, d), const), pl.BlockSpec((WB, d), const), pl.BlockSpec((WC, d), const),
                  pl.BlockSpec((1, d), const), pl.BlockSpec((tm, d), row)],
        out_specs=pl.BlockSpec((tm, d), row),
        out_shape=jax.ShapeDtypeStruct((m, d), F32),
        compiler_params=_cparams(("parallel",)),
        name="outproj",
    )(oa, ob, oc, w[:WA], w[WA:WA + WB], w[WA + WB:], g.reshape(1, d), x2d)


HALO = 16


def _ffn_kernel(x_ref, xh_ref, stg_ref, stv_ref, gpre_ref, gpost_ref, wg_ref, wv_ref, cwg_ref,
                cwv_ref, cbg_ref, cbv_ref, wd_ref, y_ref, convg_ref, convv_ref,
                h_ref, ug_ref, uv_ref, f_ref, *, tm, tiles_per_seq):
    i = pl.program_id(0)
    f = pl.program_id(1)
    first_of_seq = (i % tiles_per_seq) == 0

    def norm(x):
        ms = jnp.mean(x * x, axis=-1, keepdims=True)
        return (x * lax.rsqrt(ms + EPS) * gpre_ref[...]).astype(BF16)

    @pl.when(f == 0)
    def _():
        h_ref[0:HALO, :] = norm(xh_ref[...])
        h_ref[HALO:, :] = norm(x_ref[...])
        f_ref[...] = jnp.zeros(f_ref.shape, F32)

    h = h_ref[...]
    ug_ref[...] = _dot(h, wg_ref[...])
    uv_ref[...] = _dot(h, wv_ref[...])

    @pl.when(first_of_seq)
    def _():
        ug_ref[0:HALO, :] = stg_ref[0]
        uv_ref[0:HALO, :] = stv_ref[0]

    def conv(u_ref, cw_ref, cb_ref):
        u = u_ref[...]
        u1 = pltpu.roll(u, 1, 0)[HALO:, :]
        u2 = pltpu.roll(u, 2, 0)[HALO:, :]
        cw = cw_ref[...]
        return cb_ref[...] + (cw[0:1] * u2 + cw[1:2] * u1 + cw[2:3] * u[HALO:, :])

    gate = conv(ug_ref, cwg_ref, cbg_ref)
    val = conv(uv_ref, cwv_ref, cbv_ref)
    c0 = math.sqrt(2.0 / math.pi)
    gelu = 0.5 * gate * (1.0 + jnp.tanh(c0 * (gate + 0.044715 * (gate * gate * gate))))
    f_ref[...] += _dot((gelu * val).astype(BF16), wd_ref[...])

    convg_ref[0] = ug_ref[tm:, :]
    convv_ref[0] = uv_ref[tm:, :]

    @pl.when(f == pl.num_programs(1) - 1)
    def _():
        ff = f_ref[...]
        ms = jnp.mean(ff * ff, axis=-1, keepdims=True)
        y_ref[...] = x_ref[...] + ff * lax.rsqrt(ms + EPS) * gpost_ref[...]


def _ffn(x2d, state, g_pre, g_post, w_up, conv_w, conv_b, w_down, *, seq):
    m, d = x2d.shape
    b = m // seq
    d_ff = w_down.shape[0]
    tn = 1408
    assert d_ff % tn == 0 and tn % LANES == 0
    nf = d_ff // tn
    tm = min(512, seq)
    assert seq % tm == 0 and tm % HALO == 0
    tiles_per_seq = seq // tm
    n_tiles = m // tm
    st = jnp.pad(state.astype(F32), ((0, 0), (HALO - (CONV_W - 1), 0), (0, 0)))
    wu = w_up.astype(BF16)
    wd = w_down.astype(BF16)
    kern = functools.partial(_ffn_kernel, tm=tm, tiles_per_seq=tiles_per_seq)
    halo_blocks = tm // HALO
    y, conv_g, conv_v = pl.pallas_call(
        kern,
        grid=(n_tiles, nf),
        in_specs=[
            pl.BlockSpec((tm, d), lambda i, f: (i, 0)),
            pl.BlockSpec((HALO, d), lambda i, f: (jnp.maximum(i * halo_blocks - 1, 0), 0)),
            pl.BlockSpec((1, HALO, tn), lambda i, f: (i // tiles_per_seq, 0, f)),
            pl.BlockSpec((1, HALO, tn), lambda i, f: (i // tiles_per_seq, 0, nf + f)),
            pl.BlockSpec((1, d), lambda i, f: (0, 0)),
            pl.BlockSpec((1, d), lambda i, f: (0, 0)),
            pl.BlockSpec((d, tn), lambda i, f: (0, f)),
            pl.BlockSpec((d, tn), lambda i, f: (0, nf + f)),
            pl.BlockSpec((CONV_W, tn), lambda i, f: (0, f)),
            pl.BlockSpec((CONV_W, tn), lambda i, f: (0, nf + f)),
            pl.BlockSpec((1, tn), lambda i, f: (0, f)),
            pl.BlockSpec((1, tn), lambda i, f: (0, nf + f)),
            pl.BlockSpec((tn, d), lambda i, f: (f, 0)),
        ],
        out_specs=[pl.BlockSpec((tm, d), lambda i, f: (i, 0)),
                   pl.BlockSpec((1, HALO, tn), lambda i, f: (i, 0, f)),
                   pl.BlockSpec((1, HALO, tn), lambda i, f: (i, 0, f))],
        out_shape=[jax.ShapeDtypeStruct((m, d), F32),
                   jax.ShapeDtypeStruct((n_tiles, HALO, d_ff), F32),
                   jax.ShapeDtypeStruct((n_tiles, HALO, d_ff), F32)],
        scratch_shapes=[pltpu.VMEM((HALO + tm, d), BF16),
                        pltpu.VMEM((HALO + tm, tn), F32),
                        pltpu.VMEM((HALO + tm, tn), F32),
                        pltpu.VMEM((tm, d), F32)],
        compiler_params=_cparams(("parallel", "arbitrary")),
        name="ffn",
    )(x2d, x2d, st, st, g_pre.reshape(1, d), g_post.reshape(1, d), wu, wu,
      conv_w, conv_w, conv_b.reshape(1, -1), conv_b.reshape(1, -1), wd)
    last = lambda c: c.reshape(b, tiles_per_seq, HALO, d_ff)[:, -1, HALO - (CONV_W - 1):]
    return y, jnp.concatenate([last(conv_g), last(conv_v)], axis=-1)


def _lambda_init(l):
    return 0.8 - 0.6 * math.exp(-0.3 * l)


def _past_and_tail(past, new):
    b, t, w = new.shape
    p = past.shape[1]
    past = past.reshape(b, p, -1)
    last = past[:, p - (TAIL - KB):].astype(BF16)
    if last.shape[2] < w:
        last = jnp.pad(last, ((0, 0), (0, 0), (0, w - last.shape[2])))
    return past, jnp.concatenate([last, new, jnp.zeros((b, KB - t, w), BF16)], axis=1)


def _layer(x, past, l, prm):
    b, t, d = x.shape
    p_len = 0 if past is None else past["k_a"].shape[1]
    s = p_len + t
    qb = min(KB, t)
    assert p_len % KB == 0 and t % qb == 0 and (qb == KB or t == qb)
    qb0 = p_len // KB
    s_pad = -(-s // KB) * KB
    assert s_pad >= max(TAIL, SUP)
    n_valid = s - (s_pad - KB)
    topk = min(TOPK_MAX, s // 4)
    m = b * t

    pr = _proj(x.reshape(m, d), prm["g_pre_mix"][l], _pad_w_in(prm["w_in"][l]))
    r3 = lambda a: a.reshape(b, t, a.shape[-1])

    btail_a, cfar_a = _bias_tiles(prm["rel_bias"][:, :N_HEADS_A], qb, n_valid)
    btail_c, cfar_c = _bias_tiles(prm["rel_bias"][:, N_HEADS_A:], qb, n_valid)

    lamv = jnp.pad(jnp.stack([prm["lambda_q1"][l], prm["lambda_k1"][l],
                              prm["lambda_q2"][l], prm["lambda_k2"][l]]).astype(F32),
                   ((0, 0), (0, LANES - DC_HALF)))
    gain2 = jnp.tile(prm["subln_gain"][l].astype(F32), 2).reshape(1, LANES)
    if past is None:
        assert qb == KB
        o_a = _dsa_attn_t(r3(pr["qa_bf16"]), r3(pr["qi_f32"]), r3(pr["misc_f32"]),
                          r3(pr["ka_bf16"]), r3(pr["va_bf16"]), r3(pr["misc_bf16"]),
                          btail_a, cfar_a, qb=qb, qb0=qb0, topk=topk)
        o_b = _sb_attn(r3(pr["qb_bf16"]), r3(pr["kb_bf16"]), r3(pr["vb_bf16"]), qb=qb, qb0=qb0)
        o_c = _diff_attn_t(r3(pr["qc_bf16"]), r3(pr["kc_bf16"]), r3(pr["vc_bf16"]),
                           btail_c, cfar_c, lamv, gain2, qb=qb, qb0=qb0, lam0=_lambda_init(l))
    else:
        assert t == qb and p_len >= max(TAIL - KB, SUP)
        o_a = _dsa_attn(r3(pr["qa_bf16"]), r3(pr["qi_f32"]), r3(pr["misc_f32"]),
                        _past_and_tail(past["k_a"], r3(pr["ka_bf16"])),
                        _past_and_tail(past["v_a"], r3(pr["va_bf16"])),
                        _past_and_tail(past["k_i"], r3(pr["misc_bf16"])),
                        btail_a, cfar_a, qb=qb, qb0=qb0, topk=topk)
        diag = lambda new: jnp.concatenate([new, jnp.zeros((b, KB - t, new.shape[2]), BF16)], axis=1)
        o_b = _sb_attn(r3(pr["qb_bf16"]), past["k_b"].reshape(b, p_len, WB),
                       past["v_b"].reshape(b, p_len, WB),
                       (diag(r3(pr["kb_bf16"])), diag(r3(pr["vb_bf16"]))), qb=qb, qb0=qb0)
        o_c = _diff_attn(r3(pr["qc_bf16"]),
                         _past_and_tail(past["k_c"], r3(pr["kc_bf16"])),
                         _past_and_tail(past["v_c"], r3(pr["vc_bf16"])),
                         btail_c, cfar_c, lamv, gain2, qb=qb, qb0=qb0, lam0=_lambda_init(l))

    x1 = _outproj(o_a.reshape(m, WA), o_b.reshape(m, WB), o_c.reshape(m, WC),
                  prm["w_out"][l], prm["g_post_mix"][l], x.reshape(m, d))
    state = (jnp.zeros((b, CONV_W - 1, prm["w_up"].shape[2]), F32) if past is None
             else past["conv"])
    x2, new_conv = _ffn(x1, state, prm["g_pre_ffn"][l], prm["g_post_ffn"][l], prm["w_up"][l],
                        prm["conv_w"][l], prm["conv_b"][l], prm["w_down"][l], seq=t)
    new = (pr["ka_f32"], pr["va_f32"], pr["misc_f32"][:, :IDX_DIM], pr["kb_f32"], pr["vb_f32"],
           pr["kc_f32"], pr["vc_f32"], new_conv)
    return x2.reshape(b, t, d), new


def _run_trunk(x, caches, prm):
    depth = prm["w_in"].shape[0]
    b, t, _ = x.shape
    news = []
    for l in range(depth):
        past = None if caches is None else {k: v[l] for k, v in caches.items()}
        x, new = _layer(x, past, l, prm)
        news.append(new)
    stacked = [jnp.stack([n[i] for n in news]) for i in range(len(news[0]))]
    heads = (N_HEADS_A, N_HEADS_A, None, N_HEADS_B, N_HEADS_B, N_HEADS_C, N_HEADS_C)
    outs = [s.reshape(depth, b, t, IDX_DIM) if h is None else s.reshape(depth, b, t, h, HEAD_DIM)
            for s, h in zip(stacked[:-1], heads)]
    return x, outs + [stacked[-1]]


def kernel(x_prompt, x_sample, cache_k_a, cache_v_a, cache_idx_k, cache_k_b, cache_v_b, cache_k_c,
           cache_v_c, state_ffn_conv, w_in, w_out, rel_bias, lambda_q1, lambda_k1, lambda_q2,
           lambda_k2, subln_gain, g_pre_mix, g_post_mix, g_pre_ffn, g_post_ffn, w_up, conv_w,
           conv_b, w_down):
    prm = dict(w_in=w_in, w_out=w_out, rel_bias=rel_bias, lambda_q1=lambda_q1, lambda_k1=lambda_k1,
               lambda_q2=lambda_q2, lambda_k2=lambda_k2, subln_gain=subln_gain, g_pre_mix=g_pre_mix,
               g_post_mix=g_post_mix, g_pre_ffn=g_pre_ffn, g_post_ffn=g_post_ffn, w_up=w_up,
               conv_w=conv_w, conv_b=conv_b, w_down=w_down)
    y_prompt, p_new = _run_trunk(x_prompt, None, prm)
    caches = dict(k_a=cache_k_a, v_a=cache_v_a, k_i=cache_idx_k, k_b=cache_k_b, v_b=cache_v_b,
                  k_c=cache_k_c, v_c=cache_v_c, conv=state_ffn_conv)
    y_sample, s_new = _run_trunk(x_sample, caches, prm)
    return (y_prompt, y_sample, *p_new, *s_new)
```
